```python
import math
import jax
import jax.numpy as jnp
from jax import lax
import numpy as np

D_MODEL = 1024
BATCH = 8
SEQ = 4096
DEPTH = 2
DEC_BATCH = 8
DEC_SEQ = 64
PAST_LEN = 1024

CHUNK = 64
N_META = 16
N_EVEN = (DEPTH + 1) // 2
N_ODD = DEPTH // 2
A_HEADS = 4
A_DK = 128
A_WIDTH = D_MODEL // 2
A_DV = A_WIDTH // A_HEADS
A_QK = A_HEADS * A_DK
B_WIDTH = D_MODEL - A_WIDTH
S5_GROUP = 16
S5_GROUPS = B_WIDTH // S5_GROUP
S5_STATE = 64
EVEN_IN = 2 * A_QK + 2 * A_WIDTH + B_WIDTH
C_HEADS = 8
C_HD = D_MODEL // C_HEADS
C_WIDTH = C_HEADS * C_HD
SB_BLOCK = 128
D_FF = 4 * D_MODEL
EPS = 1e-6
DT_MIN = 1e-3
DT_MAX = 1e-1

kernel_name = 'hybrid_streaming_encoder_step'


def rmsnorm(x, g):
    xf = x.astype(jnp.float32)
    r = lax.rsqrt(jnp.mean(xf * xf, axis=-1, keepdims=True) + EPS)
    return (xf * r * g.astype(jnp.float32)).astype(x.dtype)


def heads(a, n):
    b, t, w = a.shape
    return a.reshape(b, t, n, w // n).transpose(0, 2, 1, 3)


def sq_relu_mlp(h, w_up, w_down):
    return jnp.square(jax.nn.relu(h @ w_up)) @ w_down


def hgrn_chunk(S0, q, k, logf, iv):
    L = q.shape[2]
    b = jnp.cumsum(logf, axis=2)
    causal = jnp.tril(jnp.ones((L, L), dtype=bool))[:, :, None]
    diff = b[:, :, :, None, :] - b[:, :, None, :, :]
    dec = jnp.where(causal, jnp.exp(jnp.where(causal, diff, 0.0)), 0.0)
    att = jnp.einsum('bhtsd,bhtd,bhsd->bhts', dec, q, k)
    o = (jnp.einsum('bhts,bhsv->bhtv', att, iv)
         + jnp.einsum('bhtd,bhdv->bhtv', q * jnp.exp(b), S0))
    b_last = b[:, :, -1]
    S = (jnp.exp(b_last)[..., None] * S0
         + jnp.einsum('bhsd,bhsv->bhdv', k * jnp.exp(b_last[:, :, None] - b), iv))
    return S, o


def hgrn_recur(q, k, logf, iv, S0, n_lead):
    S, o_lead = hgrn_chunk(S0, q[:, :, :n_lead], k[:, :, :n_lead], logf[:, :, :n_lead], iv[:, :, :n_lead])
    bsz, nh, T, _ = q.shape
    rest = T - n_lead
    if rest == 0:
        return o_lead, S
    nc = rest // CHUNK

    def to_chunks(a):
        return a[:, :, n_lead:].reshape(bsz, nh, nc, CHUNK, a.shape[-1]).transpose(2, 0, 1, 3, 4)

    def step(S_c, xs):
        return hgrn_chunk(S_c, xs[0], xs[1], xs[2], xs[3])

    S, o_rest = lax.scan(step, S, (to_chunks(q), to_chunks(k), to_chunks(logf), to_chunks(iv)))
    o_rest = o_rest.transpose(1, 2, 0, 3, 4).reshape(bsz, nh, rest, iv.shape[-1])
    return jnp.concatenate([o_lead, o_rest], axis=2), S


def cplx_scan(a_re, a_im, b_re, b_im):
    def combine(e1, e2):
        a1r, a1i, b1r, b1i = e1
        a2r, a2i, b2r, b2i = e2
        return (a2r * a1r - a2i * a1i,
                a2r * a1i + a2i * a1r,
                a2r * b1r - a2i * b1i + b2r,
                a2r * b1i + a2i * b1r + b2i)
    _, _, x_re, x_im = lax.associative_scan(combine, (a_re, a_im, b_re, b_im), axis=0)
    return x_re, x_im


def s5_mix(u, a_re, a_im, log_dt, b_re, b_im, c_re, c_im, d_skip, x0_re, x0_im):
    bsz, T, _ = u.shape
    f32 = jnp.float32
    uf = u.astype(f32).reshape(bsz, T, S5_GROUPS, S5_GROUP)
    ar = a_re.astype(f32)
    ai = a_im.astype(f32)
    dt = jnp.exp(log_dt.astype(f32))[:, None]
    mag = jnp.exp(dt * ar)
    abar_re = mag * jnp.cos(dt * ai)
    abar_im = mag * jnp.sin(dt * ai)
    den = ar * ar + ai * ai
    zr = ((abar_re - 1.0) * ar + abar_im * ai) / den
    zi = (abar_im * ar - (abar_re - 1.0) * ai) / den
    br = b_re.astype(f32)
    bi = b_im.astype(f32)
    bbar_re = zr[..., None] * br - zi[..., None] * bi
    bbar_im = zr[..., None] * bi + zi[..., None] * br
    bu_re = jnp.einsum('btgp,gnp->btgn', uf, bbar_re)
    bu_im = jnp.einsum('btgp,gnp->btgn', uf, bbar_im)
    if x0_re is not None:
        x0r = x0_re.astype(f32)
        x0i = x0_im.astype(f32)
        bu_re = bu_re.at[:, 0].add(abar_re * x0r - abar_im * x0i)
        bu_im = bu_im.at[:, 0].add(abar_re * x0i + abar_im * x0r)
    a_seq_re = jnp.broadcast_to(abar_re, (T, S5_GROUPS, S5_STATE))
    a_seq_im = jnp.broadcast_to(abar_im, (T, S5_GROUPS, S5_STATE))
    xs_re, xs_im = jax.vmap(cplx_scan, in_axes=(None, None, 0, 0))(a_seq_re, a_seq_im, bu_re, bu_im)
    y = (jnp.einsum('btgn,gpn->btgp', xs_re, c_re.astype(f32))
         - jnp.einsum('btgn,gpn->btgp', xs_im, c_im.astype(f32))
         + d_skip.astype(f32) * uf)
    return y.reshape(bsz, T, B_WIDTH), xs_re[:, -1], xs_im[:, -1]


def even_mixer(h, S0, x0_re, x0_im, n_lead, w_in, lb, g_norm, a_re, a_im, log_dt,
               b_re, b_im, c_re, c_im, d_skip, w_glu, w_out):
    bsz, T, _ = h.shape
    f32 = jnp.float32
    p = h @ w_in
    q, zf, iv, g, u = jnp.split(p, [A_QK, 2 * A_QK, 2 * A_QK + A_WIDTH, 2 * A_QK + 2 * A_WIDTH], axis=-1)
    fgate = lb + (1.0 - lb) * jax.nn.sigmoid(zf.astype(f32))
    logf = jnp.log(fgate)
    kk = 1.0 - fgate
    o_a, S = hgrn_recur(heads(q.astype(f32), A_HEADS), heads(kk, A_HEADS), heads(logf, A_HEADS),
                        heads(iv.astype(f32), A_HEADS), S0.astype(f32), n_lead)
    o_a = rmsnorm(o_a.transpose(0, 2, 1, 3), g_norm).reshape(bsz, T, A_WIDTH)
    o_a = (o_a * jax.nn.silu(g.astype(f32))).astype(h.dtype)
    y, xr, xi = s5_mix(u, a_re, a_im, log_dt, b_re, b_im, c_re, c_im, d_skip, x0_re, x0_im)
    yg = jax.nn.gelu(y, approximate=False).astype(h.dtype)
    o_b = yg * jax.nn.sigmoid(yg @ w_glu)
    out = jnp.concatenate([o_a, o_b], axis=-1) @ w_out
    return out, S, xr, xi


def sb_attend(q, k, v, q_pos, k_pos):
    z = jnp.einsum('bqhd,bkhd->bhqk', q.astype(jnp.float32), k.astype(jnp.float32)) * (C_HD ** -0.5)
    mask = k_pos[None, :] < q_pos[:, None]
    log_b = jax.nn.log_sigmoid(z)
    log_1mb = jnp.where(mask, log_b - z, 0.0)
    later = lax.cumsum(log_1mb, axis=3, reverse=True) - log_1mb
    w = jnp.where(mask, jnp.exp(log_b + later), 0.0)
    return jnp.einsum('bhqk,bkhd->bqhd', w, v.astype(jnp.float32))


def sb_qkv(h, w_in):
    bsz, T, _ = h.shape
    q, k, v = jnp.split(h @ w_in, 3, axis=-1)
    shp = (bsz, T, C_HEADS, C_HD)
    return q.reshape(shp), k.reshape(shp), v.reshape(shp)


def sb_prompt(h, w_in, w_out):
    bsz, T, _ = h.shape
    q, k, v = sb_qkv(h, w_in)
    pos = jnp.arange(T)
    o_meta = sb_attend(q[:, :N_META], k[:, :N_META], v[:, :N_META], pos[:N_META], pos[:N_META])
    nb = (T - N_META) // SB_BLOCK
    qb = q[:, N_META:].reshape(bsz, nb, SB_BLOCK, C_HEADS, C_HD).transpose(1, 0, 2, 3, 4)
    pb = pos[N_META:].reshape(nb, SB_BLOCK)
    ob = lax.map(lambda a: sb_attend(a[0], k, v, a[1], pos), (qb, pb))
    ob = ob.transpose(1, 0, 2, 3, 4).reshape(bsz, T - N_META, C_HEADS, C_HD)
    o = jnp.concatenate([o_meta, ob], axis=1).reshape(bsz, T, C_WIDTH).astype(h.dtype)
    return o @ w_out, k, v


def sb_sample(h, ck, cv, w_in, w_out):
    bsz, T, _ = h.shape
    past = ck.shape[1]
    q, k, v = sb_qkv(h, w_in)
    k_all = jnp.concatenate([ck.astype(k.dtype), k], axis=1)
    v_all = jnp.concatenate([cv.astype(v.dtype), v], axis=1)
    k_pos = jnp.arange(past + T)
    q_pos = past + jnp.arange(T)
    o = sb_attend(q, k_all, v_all, q_pos, k_pos).reshape(bsz, T, C_WIDTH).astype(h.dtype)
    return o @ w_out, k, v


def setup_inputs(seed: int = 0) -> dict:
    key = jax.random.key(seed)
    ks = jax.random.split(key, 32)
    f32 = jnp.float32
    nrm = lambda k, shp, s: jax.random.normal(k, shp, f32) * s
    n_idx = jnp.arange(S5_STATE, dtype=f32)
    return {
        'x_prompt': nrm(ks[0], (BATCH, SEQ, D_MODEL), 1.0),
        'x_sample': nrm(ks[1], (DEC_BATCH, DEC_SEQ, D_MODEL), 1.0),
        'state_hgrn': nrm(ks[2], (N_EVEN, DEC_BATCH, A_HEADS, A_DK, A_DV), 0.5),
        'state_ssm_re': nrm(ks[3], (N_EVEN, DEC_BATCH, S5_GROUPS, S5_STATE), 0.1),
        'state_ssm_im': nrm(ks[4], (N_EVEN, DEC_BATCH, S5_GROUPS, S5_STATE), 0.1),
        'cache_k': nrm(ks[5], (N_ODD, DEC_BATCH, PAST_LEN, C_HEADS, C_HD), 1.0),
        'cache_v': nrm(ks[6], (N_ODD, DEC_BATCH, PAST_LEN, C_HEADS, C_HD), 1.0),
        'meta_tokens': nrm(ks[7], (N_META, D_MODEL), 1.0),
        'ln_mix': 1.0 + nrm(ks[8], (DEPTH, D_MODEL), 0.01),
        'ln_mlp': 1.0 + nrm(ks[9], (DEPTH, D_MODEL), 0.01),
        'ln_final': 1.0 + nrm(ks[10], (D_MODEL,), 0.01),
        'w_in_even': nrm(ks[11], (N_EVEN, D_MODEL, EVEN_IN), D_MODEL ** -0.5),
        'hgrn_lb': nrm(ks[12], (N_EVEN + 1, A_QK), 1.0),
        'hgrn_norm': 1.0 + nrm(ks[13], (N_EVEN, A_DV), 0.01),
        'ssm_a_re': -0.5 + nrm(ks[14], (N_EVEN, S5_GROUPS, S5_STATE), 0.01),
        'ssm_a_im': math.pi * n_idx + nrm(ks[15], (N_EVEN, S5_GROUPS, S5_STATE), 0.01),
        'ssm_log_dt': jax.random.uniform(ks[16], (N_EVEN, S5_GROUPS), f32, math.log(DT_MIN), math.log(DT_MAX)),
        'ssm_b_re': nrm(ks[17], (N_EVEN, S5_GROUPS, S5_STATE, S5_GROUP), (2 * S5_GROUP) ** -0.5),
        'ssm_b_im': nrm(ks[18], (N_EVEN, S5_GROUPS, S5_STATE, S5_GROUP), (2 * S5_GROUP) ** -0.5),
        'ssm_c_re': nrm(ks[19], (N_EVEN, S5_GROUPS, S5_GROUP, S5_STATE), S5_STATE ** -0.5),
        'ssm_c_im': nrm(ks[20], (N_EVEN, S5_GROUPS, S5_GROUP, S5_STATE), S5_STATE ** -0.5),
        'ssm_d': nrm(ks[21], (N_EVEN, S5_GROUPS, S5_GROUP), 1.0),
        'w_glu': nrm(ks[22], (N_EVEN, B_WIDTH, B_WIDTH), B_WIDTH ** -0.5),
        'w_out_even': nrm(ks[23], (N_EVEN, A_WIDTH + B_WIDTH, D_MODEL), (A_WIDTH + B_WIDTH) ** -0.5),
        'w_in_odd': nrm(ks[24], (N_ODD, D_MODEL, 3 * C_WIDTH), D_MODEL ** -0.5),
        'w_out_odd': nrm(ks[25], (N_ODD, C_WIDTH, D_MODEL), C_WIDTH ** -0.5),
        'w_up': nrm(ks[26], (DEPTH, D_MODEL, D_FF), D_MODEL ** -0.5),
        'w_down': nrm(ks[27], (DEPTH, D_FF, D_MODEL), D_FF ** -0.5),
    }


def reference(x_prompt, x_sample, state_hgrn, state_ssm_re, state_ssm_im, cache_k, cache_v,
              meta_tokens, ln_mix, ln_mlp, ln_final, w_in_even, hgrn_lb, hgrn_norm,
              ssm_a_re, ssm_a_im, ssm_log_dt, ssm_b_re, ssm_b_im, ssm_c_re, ssm_c_im, ssm_d,
              w_glu, w_out_even, w_in_odd, w_out_odd, w_up, w_down):
    bsz = x_prompt.shape[0]
    meta = jnp.broadcast_to(meta_tokens.astype(x_prompt.dtype)[None], (bsz, N_META, D_MODEL))
    xp = jnp.concatenate([meta, x_prompt], axis=1)
    xs = x_sample
    lb_all = jnp.cumsum(jax.nn.softmax(hgrn_lb.astype(jnp.float32), axis=0), axis=0)
    hg_p, hg_s, sr_p, si_p, sr_s, si_s, k_p, v_p, k_s, v_s = [], [], [], [], [], [], [], [], [], []
    for l in range(DEPTH):
        j = l // 2
        hp = rmsnorm(xp, ln_mix[l])
        hs = rmsnorm(xs, ln_mix[l])
        if l % 2 == 0:
            shared = (w_in_even[j], lb_all[j], hgrn_norm[j], ssm_a_re[j], ssm_a_im[j], ssm_log_dt[j],
                      ssm_b_re[j], ssm_b_im[j], ssm_c_re[j], ssm_c_im[j], ssm_d[j], w_glu[j], w_out_even[j])
            S0p = jnp.zeros((bsz, A_HEADS, A_DK, A_DV), jnp.float32)
            op, Sp, xr_p, xi_p = even_mixer(hp, S0p, None, None, N_META, *shared)
            os_, Ss, xr_s, xi_s = even_mixer(hs, state_hgrn[j], state_ssm_re[j], state_ssm_im[j],
                                             xs.shape[1], *shared)
            hg_p.append(Sp)
            hg_s.append(Ss)
            sr_p.append(xr_p)
            si_p.append(xi_p)
            sr_s.append(xr_s)
            si_s.append(xi_s)
        else:
            op, kp, vp = sb_prompt(hp, w_in_odd[j], w_out_odd[j])
            os_, ks_, vs_ = sb_sample(hs, cache_k[j], cache_v[j], w_in_odd[j], w_out_odd[j])
            k_p.append(kp)
            v_p.append(vp)
            k_s.append(ks_)
            v_s.append(vs_)
        xp = xp + op
        xs = xs + os_
        xp = xp + sq_relu_mlp(rmsnorm(xp, ln_mlp[l]), w_up[l], w_down[l])
        xs = xs + sq_relu_mlp(rmsnorm(xs, ln_mlp[l]), w_up[l], w_down[l])
    y_prompt = rmsnorm(xp[:, N_META:], ln_final)
    y_sample = rmsnorm(xs, ln_final)
    new_hgrn_prompt = jnp.stack(hg_p, 0)
    new_hgrn_sample = jnp.stack(hg_s, 0)
    new_ssm_re_prompt = jnp.stack(sr_p, 0)
    new_ssm_im_prompt = jnp.stack(si_p, 0)
    new_ssm_re_sample = jnp.stack(sr_s, 0)
    new_ssm_im_sample = jnp.stack(si_s, 0)
    new_k_prompt = jnp.stack(k_p, 0)
    new_v_prompt = jnp.stack(v_p, 0)
    new_k_sample = jnp.stack(k_s, 0)
    new_v_sample = jnp.stack(v_s, 0)
    return (y_prompt, y_sample, new_hgrn_prompt, new_hgrn_sample, new_ssm_re_prompt, new_ssm_im_prompt,
            new_ssm_re_sample, new_ssm_im_sample, new_k_prompt, new_v_prompt, new_k_sample, new_v_sample)
```

```python
import functools
import math

import jax
import jax.numpy as jnp
from jax import lax
from jax.experimental import pallas as pl
from jax.experimental.pallas import tpu as pltpu

F32 = jnp.float32
BF16 = jnp.bfloat16

D_MODEL = 1024
N_META = 16
A_HEADS = 4
A_DK = 128
A_DV = 128
A_WIDTH = 512
A_QK = A_HEADS * A_DK
B_WIDTH = 512
S5_GROUP = 16
S5_GROUPS = 32
S5_STATE = 64
C_HEADS = 8
C_HD = 128
D_FF = 4 * D_MODEL
EPS = 1e-6

SUBLANES = 8
LANES = 128
VMEM_LIMIT_BYTES = 48 * 1024 * 1024

HGRN_CHUNK = 64
HGRN_STEP_ROWS = 256
S5_CHUNK = 16
S5_PAIR = 2
ATT_BLOCK = 128
ATT_HEADS_PER_STEP = 2
ATT_DEAD_LOG_WEIGHT = -104.0
NEG_BIG = -1e30


def _cparams(*sem):
    return pltpu.CompilerParams(dimension_semantics=sem, vmem_limit_bytes=VMEM_LIMIT_BYTES)


def _sigmoid(x):
    return 1.0 / (1.0 + jnp.exp(-x))


def _rms_scale(x):
    return lax.rsqrt(jnp.mean(x * x, axis=-1, keepdims=True) + EPS)


def _norm_matmul_kernel(x_ref, g_ref, w_ref, *out_refs, cols, nchunk):
    x = x_ref[...]
    h = (x * _rms_scale(x) * g_ref[...]).astype(BF16)
    for c0, width in sorted(set(cols)):
        for c in range(0, width, nchunk):
            res = jnp.dot(h, w_ref[:, c0 + c:c0 + c + nchunk], preferred_element_type=F32)
            for o_ref, col in zip(out_refs, cols):
                if col == (c0, width):
                    o_ref[:, c:c + nchunk] = res.astype(o_ref.dtype)


def norm_matmul(x, g, w_bf16, outs, *, tm=512, nchunk=512):
    m, d = x.shape
    tm = min(tm, m)
    kern = functools.partial(_norm_matmul_kernel, cols=[(c0, wd) for c0, wd, _ in outs], nchunk=nchunk)
    return pl.pallas_call(
        kern,
        grid=(pl.cdiv(m, tm),),
        in_specs=[pl.BlockSpec((tm, d), lambda i: (i, 0)),
                  pl.BlockSpec((1, d), lambda i: (0, 0)),
                  pl.BlockSpec(w_bf16.shape, lambda i: (0, 0))],
        out_specs=[pl.BlockSpec((tm, wd), lambda i: (i, 0)) for _, wd, _ in outs],
        out_shape=[jax.ShapeDtypeStruct((m, wd), dt) for _, wd, dt in outs],
        compiler_params=_cparams("parallel"),
        name="norm_matmul",
    )(x, g.reshape(1, d), w_bf16)


def _even_out_kernel(x_ref, oa_ref, yg_ref, wglu_ref, wa_ref, wb_ref, o_ref):
    yg = yg_ref[...]
    gate = _sigmoid(jnp.dot(yg.astype(BF16), wglu_ref[...], preferred_element_type=F32))
    ob = (yg * gate).astype(BF16)
    o_ref[...] = (x_ref[...]
                  + jnp.dot(oa_ref[...], wa_ref[...], preferred_element_type=F32)
                  + jnp.dot(ob, wb_ref[...], preferred_element_type=F32))


def even_out(x, o_a, yg, w_glu, w_out_a, w_out_b, *, tm=512):
    m, d = x.shape
    tm = min(tm, m)
    row = lambda i: (i, 0)
    fixed = lambda i: (0, 0)
    return pl.pallas_call(
        _even_out_kernel,
        grid=(pl.cdiv(m, tm),),
        in_specs=[pl.BlockSpec((tm, d), row), pl.BlockSpec((tm, A_WIDTH), row),
                  pl.BlockSpec((tm, B_WIDTH), row), pl.BlockSpec(w_glu.shape, fixed),
                  pl.BlockSpec(w_out_a.shape, fixed), pl.BlockSpec(w_out_b.shape, fixed)],
        out_specs=pl.BlockSpec((tm, d), row),
        out_shape=jax.ShapeDtypeStruct((m, d), F32),
        compiler_params=_cparams("parallel"),
        name="even_out",
    )(x, o_a, yg, w_glu, w_out_a, w_out_b)


def _odd_out_kernel(x_ref, o_ref_in, w_ref, o_ref):
    o_ref[...] = x_ref[...] + jnp.dot(o_ref_in[...], w_ref[...], preferred_element_type=F32)


def odd_out(x, o, w_out, *, tm=512):
    m, d = x.shape
    tm = min(tm, m)
    row = lambda i: (i, 0)
    return pl.pallas_call(
        _odd_out_kernel,
        grid=(pl.cdiv(m, tm),),
        in_specs=[pl.BlockSpec((tm, d), row), pl.BlockSpec((tm, o.shape[1]), row),
                  pl.BlockSpec(w_out.shape, lambda i: (0, 0))],
        out_specs=pl.BlockSpec((tm, d), row),
        out_shape=jax.ShapeDtypeStruct((m, d), F32),
        compiler_params=_cparams("parallel"),
        name="odd_out",
    )(x, o, w_out)


def _mlp_kernel(x_ref, g_ref, wup_ref, wdn_ref, gf_ref, o_ref, h_scr, acc_scr, *, final_norm):
    j = pl.program_id(1)

    @pl.when(j == 0)
    def _():
        x = x_ref[...]
        h_scr[...] = (x * _rms_scale(x) * g_ref[...]).astype(BF16)
        acc_scr[...] = jnp.zeros_like(acc_scr)

    a = jnp.dot(h_scr[...], wup_ref[...], preferred_element_type=F32)
    a = jnp.square(jnp.maximum(a, 0.0)).astype(BF16)
    acc_scr[...] += jnp.dot(a, wdn_ref[...], preferred_element_type=F32)

    @pl.when(j == pl.num_programs(1) - 1)
    def _():
        y = x_ref[...] + acc_scr[...]
        if final_norm:
            y = y * _rms_scale(y) * gf_ref[...]
        o_ref[...] = y


def mlp(x, g, w_up, w_down, g_final, *, final_norm, tm=1024, tf=512):
    m, d = x.shape
    tm = min(tm, m)
    ff = w_up.shape[1]
    return pl.pallas_call(
        functools.partial(_mlp_kernel, final_norm=final_norm),
        grid=(pl.cdiv(m, tm), ff // tf),
        in_specs=[pl.BlockSpec((tm, d), lambda i, j: (i, 0)),
                  pl.BlockSpec((1, d), lambda i, j: (0, 0)),
                  pl.BlockSpec((d, tf), lambda i, j: (0, j)),
                  pl.BlockSpec((tf, d), lambda i, j: (j, 0)),
                  pl.BlockSpec((1, d), lambda i, j: (0, 0))],
        out_specs=pl.BlockSpec((tm, d), lambda i, j: (i, 0)),
        out_shape=jax.ShapeDtypeStruct((m, d), F32),
        scratch_shapes=[pltpu.VMEM((tm, d), BF16), pltpu.VMEM((tm, d), F32)],
        compiler_params=_cparams("parallel", "arbitrary"),
        name="mlp",
    )(x, g.reshape(1, d), w_up, w_down, g_final.reshape(1, d))


def _split3(x):
    hi = x.astype(BF16)
    r1 = x - hi.astype(F32)
    mid = r1.astype(BF16)
    lo = (r1 - mid.astype(F32)).astype(BF16)
    return hi, mid, lo


def _hgrn_kernel(q_ref, zf_ref, iv_ref, g_ref, lb_ref, gn_ref, s0_ref, o_ref, sout_ref,
                 s_scr, b_scr, k_scr, iv_scr, *, chunk, n_sub):
    step = pl.program_id(2)

    @pl.when(step == 0)
    def _():
        s_scr[...] = s0_ref[...]

    lb = lb_ref[...]
    gn = gn_ref[...]
    rows = lax.broadcasted_iota(jnp.int32, (chunk, chunk), 0)
    cols = lax.broadcasted_iota(jnp.int32, (chunk, chunk), 1)
    tri = jnp.where(rows >= cols, 1.0, 0.0).astype(BF16)
    ones = jnp.ones((A_DK, LANES), BF16)
    sub = lax.broadcasted_iota(jnp.int32, (SUBLANES, A_DK), 0)
    n_blk = chunk // SUBLANES

    def do_chunk(cc, carry):
        r0 = pl.multiple_of(cc * chunk, chunk)
        q = q_ref[pl.ds(r0, chunk), :]
        zf = zf_ref[pl.ds(r0, chunk), :]
        iv = iv_ref[pl.ds(r0, chunk), :]
        g = g_ref[pl.ds(r0, chunk), :]
        f = lb + (1.0 - lb) * _sigmoid(zf)
        logf = jnp.log(f)
        kk = 1.0 - f
        hi, mid, lo = _split3(logf)
        b = (jnp.dot(tri, hi, preferred_element_type=F32)
             + jnp.dot(tri, mid, preferred_element_type=F32)
             + jnp.dot(tri, lo, preferred_element_type=F32))
        b_scr[...] = b
        k_scr[...] = kk
        iv_scr[...] = iv

        o_blocks = []
        for i in range(n_blk):
            b_i = b[SUBLANES * i:SUBLANES * (i + 1)]
            q_i = q[SUBLANES * i:SUBLANES * (i + 1)]
            n_s = SUBLANES * (i + 1)
            ws = []
            for s in range(n_s):
                dlt = b_i - b_scr[s:s + 1, :]
                if s >= SUBLANES * i:
                    dlt = jnp.where(sub >= (s - SUBLANES * i), dlt, NEG_BIG)
                ws.append(jnp.exp(dlt) * q_i * k_scr[s:s + 1, :])
            w = jnp.concatenate(ws, axis=0).astype(BF16)
            att = jnp.dot(w, ones, preferred_element_type=F32)
            acc = jnp.zeros((SUBLANES, A_DV), F32)
            for s in range(n_s):
                acc = acc + att[SUBLANES * s:SUBLANES * (s + 1)] * iv_scr[s:s + 1, :]
            o_blocks.append(acc)
        o_intra = jnp.concatenate(o_blocks, axis=0)

        s_t = s_scr[...]
        qh = (q * jnp.exp(b)).astype(BF16)
        o_inter = lax.dot_general(qh, s_t.astype(BF16), (((1,), (1,)), ((), ())),
                                  preferred_element_type=F32)
        b_last = b[chunk - 1:chunk, :]
        kd = (kk * jnp.exp(b_last - b)).astype(BF16)
        upd = lax.dot_general(iv.astype(BF16), kd, (((0,), (0,)), ((), ())),
                              preferred_element_type=F32)
        s_scr[...] = s_t * jnp.exp(b_last) + upd

        o = o_intra + o_inter
        o = o * _rms_scale(o) * gn
        o = o * (g * _sigmoid(g))
        o_ref[pl.ds(r0, chunk), :] = o.astype(o_ref.dtype)
        return carry

    lax.fori_loop(0, n_sub, do_chunk, 0)

    @pl.when(step == pl.num_programs(2) - 1)
    def _():
        sout_ref[...] = s_scr[...]


def hgrn(q, zf, iv, g, lb, gnorm, s0_t, bsz, t_len):
    chunk = min(HGRN_CHUNK, t_len)
    step_rows = min(HGRN_STEP_ROWS, t_len)
    n_steps = t_len // step_rows
    tok = pl.BlockSpec((step_rows, A_DK), lambda b, h, c: (b * n_steps + c, h))
    head_vec = pl.BlockSpec((1, A_DK), lambda b, h, c: (0, h))
    s0_b = (lambda b: b) if s0_t.shape[0] == bsz else (lambda b: 0)
    st_in = pl.BlockSpec((None, None, A_DV, A_DK), lambda b, h, c: (s0_b(b), h, 0, 0))
    st_out = pl.BlockSpec((None, None, A_DV, A_DK), lambda b, h, c: (b, h, 0, 0))
    return pl.pallas_call(
        functools.partial(_hgrn_kernel, chunk=chunk, n_sub=step_rows // chunk),
        grid=(bsz, A_HEADS, n_steps),
        in_specs=[tok, tok, tok, tok, head_vec,
                  pl.BlockSpec((1, A_DV), lambda b, h, c: (0, 0)), st_in],
        out_specs=[tok, st_out],
        out_shape=[jax.ShapeDtypeStruct((bsz * t_len, A_WIDTH), BF16),
                   jax.ShapeDtypeStruct((bsz, A_HEADS, A_DV, A_DK), F32)],
        scratch_shapes=[pltpu.VMEM((A_DV, A_DK), F32), pltpu.VMEM((chunk, A_DK), F32),
                        pltpu.VMEM((chunk, A_DK), F32), pltpu.VMEM((chunk, A_DV), F32)],
        compiler_params=_cparams("parallel", "parallel", "arbitrary"),
        name="hgrn",
    )(q, zf, iv, g, lb.reshape(1, A_QK), gnorm.reshape(1, A_DV), s0_t)


def s5_operators(a_re, a_im, log_dt, b_re, b_im, c_re, c_im, d_skip):
    lc, p, n, gp = S5_CHUNK, S5_GROUP, S5_STATE, S5_GROUPS // S5_PAIR
    ar = a_re.astype(F32)
    ai = a_im.astype(F32)
    dt = jnp.exp(log_dt.astype(F32))[:, None]
    lam_re, lam_im = dt * ar, dt * ai

    def power(mm):
        mag = jnp.exp(mm[:, None, None] * lam_re[None])
        ang = mm[:, None, None] * lam_im[None]
        return mag * jnp.cos(ang), mag * jnp.sin(ang)

    abar_re, abar_im = power(jnp.ones((1,), F32))
    abar_re, abar_im = abar_re[0], abar_im[0]
    den = ar * ar + ai * ai
    zr = ((abar_re - 1.0) * ar + abar_im * ai) / den
    zi = (abar_im * ar - (abar_re - 1.0) * ai) / den
    br, bi = b_re.astype(F32), b_im.astype(F32)
    bb_re = zr[..., None] * br - zi[..., None] * bi
    bb_im = zr[..., None] * bi + zi[..., None] * br
    cr, ci = c_re.astype(F32), c_im.astype(F32)

    hp = lax.Precision.HIGHEST
    pw_re, pw_im = power(jnp.arange(lc + 1, dtype=F32))
    ab_re = pw_re[:, :, :, None] * bb_re[None] - pw_im[:, :, :, None] * bb_im[None]
    ab_im = pw_re[:, :, :, None] * bb_im[None] + pw_im[:, :, :, None] * bb_re[None]
    kern = (jnp.einsum('gpn,mgnq->mgpq', cr, ab_re, precision=hp)
            - jnp.einsum('gpn,mgnq->mgpq', ci, ab_im, precision=hp))
    t_idx = jnp.arange(lc)
    lag = t_idx[None, :] - t_idx[:, None]
    toe = jnp.where((lag >= 0)[:, :, None, None, None], kern[jnp.clip(lag, 0, lc)], 0.0)
    toe = toe.transpose(2, 0, 4, 1, 3)
    e_re = ab_re[lc - 1 - t_idx].transpose(1, 0, 3, 2)
    e_im = ab_im[lc - 1 - t_idx].transpose(1, 0, 3, 2)
    f_re = (cr[None] * pw_re[1:, :, None, :] - ci[None] * pw_im[1:, :, None, :])
    f_im = (-cr[None] * pw_im[1:, :, None, :] - ci[None] * pw_re[1:, :, None, :])
    f_re = f_re.transpose(1, 3, 0, 2)
    f_im = f_im.transpose(1, 3, 0, 2)

    def pair_diag(x, rows, cols):
        x = x.reshape(gp, S5_PAIR, rows, cols)
        eye = jnp.eye(S5_PAIR, dtype=F32)
        return jnp.einsum('girc,ij->girjc', x, eye).reshape(gp, S5_PAIR * rows, S5_PAIR * cols)

    toe2 = pair_diag(toe.reshape(S5_GROUPS, lc * p, lc * p), lc * p, lc * p)
    fre2 = pair_diag(f_re.reshape(S5_GROUPS, n, lc * p), n, lc * p)
    fim2 = pair_diag(f_im.reshape(S5_GROUPS, n, lc * p), n, lc * p)
    ere2 = pair_diag(e_re.reshape(S5_GROUPS, lc * p, n), lc * p, n)
    eim2 = pair_diag(e_im.reshape(S5_GROUPS, lc * p, n), lc * p, n)
    tf = jnp.concatenate([toe2, fre2, fim2], axis=1).astype(BF16)
    d_vec = jnp.broadcast_to(d_skip.astype(F32).reshape(gp, S5_PAIR, 1, p),
                             (gp, S5_PAIR, lc, p)).reshape(gp, 1, S5_PAIR * lc * p)
    al_re = pw_re[lc].reshape(1, S5_GROUPS * n)
    al_im = pw_im[lc].reshape(1, S5_GROUPS * n)
    return dict(tf=tf, e_re=ere2.astype(BF16), e_im=eim2.astype(BF16), d=d_vec, al_re=al_re, al_im=al_im)


def _s5_local_kernel(u_ref, ere_ref, eim_ref, lre_ref, lim_ref):
    u = u_ref[...].astype(BF16)
    lre_ref[...] = jnp.dot(u, ere_ref[...], preferred_element_type=F32)
    lim_ref[...] = jnp.dot(u, eim_ref[...], preferred_element_type=F32)


def _s5_scan_kernel(lre_ref, lim_ref, x0re_ref, x0im_ref, are_ref, aim_ref,
                    xre_ref, xim_ref, fre_ref, fim_ref, *, n_chunks, rows):
    a_re = jnp.broadcast_to(are_ref[...], (rows, are_ref.shape[1]))
    a_im = jnp.broadcast_to(aim_ref[...], (rows, aim_ref.shape[1]))

    def body(c, st):
        x_re, x_im = st
        r0 = pl.multiple_of(c * rows, rows)
        xre_ref[pl.ds(r0, rows), :] = x_re
        xim_ref[pl.ds(r0, rows), :] = x_im
        n_re = a_re * x_re - a_im * x_im + lre_ref[pl.ds(r0, rows), :]
        n_im = a_re * x_im + a_im * x_re + lim_ref[pl.ds(r0, rows), :]
        return n_re, n_im

    x_re, x_im = lax.fori_loop(0, n_chunks, body, (x0re_ref[...], x0im_ref[...]))
    fre_ref[...] = x_re
    fim_ref[...] = x_im


def _s5_out_kernel(u_ref, xre_ref, xim_ref, tf_ref, d_ref, y_ref):
    u = u_ref[...]
    lhs = jnp.concatenate([u.astype(BF16), xre_ref[...].astype(BF16), xim_ref[...].astype(BF16)], axis=1)
    y = jnp.dot(lhs, tf_ref[...], preferred_element_type=F32) + d_ref[...] * u
    y_ref[...] = 0.5 * y * (1.0 + lax.erf(y * (1.0 / math.sqrt(2.0))))


def s5(u, ops, x0_re, x0_im, bsz, t_len):
    lc, p, n, gp = S5_CHUNK, S5_GROUP, S5_STATE, S5_GROUPS // S5_PAIR
    rows = SUBLANES
    n_chunks = t_len // lc
    width = S5_PAIR * lc * p
    mc = n_chunks * rows
    u6 = u.reshape(bsz, n_chunks, lc, gp, S5_PAIR, p)
    u6 = jnp.pad(u6, ((0, rows - bsz),) + ((0, 0),) * 5)
    u_g = u6.transpose(3, 1, 0, 4, 2, 5).reshape(gp, mc, width)
    x0 = [jnp.pad(x.astype(F32).reshape(bsz, S5_GROUPS * n), ((0, rows - bsz), (0, 0)))
          for x in (x0_re, x0_im)]
    n_state = S5_GROUPS * n
    pair_cols = S5_PAIR * n

    lre, lim = pl.pallas_call(
        _s5_local_kernel,
        grid=(gp,),
        in_specs=[pl.BlockSpec((None, mc, width), lambda g: (g, 0, 0)),
                  pl.BlockSpec((None, width, pair_cols), lambda g: (g, 0, 0)),
                  pl.BlockSpec((None, width, pair_cols), lambda g: (g, 0, 0))],
        out_specs=[pl.BlockSpec((mc, pair_cols), lambda g: (0, g))] * 2,
        out_shape=[jax.ShapeDtypeStruct((mc, n_state), F32)] * 2,
        compiler_params=_cparams("parallel"),
        name="s5_local",
    )(u_g, ops['e_re'], ops['e_im'])

    sc = 2 * pair_cols
    col = lambda j: (0, j)
    xre, xim, fre, fim = pl.pallas_call(
        functools.partial(_s5_scan_kernel, n_chunks=n_chunks, rows=rows),
        grid=(n_state // sc,),
        in_specs=[pl.BlockSpec((mc, sc), col), pl.BlockSpec((mc, sc), col),
                  pl.BlockSpec((rows, sc), col), pl.BlockSpec((rows, sc), col),
                  pl.BlockSpec((1, sc), col), pl.BlockSpec((1, sc), col)],
        out_specs=[pl.BlockSpec((mc, sc), col), pl.BlockSpec((mc, sc), col),
                   pl.BlockSpec((rows, sc), col), pl.BlockSpec((rows, sc), col)],
        out_shape=[jax.ShapeDtypeStruct((mc, n_state), F32)] * 2
                  + [jax.ShapeDtypeStruct((rows, n_state), F32)] * 2,
        compiler_params=_cparams("parallel"),
        name="s5_scan",
    )(lre, lim, x0[0], x0[1], ops['al_re'], ops['al_im'])

    yg = pl.pallas_call(
        _s5_out_kernel,
        grid=(gp,),
        in_specs=[pl.BlockSpec((None, mc, width), lambda g: (g, 0, 0)),
                  pl.BlockSpec((mc, pair_cols), lambda g: (0, g)),
                  pl.BlockSpec((mc, pair_cols), lambda g: (0, g)),
                  pl.BlockSpec((None, width + 2 * pair_cols, width), lambda g: (g, 0, 0)),
                  pl.BlockSpec((None, 1, width), lambda g: (g, 0, 0))],
        out_specs=pl.BlockSpec((None, mc, width), lambda g: (g, 0, 0)),
        out_shape=jax.ShapeDtypeStruct((gp, mc, width), F32),
        compiler_params=_cparams("parallel"),
        name="s5_out",
    )(u_g, xre, xim, ops['tf'], ops['d'])

    yg = yg.reshape(gp, n_chunks, rows, S5_PAIR, lc, p)[:, :, :bsz]
    yg = yg.transpose(2, 1, 4, 0, 3, 5).reshape(bsz * t_len, B_WIDTH)
    fin_re = fre[:bsz].reshape(bsz, S5_GROUPS, n)
    fin_im = fim[:bsz].reshape(bsz, S5_GROUPS, n)
    return yg, fin_re, fin_im


def _sb_attn_kernel(*refs, heads, n_past, p_valid, scale):
    if n_past:
        q_ref, k_ref, v_ref, pk_ref, pv_ref, o_ref, carry_scr, acc_scr = refs
    else:
        q_ref, k_ref, v_ref, o_ref, carry_scr, acc_scr = refs
    blk = ATT_BLOCK
    i = pl.program_id(2)
    rows = lax.broadcasted_iota(jnp.int32, (blk, blk), 0)
    cols = lax.broadcasted_iota(jnp.int32, (blk, blk), 1)
    suffix = jnp.concatenate([jnp.where(rows > cols, 1.0, 0.0), jnp.ones((blk, blk), F32)],
                             axis=1).astype(BF16)
    diag_mask = cols < rows

    def visit(hh, k_blk, v_blk, mask):
        lanes = slice(hh * C_HD, (hh + 1) * C_HD)
        z = lax.dot_general(q_ref[:, lanes], k_blk, (((1,), (1,)), ((), ())),
                            preferred_element_type=F32) * scale
        log_b = jnp.minimum(z, 0.0) - jnp.log1p(jnp.exp(-jnp.abs(z)))
        x = log_b - z
        if mask is not None:
            x = jnp.where(mask, x, 0.0)
        hi = x.astype(BF16)
        lo = (x - hi.astype(F32)).astype(BF16)
        cs = (jnp.dot(hi, suffix, preferred_element_type=F32)
              + jnp.dot(lo, suffix, preferred_element_type=F32))
        carry = carry_scr[hh]
        w = jnp.exp(log_b + cs[:, :blk] + carry)
        if mask is not None:
            w = jnp.where(mask, w, 0.0)
        acc_scr[hh] = acc_scr[hh] + jnp.dot(w.astype(BF16), v_blk, preferred_element_type=F32)
        carry = carry + cs[:, blk:]
        carry_scr[hh] = carry
        return jnp.max(carry)

    carry_scr[...] = jnp.zeros_like(carry_scr)
    acc_scr[...] = jnp.zeros_like(acc_scr)

    def visit_all(k_at, v_at, mask):
        live = None
        for hh in range(heads):
            lanes = slice(hh * C_HD, (hh + 1) * C_HD)
            m = visit(hh, k_at(lanes), v_at(lanes), mask)
            live = m if live is None else jnp.maximum(live, m)
        return live

    r0 = pl.multiple_of(i * blk, blk)
    live = visit_all(lambda l: k_ref[pl.ds(r0, blk), l], lambda l: v_ref[pl.ds(r0, blk), l], diag_mask)

    def own_body(st):
        j, _ = st
        rj = pl.multiple_of(j * blk, blk)
        live = visit_all(lambda l: k_ref[pl.ds(rj, blk), l], lambda l: v_ref[pl.ds(rj, blk), l], None)
        return j - 1, live

    def alive(st):
        return jnp.logical_and(st[0] >= 0, st[1] > ATT_DEAD_LOG_WEIGHT)

    _, live = lax.while_loop(alive, own_body, (i - 1, live))

    if n_past:
        def past_body(st):
            j, _ = st
            rj = pl.multiple_of(j * blk, blk)
            mask = (cols + j * blk) < p_valid
            live = visit_all(lambda l: pk_ref[pl.ds(rj, blk), l], lambda l: pv_ref[pl.ds(rj, blk), l], mask)
            return j - 1, live

        lax.while_loop(alive, past_body, (jnp.int32(n_past - 1), live))

    for hh in range(heads):
        o_ref[:, hh * C_HD:(hh + 1) * C_HD] = acc_scr[hh].astype(o_ref.dtype)


def sb_attention(q, k, v, past_k, past_v, p_valid):
    bsz, t_len, width = q.shape
    heads = ATT_HEADS_PER_STEP
    hw = heads * C_HD
    n_q = t_len // ATT_BLOCK
    n_past = 0 if past_k is None else past_k.shape[1] // ATT_BLOCK
    q_spec = pl.BlockSpec((None, ATT_BLOCK, hw), lambda b, h, i: (b, i, h))
    kv_spec = pl.BlockSpec((None, t_len, hw), lambda b, h, i: (b, 0, h))
    in_specs = [q_spec, kv_spec, kv_spec]
    args = [q, k, v]
    if n_past:
        pb = (lambda b: b) if past_k.shape[0] == bsz else (lambda b: 0)
        p_spec = pl.BlockSpec((None, past_k.shape[1], hw), lambda b, h, i: (pb(b), 0, h))
        in_specs += [p_spec, p_spec]
        args += [past_k, past_v]
    return pl.pallas_call(
        functools.partial(_sb_attn_kernel, heads=heads, n_past=n_past, p_valid=p_valid,
                          scale=C_HD ** -0.5),
        grid=(bsz, width // hw, n_q),
        in_specs=in_specs,
        out_specs=q_spec,
        out_shape=jax.ShapeDtypeStruct((bsz, t_len, width), BF16),
        scratch_shapes=[pltpu.VMEM((heads, ATT_BLOCK, C_HD), F32),
                        pltpu.VMEM((heads, ATT_BLOCK, C_HD), F32)],
        compiler_params=_cparams("parallel", "parallel", "arbitrary"),
        name="sb_attention",
    )(*args)


def _pad_rows(x, mult):
    t = x.shape[1]
    tp = -(-t // mult) * mult
    return x if tp == t else jnp.pad(x, ((0, 0), (0, tp - t), (0, 0)))


def _run_stream(x, bsz, t_len, wts, hgrn_s0_t, ssm0_re, ssm0_im, past_k, past_v, p_valid):
    q, zf, iv, g, u = norm_matmul(
        x, wts['ln_mix'][0], wts['w_in_even'],
        [(0, A_QK, F32), (A_QK, A_QK, F32), (2 * A_QK, A_WIDTH, F32),
         (2 * A_QK + A_WIDTH, A_WIDTH, F32), (2 * A_QK + 2 * A_WIDTH, B_WIDTH, F32)])
    o_a, s_t = hgrn(q, zf, iv, g, wts['lb'], wts['hgrn_norm'], hgrn_s0_t, bsz, t_len)
    yg, x_re, x_im = s5(u, wts['s5'], ssm0_re, ssm0_im, bsz, t_len)
    x = even_out(x, o_a, yg, wts['w_glu'], wts['w_out_even_a'], wts['w_out_even_b'])
    x = mlp(x, wts['ln_mlp'][0], wts['w_up'][0], wts['w_down'][0], wts['ln_final'], final_norm=False)
    qb, kb, vb, k32, v32 = norm_matmul(
        x, wts['ln_mix'][1], wts['w_in_odd'],
        [(0, D_MODEL, BF16), (D_MODEL, D_MODEL, BF16), (2 * D_MODEL, D_MODEL, BF16),
         (D_MODEL, D_MODEL, F32), (2 * D_MODEL, D_MODEL, F32)])
    shp = (bsz, t_len, D_MODEL)
    o = sb_attention(_pad_rows(qb.reshape(shp), ATT_BLOCK), _pad_rows(kb.reshape(shp), ATT_BLOCK),
                     _pad_rows(vb.reshape(shp), ATT_BLOCK), past_k, past_v, p_valid)
    o = o[:, :t_len].reshape(bsz * t_len, D_MODEL)
    x = odd_out(x, o, wts['w_out_odd'])
    y = mlp(x, wts['ln_mlp'][1], wts['w_up'][1], wts['w_down'][1], wts['ln_final'], final_norm=True)
    return y, s_t, x_re, x_im, k32, v32


def kernel(x_prompt, x_sample, state_hgrn, state_ssm_re, state_ssm_im, cache_k, cache_v, meta_tokens,
           ln_mix, ln_mlp, ln_final, w_in_even, hgrn_lb, hgrn_norm, ssm_a_re, ssm_a_im, ssm_log_dt,
           ssm_b_re, ssm_b_im, ssm_c_re, ssm_c_im, ssm_d, w_glu, w_out_even, w_in_odd, w_out_odd,
           w_up, w_down):
    bsz, seq, _ = x_prompt.shape
    dbsz, dseq, _ = x_sample.shape
    past = cache_k.shape[2]
    lb_all = jnp.cumsum(jax.nn.softmax(hgrn_lb.astype(F32), axis=0), axis=0)
    w_out_e = w_out_even[0].astype(BF16)
    wts = dict(
        ln_mix=ln_mix.astype(F32), ln_mlp=ln_mlp.astype(F32), ln_final=ln_final.astype(F32),
        w_in_even=w_in_even[0].astype(BF16), lb=lb_all[0], hgrn_norm=hgrn_norm[0].astype(F32),
        s5=s5_operators(ssm_a_re[0], ssm_a_im[0], ssm_log_dt[0], ssm_b_re[0], ssm_b_im[0],
                        ssm_c_re[0], ssm_c_im[0], ssm_d[0]),
        w_glu=w_glu[0].astype(BF16), w_out_even_a=w_out_e[:A_WIDTH], w_out_even_b=w_out_e[A_WIDTH:],
        w_in_odd=w_in_odd[0].astype(BF16), w_out_odd=w_out_odd[0].astype(BF16),
        w_up=w_up.astype(BF16), w_down=w_down.astype(BF16))

    zeros_s = jnp.zeros((1, A_HEADS, A_DV, A_DK), F32)
    zeros_x = jnp.zeros((1, S5_GROUPS, S5_STATE), F32)
    _, m_s, m_re, m_im, m_k, m_v = _run_stream(
        meta_tokens.astype(F32), 1, N_META, wts, zeros_s, zeros_x, zeros_x, None, None, 0)

    m_kp = _pad_rows(m_k.astype(BF16)[None], ATT_BLOCK)
    m_vp = _pad_rows(m_v.astype(BF16)[None], ATT_BLOCK)
    y_p, s_p, re_p, im_p, k_p, v_p = _run_stream(
        x_prompt.reshape(bsz * seq, D_MODEL), bsz, seq, wts, m_s,
        jnp.broadcast_to(m_re, (bsz, S5_GROUPS, S5_STATE)), jnp.broadcast_to(m_im, (bsz, S5_GROUPS, S5_STATE)),
        m_kp, m_vp, N_META)

    ck = cache_k[0].reshape(dbsz, past, D_MODEL).astype(BF16)
    cv = cache_v[0].reshape(dbsz, past, D_MODEL).astype(BF16)
    y_s, s_s, re_s, im_s, k_s, v_s = _run_stream(
        x_sample.reshape(dbsz * dseq, D_MODEL), dbsz, dseq, wts,
        jnp.swapaxes(state_hgrn[0].astype(F32), -1, -2), state_ssm_re[0], state_ssm_im[0],
        _pad_rows(ck, ATT_BLOCK), _pad_rows(cv, ATT_BLOCK), past)

    def with_meta(meta_rows, main):
        meta_b = jnp.broadcast_to(meta_rows[None], (bsz, N_META, D_MODEL))
        full = jnp.concatenate([meta_b, main.reshape(bsz, seq, D_MODEL)], axis=1)
        return full.reshape(1, bsz, N_META + seq, C_HEADS, C_HD)

    return (y_p.reshape(bsz, seq, D_MODEL), y_s.reshape(dbsz, dseq, D_MODEL),
            jnp.swapaxes(s_p, -1, -2)[None], jnp.swapaxes(s_s, -1, -2)[None],
            re_p[None], im_p[None], re_s[None], im_s[None],
            with_meta(m_k, k_p), with_meta(m_v, v_p),
            k_s.reshape(1, dbsz, dseq, C_HEADS, C_HD), v_s.reshape(1, dbsz, dseq, C_HEADS, C_HD))
```

```python
import functools
import math

import numpy as np
import jax
import jax.numpy as jnp
from jax import lax
from jax.experimental import pallas as pl
from jax.experimental.pallas import tpu as pltpu

F32 = jnp.float32
BF16 = jnp.bfloat16

D_MODEL = 1024
N_META = 16
A_HEADS = 4
A_DK = 128
A_DV = 128
A_WIDTH = 512
A_QK = A_HEADS * A_DK
B_WIDTH = 512
S5_GROUP = 16
S5_GROUPS = 32
S5_STATE = 64
S5_NSTATE = S5_GROUPS * S5_STATE
C_HEADS = 8
C_HD = 128
D_FF = 4 * D_MODEL
EPS = 1e-6

SUBLANES = 8
LANES = 128
VMEM_LIMIT_BYTES = 48 * 1024 * 1024

HGRN_CHUNK = 64
HGRN_STEP_ROWS = 256
S5_STEP_T = 64
S5_LANE_CHUNK = 1024
ATT_Q = 256
ATT_K = 128
ATT_HEADS_PER_STEP = 4
ATT_DEAD_LOG_WEIGHT = -104.0
NEG_BIG = -1e30


def _cparams(*sem):
    return pltpu.CompilerParams(dimension_semantics=sem, vmem_limit_bytes=VMEM_LIMIT_BYTES)


def _sigmoid(x):
    return 1.0 / (1.0 + jnp.exp(-x))


def _rms_scale(x):
    return lax.rsqrt(jnp.mean(x * x, axis=-1, keepdims=True) + EPS)


def _norm_matmul_kernel(x_ref, g_ref, w_ref, *out_refs, cols, nchunk):
    x = x_ref[...]
    h = (x * _rms_scale(x) * g_ref[...]).astype(BF16)
    for c0, width in sorted(set(cols)):
        for c in range(0, width, nchunk):
            res = jnp.dot(h, w_ref[:, c0 + c:c0 + c + nchunk], preferred_element_type=F32)
            for o_ref, col in zip(out_refs, cols):
                if col == (c0, width):
                    o_ref[:, c:c + nchunk] = res.astype(o_ref.dtype)


def norm_matmul(x, g, w_bf16, outs, *, tm=512, nchunk=512):
    m, d = x.shape
    tm = min(tm, m)
    kern = functools.partial(_norm_matmul_kernel, cols=[(c0, wd) for c0, wd, _ in outs], nchunk=nchunk)
    return pl.pallas_call(
        kern,
        grid=(pl.cdiv(m, tm),),
        in_specs=[pl.BlockSpec((tm, d), lambda i: (i, 0)),
                  pl.BlockSpec((1, d), lambda i: (0, 0)),
                  pl.BlockSpec(w_bf16.shape, lambda i: (0, 0))],
        out_specs=[pl.BlockSpec((tm, wd), lambda i: (i, 0)) for _, wd, _ in outs],
        out_shape=[jax.ShapeDtypeStruct((m, wd), dt) for _, wd, dt in outs],
        compiler_params=_cparams("parallel"),
        name="norm_matmul",
    )(x, g.reshape(1, d), w_bf16)


def _even_out_kernel(x_ref, oa_ref, yg_ref, wglu_ref, wa_ref, wb_ref, o_ref):
    yg = yg_ref[...]
    gate = _sigmoid(jnp.dot(yg.astype(BF16), wglu_ref[...], preferred_element_type=F32))
    ob = (yg * gate).astype(BF16)
    o_ref[...] = (x_ref[...]
                  + jnp.dot(oa_ref[...], wa_ref[...], preferred_element_type=F32)
                  + jnp.dot(ob, wb_ref[...], preferred_element_type=F32))


def even_out(x, o_a, yg, w_glu, w_out_a, w_out_b, *, tm=512):
    m, d = x.shape
    tm = min(tm, m)
    row = lambda i: (i, 0)
    fixed = lambda i: (0, 0)
    return pl.pallas_call(
        _even_out_kernel,
        grid=(pl.cdiv(m, tm),),
        in_specs=[pl.BlockSpec((tm, d), row), pl.BlockSpec((tm, A_WIDTH), row),
                  pl.BlockSpec((tm, B_WIDTH), row), pl.BlockSpec(w_glu.shape, fixed),
                  pl.BlockSpec(w_out_a.shape, fixed), pl.BlockSpec(w_out_b.shape, fixed)],
        out_specs=pl.BlockSpec((tm, d), row),
        out_shape=jax.ShapeDtypeStruct((m, d), F32),
        compiler_params=_cparams("parallel"),
        name="even_out",
    )(x, o_a, yg, w_glu, w_out_a, w_out_b)


def _odd_out_kernel(x_ref, o_ref_in, w_ref, o_ref):
    o_ref[...] = x_ref[...] + jnp.dot(o_ref_in[...], w_ref[...], preferred_element_type=F32)


def odd_out(x, o, w_out, *, tm=512):
    m, d = x.shape
    tm = min(tm, m)
    row = lambda i: (i, 0)
    return pl.pallas_call(
        _odd_out_kernel,
        grid=(pl.cdiv(m, tm),),
        in_specs=[pl.BlockSpec((tm, d), row), pl.BlockSpec((tm, o.shape[1]), row),
                  pl.BlockSpec(w_out.shape, lambda i: (0, 0))],
        out_specs=pl.BlockSpec((tm, d), row),
        out_shape=jax.ShapeDtypeStruct((m, d), F32),
        compiler_params=_cparams("parallel"),
        name="odd_out",
    )(x, o, w_out)


def _mlp_kernel(x_ref, g_ref, wup_ref, wdn_ref, gf_ref, o_ref, h_scr, acc_scr, *, final_norm):
    j = pl.program_id(1)

    @pl.when(j == 0)
    def _():
        x = x_ref[...]
        h_scr[...] = (x * _rms_scale(x) * g_ref[...]).astype(BF16)
        acc_scr[...] = jnp.zeros_like(acc_scr)

    a = jnp.dot(h_scr[...], wup_ref[...], preferred_element_type=F32)
    a = jnp.square(jnp.maximum(a, 0.0)).astype(BF16)
    acc_scr[...] += jnp.dot(a, wdn_ref[...], preferred_element_type=F32)

    @pl.when(j == pl.num_programs(1) - 1)
    def _():
        y = x_ref[...] + acc_scr[...]
        if final_norm:
            y = y * _rms_scale(y) * gf_ref[...]
        o_ref[...] = y


def mlp(x, g, w_up, w_down, g_final, *, final_norm, tm=1024, tf=512):
    m, d = x.shape
    tm = min(tm, m)
    ff = w_up.shape[1]
    return pl.pallas_call(
        functools.partial(_mlp_kernel, final_norm=final_norm),
        grid=(pl.cdiv(m, tm), ff // tf),
        in_specs=[pl.BlockSpec((tm, d), lambda i, j: (i, 0)),
                  pl.BlockSpec((1, d), lambda i, j: (0, 0)),
                  pl.BlockSpec((d, tf), lambda i, j: (0, j)),
                  pl.BlockSpec((tf, d), lambda i, j: (j, 0)),
                  pl.BlockSpec((1, d), lambda i, j: (0, 0))],
        out_specs=pl.BlockSpec((tm, d), lambda i, j: (i, 0)),
        out_shape=jax.ShapeDtypeStruct((m, d), F32),
        scratch_shapes=[pltpu.VMEM((tm, d), BF16), pltpu.VMEM((tm, d), F32)],
        compiler_params=_cparams("parallel", "arbitrary"),
        name="mlp",
    )(x, g.reshape(1, d), w_up, w_down, g_final.reshape(1, d))


def _split3(x):
    hi = x.astype(BF16)
    r1 = x - hi.astype(F32)
    mid = r1.astype(BF16)
    lo = (r1 - mid.astype(F32)).astype(BF16)
    return hi, mid, lo


def _hgrn_offdiag_mask(chunk):
    n_blk = chunk // SUBLANES
    seg = np.concatenate([np.full(SUBLANES * i, i) for i in range(1, n_blk)])
    blk = np.repeat(np.arange(1, n_blk), SUBLANES)
    return (blk[:, None] == seg[None, :]).astype(np.float32)


def _hgrn_kernel(*refs, chunk, n_sub):
    n_blk = chunk // SUBLANES
    if n_blk > 1:
        (q_ref, zf_ref, iv_ref, g_ref, lb_ref, gn_ref, s0_ref, mask_ref, o_ref, sout_ref,
         s_scr, b_scr, k_scr, iv_scr) = refs
    else:
        (q_ref, zf_ref, iv_ref, g_ref, lb_ref, gn_ref, s0_ref, o_ref, sout_ref,
         s_scr, b_scr, k_scr, iv_scr) = refs
    step = pl.program_id(1)

    @pl.when(step == 0)
    def _():
        s_scr[...] = s0_ref[...]

    lb = lb_ref[...]
    gn = gn_ref[...]
    rows = lax.broadcasted_iota(jnp.int32, (chunk, chunk), 0)
    cols = lax.broadcasted_iota(jnp.int32, (chunk, chunk), 1)
    tri = jnp.where(rows >= cols, 1.0, 0.0).astype(BF16)
    hrow = lax.broadcasted_iota(jnp.int32, (A_QK, A_QK), 0) // A_DK
    hcol = lax.broadcasted_iota(jnp.int32, (A_QK, A_QK), 1) // A_DK
    head_ones = jnp.where(hrow == hcol, 1.0, 0.0).astype(BF16)
    sub = lax.broadcasted_iota(jnp.int32, (SUBLANES, A_QK), 0)
    head_lanes = [slice(hh * A_DK, (hh + 1) * A_DK) for hh in range(A_HEADS)]
    nt = (((1,), (1,)), ((), ()))
    tn = (((0,), (0,)), ((), ()))

    def do_chunk(cc, carry):
        r0 = pl.multiple_of(cc * chunk, chunk)
        q = q_ref[pl.ds(r0, chunk), :]
        zf = zf_ref[pl.ds(r0, chunk), :]
        iv = iv_ref[pl.ds(r0, chunk), :]
        g = g_ref[pl.ds(r0, chunk), :]
        f = lb + (1.0 - lb) * _sigmoid(zf)
        logf = jnp.log(f)
        kk = 1.0 - f
        hi, mid, lo = _split3(logf)
        b = (jnp.dot(tri, hi, preferred_element_type=F32)
             + jnp.dot(tri, mid, preferred_element_type=F32)
             + jnp.dot(tri, lo, preferred_element_type=F32))
        b_scr[...] = b
        k_scr[...] = kk
        iv_scr[...] = iv

        ws = []
        for i in range(n_blk):
            b_i = b[SUBLANES * i:SUBLANES * (i + 1)]
            q_i = q[SUBLANES * i:SUBLANES * (i + 1)]
            for s in range(SUBLANES):
                row = SUBLANES * i + s
                dlt = jnp.where(sub >= s, b_i - b_scr[row:row + 1, :], NEG_BIG)
                ws.append(jnp.exp(dlt) * q_i * k_scr[row:row + 1, :])
        w = jnp.concatenate(ws, axis=0).astype(BF16)
        att = jnp.dot(w, head_ones, preferred_element_type=F32)
        o_blocks = []
        for i in range(n_blk):
            acc = jnp.zeros((SUBLANES, A_WIDTH), F32)
            for s in range(SUBLANES):
                row = SUBLANES * i + s
                acc = acc + att[SUBLANES * row:SUBLANES * (row + 1)] * iv_scr[row:row + 1, :]
            o_blocks.append(acc)
        o = jnp.concatenate(o_blocks, axis=0)

        if n_blk > 1:
            qt, kh, ivs = [], [], []
            for i in range(1, n_blk):
                n_s = SUBLANES * i
                r_i = b_scr[n_s - 1:n_s, :]
                qt.append(q[n_s:n_s + SUBLANES] * jnp.exp(b[n_s:n_s + SUBLANES] - r_i))
                kh.append(kk[:n_s] * jnp.exp(r_i - b[:n_s]))
                ivs.append(iv[:n_s])
            qt = jnp.concatenate(qt, axis=0).astype(BF16)
            kh = jnp.concatenate(kh, axis=0).astype(BF16)
            ivs = jnp.concatenate(ivs, axis=0).astype(BF16)
            mask = mask_ref[...]
            o_off = []
            for l in head_lanes:
                a = lax.dot_general(qt[:, l], kh[:, l], nt, preferred_element_type=F32)
                o_off.append(jnp.dot((a * mask).astype(BF16), ivs[:, l], preferred_element_type=F32))
            o_off = jnp.concatenate(o_off, axis=1)
            o = o + jnp.concatenate([jnp.zeros((SUBLANES, A_WIDTH), F32), o_off], axis=0)

        qh = (q * jnp.exp(b)).astype(BF16)
        b_last = b[chunk - 1:chunk, :]
        kd = (kk * jnp.exp(b_last - b)).astype(BF16)
        dec = jnp.exp(b_last)
        ivb = iv.astype(BF16)
        o_inter = []
        for hh, l in enumerate(head_lanes):
            s_t = s_scr[hh]
            o_inter.append(lax.dot_general(qh[:, l], s_t.astype(BF16), nt, preferred_element_type=F32))
            upd = lax.dot_general(ivb[:, l], kd[:, l], tn, preferred_element_type=F32)
            s_scr[hh] = s_t * dec[:, l] + upd
        o = o + jnp.concatenate(o_inter, axis=1)

        o = jnp.concatenate([o[:, l] * _rms_scale(o[:, l]) * gn for l in head_lanes], axis=1)
        o = o * (g * _sigmoid(g))
        o_ref[pl.ds(r0, chunk), :] = o.astype(o_ref.dtype)
        return carry

    lax.fori_loop(0, n_sub, do_chunk, 0)

    @pl.when(step == pl.num_programs(1) - 1)
    def _():
        sout_ref[...] = s_scr[...]


def hgrn(q, zf, iv, g, lb, gnorm, s0_t, bsz, t_len):
    chunk = HGRN_CHUNK if t_len % HGRN_CHUNK == 0 else SUBLANES
    step_rows = min(HGRN_STEP_ROWS, t_len)
    n_steps = t_len // step_rows
    tok = pl.BlockSpec((step_rows, A_QK), lambda b, c: (b * n_steps + c, 0))
    s0_b = (lambda b: b) if s0_t.shape[0] == bsz else (lambda b: 0)
    in_specs = [tok, tok, tok, tok,
                pl.BlockSpec((1, A_QK), lambda b, c: (0, 0)),
                pl.BlockSpec((1, A_DV), lambda b, c: (0, 0)),
                pl.BlockSpec((None, A_HEADS, A_DV, A_DK), lambda b, c: (s0_b(b), 0, 0, 0))]
    args = [q, zf, iv, g, lb.reshape(1, A_QK), gnorm.reshape(1, A_DV), s0_t]
    if chunk > SUBLANES:
        mask = jnp.asarray(_hgrn_offdiag_mask(chunk))
        in_specs.append(pl.BlockSpec(mask.shape, lambda b, c: (0, 0)))
        args.append(mask)
    return pl.pallas_call(
        functools.partial(_hgrn_kernel, chunk=chunk, n_sub=step_rows // chunk),
        grid=(bsz, n_steps),
        in_specs=in_specs,
        out_specs=[tok, pl.BlockSpec((None, A_HEADS, A_DV, A_DK), lambda b, c: (b, 0, 0, 0))],
        out_shape=[jax.ShapeDtypeStruct((bsz * t_len, A_WIDTH), BF16),
                   jax.ShapeDtypeStruct((bsz, A_HEADS, A_DV, A_DK), F32)],
        scratch_shapes=[pltpu.VMEM((A_HEADS, A_DV, A_DK), F32), pltpu.VMEM((chunk, A_QK), F32),
                        pltpu.VMEM((chunk, A_QK), F32), pltpu.VMEM((chunk, A_WIDTH), F32)],
        compiler_params=_cparams("parallel", "arbitrary"),
        name="hgrn",
    )(*args)


def s5_weights(a_re, a_im, log_dt, b_re, b_im, c_re, c_im, d_skip):
    f32 = F32
    ar = a_re.astype(f32)
    ai = a_im.astype(f32)
    dt = jnp.exp(log_dt.astype(f32))[:, None]
    mag = jnp.exp(dt * ar)
    abar_re = mag * jnp.cos(dt * ai)
    abar_im = mag * jnp.sin(dt * ai)
    den = ar * ar + ai * ai
    zr = ((abar_re - 1.0) * ar + abar_im * ai) / den
    zi = (abar_im * ar - (abar_re - 1.0) * ai) / den
    br, bi = b_re.astype(f32), b_im.astype(f32)
    bb_re = zr[..., None] * br - zi[..., None] * bi
    bb_im = zr[..., None] * bi + zi[..., None] * br
    eye = jnp.eye(S5_GROUPS, dtype=f32)

    def in_proj(bb):
        return jnp.einsum('gnp,gh->gphn', bb, eye).reshape(B_WIDTH, S5_NSTATE)

    def out_proj(c):
        return jnp.einsum('gpn,gh->gnhp', c, eye).reshape(S5_NSTATE, B_WIDTH)

    bbd = jnp.concatenate([in_proj(bb_re), in_proj(bb_im)], axis=1).astype(BF16)
    cbd = jnp.concatenate([out_proj(c_re.astype(f32)), -out_proj(c_im.astype(f32))],
                          axis=0).astype(BF16)
    return dict(bbd=bbd, cbd=cbd, a_re=abar_re.reshape(1, S5_NSTATE), a_im=abar_im.reshape(1, S5_NSTATE),
                d=d_skip.astype(f32).reshape(1, B_WIDTH))


def _s5_kernel(u_ref, x0re_ref, x0im_ref, bbd_ref, cbd_ref, are_ref, aim_ref, d_ref,
               y_ref, fre_ref, fim_ref, u_scr, bu_scr, xre_scr, xim_scr, *, tt, nb):
    step = pl.program_id(0)

    @pl.when(step == 0)
    def _():
        xre_scr[...] = x0re_ref[...]
        xim_scr[...] = x0im_ref[...]

    n_lt = B_WIDTH // LANES
    u_bt = u_ref[...].reshape(nb * tt, B_WIDTH)
    for c in range(n_lt):
        u_scr[c] = u_bt[:, c * LANES:(c + 1) * LANES]
    u_tb = jnp.concatenate(
        [jnp.concatenate([u_scr[c, pl.ds(t, nb, stride=tt), :] for c in range(n_lt)], axis=1)
         for t in range(tt)], axis=0)
    bu_scr[...] = jnp.dot(u_tb.astype(BF16), bbd_ref[...], preferred_element_type=F32)

    for c0 in range(0, S5_NSTATE, S5_LANE_CHUNK):
        re_l = slice(c0, c0 + S5_LANE_CHUNK)
        im_l = slice(S5_NSTATE + c0, S5_NSTATE + c0 + S5_LANE_CHUNK)
        a_re = jnp.broadcast_to(are_ref[:, re_l], (nb, S5_LANE_CHUNK))
        a_im = jnp.broadcast_to(aim_ref[:, re_l], (nb, S5_LANE_CHUNK))

        def body(t, st, re_l=re_l, im_l=im_l, a_re=a_re, a_im=a_im):
            x_re, x_im = st
            r0 = pl.multiple_of(t * nb, nb)
            n_re = a_re * x_re - a_im * x_im + bu_scr[pl.ds(r0, nb), re_l]
            n_im = a_re * x_im + a_im * x_re + bu_scr[pl.ds(r0, nb), im_l]
            bu_scr[pl.ds(r0, nb), re_l] = n_re
            bu_scr[pl.ds(r0, nb), im_l] = n_im
            return n_re, n_im

        x_re, x_im = lax.fori_loop(0, tt, body, (xre_scr[:, re_l], xim_scr[:, re_l]))
        xre_scr[:, re_l] = x_re
        xim_scr[:, re_l] = x_im

    y = jnp.dot(bu_scr[...].astype(BF16), cbd_ref[...], preferred_element_type=F32) + d_ref[...] * u_tb
    y = 0.5 * y * (1.0 + lax.erf(y * (1.0 / math.sqrt(2.0))))
    for t in range(tt):
        for c in range(n_lt):
            u_scr[c, pl.ds(t, nb, stride=tt), :] = y[nb * t:nb * (t + 1), c * LANES:(c + 1) * LANES]
    y_ref[...] = jnp.concatenate([u_scr[c] for c in range(n_lt)], axis=1).reshape(nb, tt, B_WIDTH)

    @pl.when(step == pl.num_programs(0) - 1)
    def _():
        fre_ref[...] = xre_scr[...]
        fim_ref[...] = xim_scr[...]


def s5(u, wts, x0_re, x0_im, bsz, t_len):
    nb = SUBLANES
    tt = min(S5_STEP_T, t_len)
    u3 = u.reshape(bsz, t_len, B_WIDTH)
    x0 = [x.astype(F32).reshape(bsz, S5_NSTATE) for x in (x0_re, x0_im)]
    if bsz < nb:
        u3 = jnp.pad(u3, ((0, nb - bsz), (0, 0), (0, 0)))
        x0 = [jnp.pad(x, ((0, nb - bsz), (0, 0))) for x in x0]
    fixed = lambda i: (0, 0)
    tok = pl.BlockSpec((nb, tt, B_WIDTH), lambda i: (0, i, 0))
    state = pl.BlockSpec((nb, S5_NSTATE), fixed)
    yg, fre, fim = pl.pallas_call(
        functools.partial(_s5_kernel, tt=tt, nb=nb),
        grid=(t_len // tt,),
        in_specs=[tok, state, state,
                  pl.BlockSpec((B_WIDTH, 2 * S5_NSTATE), fixed), pl.BlockSpec((2 * S5_NSTATE, B_WIDTH), fixed),
                  pl.BlockSpec((1, S5_NSTATE), fixed), pl.BlockSpec((1, S5_NSTATE), fixed),
                  pl.BlockSpec((1, B_WIDTH), fixed)],
        out_specs=[tok, state, state],
        out_shape=[jax.ShapeDtypeStruct((nb, t_len, B_WIDTH), F32),
                   jax.ShapeDtypeStruct((nb, S5_NSTATE), F32), jax.ShapeDtypeStruct((nb, S5_NSTATE), F32)],
        scratch_shapes=[pltpu.VMEM((B_WIDTH // LANES, nb * tt, LANES), F32),
                        pltpu.VMEM((nb * tt, 2 * S5_NSTATE), F32),
                        pltpu.VMEM((nb, S5_NSTATE), F32), pltpu.VMEM((nb, S5_NSTATE), F32)],
        compiler_params=_cparams("arbitrary"),
        name="s5",
    )(u3, x0[0], x0[1], wts['bbd'], wts['cbd'], wts['a_re'], wts['a_im'], wts['d'])
    yg = yg[:bsz].reshape(bsz * t_len, B_WIDTH)
    return (yg, fre[:bsz].reshape(bsz, S5_GROUPS, S5_STATE), fim[:bsz].reshape(bsz, S5_GROUPS, S5_STATE))


def _sb_attn_kernel(*refs, heads, n_past, p_valid, scale):
    if n_past:
        q_ref, k_ref, v_ref, pk_ref, pv_ref, o_ref, carry_scr, acc_scr = refs
    else:
        q_ref, k_ref, v_ref, o_ref, carry_scr, acc_scr = refs
    tq, tk = ATT_Q, ATT_K
    i = pl.program_id(2)
    srow = lax.broadcasted_iota(jnp.int32, (tk, tk), 0)
    scol = lax.broadcasted_iota(jnp.int32, (tk, tk), 1)
    suffix = jnp.concatenate([jnp.where(srow > scol, 1.0, 0.0), jnp.ones((tk, tk), F32)],
                             axis=1).astype(BF16)
    rows = lax.broadcasted_iota(jnp.int32, (tq, tk), 0)
    cols = lax.broadcasted_iota(jnp.int32, (tq, tk), 1)

    carry_scr[...] = jnp.zeros_like(carry_scr)
    acc_scr[...] = jnp.zeros_like(acc_scr)
    head_lanes = [slice(hh * C_HD, (hh + 1) * C_HD) for hh in range(heads)]
    n_rows = heads * tq

    def per_head(x, fn):
        return jnp.concatenate([fn(x[hh * tq:(hh + 1) * tq]) for hh in range(heads)], axis=0)

    def visit_all(kr, vr, rj, mask):
        z = jnp.concatenate(
            [lax.dot_general(q_ref[:, l], kr[pl.ds(rj, tk), l], (((1,), (1,)), ((), ())),
                             preferred_element_type=F32) for l in head_lanes], axis=0) * scale
        log_b = jnp.minimum(z, 0.0) - jnp.log(1.0 + jnp.exp(-jnp.abs(z)))
        x = log_b - z
        if mask is not None:
            x = per_head(x, lambda xh: jnp.where(mask, xh, 0.0))
        hi = x.astype(BF16)
        lo = (x - hi.astype(F32)).astype(BF16)
        cs = jnp.dot(jnp.concatenate([hi, lo], axis=0), suffix, preferred_element_type=F32)
        cs = cs[:n_rows] + cs[n_rows:]
        carry = carry_scr[...]
        w = jnp.exp(log_b + cs[:, :tk] + carry)
        if mask is not None:
            w = per_head(w, lambda wh: jnp.where(mask, wh, 0.0))
        w = w.astype(BF16)
        acc_scr[...] += jnp.concatenate(
            [jnp.dot(w[hh * tq:(hh + 1) * tq], vr[pl.ds(rj, tk), l], preferred_element_type=F32)
             for hh, l in enumerate(head_lanes)], axis=0)
        carry = carry + cs[:, tk:]
        carry_scr[...] = carry
        return jnp.max(carry)

    live = None
    for d in (1, 0):
        kb = i * (tq // tk) + d
        live = visit_all(k_ref, v_ref, pl.multiple_of(kb * tk, tk), (cols + d * tk) < rows)

    def own_body(st):
        j, _ = st
        return j - 1, visit_all(k_ref, v_ref, pl.multiple_of(j * tk, tk), None)

    def alive(st):
        return jnp.logical_and(st[0] >= 0, st[1] > ATT_DEAD_LOG_WEIGHT)

    _, live = lax.while_loop(alive, own_body, (i * (tq // tk) - 1, live))

    if n_past:
        def past_body(st):
            j, _ = st
            return j - 1, visit_all(pk_ref, pv_ref, pl.multiple_of(j * tk, tk), (cols + j * tk) < p_valid)

        lax.while_loop(alive, past_body, (jnp.int32(n_past - 1), live))

    for hh in range(heads):
        o_ref[:, head_lanes[hh]] = acc_scr[hh * tq:(hh + 1) * tq, :].astype(o_ref.dtype)


def sb_attention(q, k, v, past_k, past_v, p_valid):
    bsz, t_len, width = q.shape
    heads = ATT_HEADS_PER_STEP
    hw = heads * C_HD
    n_past = 0 if past_k is None else past_k.shape[1] // ATT_K
    q_spec = pl.BlockSpec((None, ATT_Q, hw), lambda b, h, i: (b, i, h))
    kv_spec = pl.BlockSpec((None, t_len, hw), lambda b, h, i: (b, 0, h))
    in_specs = [q_spec, kv_spec, kv_spec]
    args = [q, k, v]
    if n_past:
        pb = (lambda b: b) if past_k.shape[0] == bsz else (lambda b: 0)
        p_spec = pl.BlockSpec((None, past_k.shape[1], hw), lambda b, h, i: (pb(b), 0, h))
        in_specs += [p_spec, p_spec]
        args += [past_k, past_v]
    return pl.pallas_call(
        functools.partial(_sb_attn_kernel, heads=heads, n_past=n_past, p_valid=p_valid,
                          scale=C_HD ** -0.5),
        grid=(bsz, width // hw, t_len // ATT_Q),
        in_specs=in_specs,
        out_specs=q_spec,
        out_shape=jax.ShapeDtypeStruct((bsz, t_len, width), BF16),
        scratch_shapes=[pltpu.VMEM((heads * ATT_Q, C_HD), F32)] * 2,
        compiler_params=_cparams("parallel", "parallel", "arbitrary"),
        name="sb_attention",
    )(*args)


def _pad_rows(x, mult):
    t = x.shape[1]
    tp = -(-t // mult) * mult
    return x if tp == t else jnp.pad(x, ((0, 0), (0, tp - t), (0, 0)))


def _run_stream(x, bsz, t_len, wts, hgrn_s0_t, ssm0_re, ssm0_im, past_k, past_v, p_valid):
    q, zf, iv, g, u = norm_matmul(
        x, wts['ln_mix'][0], wts['w_in_even'],
        [(0, A_QK, F32), (A_QK, A_QK, F32), (2 * A_QK, A_WIDTH, F32),
         (2 * A_QK + A_WIDTH, A_WIDTH, F32), (2 * A_QK + 2 * A_WIDTH, B_WIDTH, F32)])
    o_a, s_t = hgrn(q, zf, iv, g, wts['lb'], wts['hgrn_norm'], hgrn_s0_t, bsz, t_len)
    yg, x_re, x_im = s5(u, wts['s5'], ssm0_re, ssm0_im, bsz, t_len)
    x = even_out(x, o_a, yg, wts['w_glu'], wts['w_out_even_a'], wts['w_out_even_b'])
    x = mlp(x, wts['ln_mlp'][0], wts['w_up'][0], wts['w_down'][0], wts['ln_final'], final_norm=False)
    qb, kb, vb, k32, v32 = norm_matmul(
        x, wts['ln_mix'][1], wts['w_in_odd'],
        [(0, D_MODEL, BF16), (D_MODEL, D_MODEL, BF16), (2 * D_MODEL, D_MODEL, BF16),
         (D_MODEL, D_MODEL, F32), (2 * D_MODEL, D_MODEL, F32)])
    shp = (bsz, t_len, D_MODEL)
    o = sb_attention(_pad_rows(qb.reshape(shp), ATT_Q), _pad_rows(kb.reshape(shp), ATT_Q),
                     _pad_rows(vb.reshape(shp), ATT_Q), past_k, past_v, p_valid)
    o = o[:, :t_len].reshape(bsz * t_len, D_MODEL)
    x = odd_out(x, o, wts['w_out_odd'])
    y = mlp(x, wts['ln_mlp'][1], wts['w_up'][1], wts['w_down'][1], wts['ln_final'], final_norm=True)
    return y, s_t, x_re, x_im, k32, v32


def kernel(x_prompt, x_sample, state_hgrn, state_ssm_re, state_ssm_im, cache_k, cache_v, meta_tokens,
           ln_mix, ln_mlp, ln_final, w_in_even, hgrn_lb, hgrn_norm, ssm_a_re, ssm_a_im, ssm_log_dt,
           ssm_b_re, ssm_b_im, ssm_c_re, ssm_c_im, ssm_d, w_glu, w_out_even, w_in_odd, w_out_odd,
           w_up, w_down):
    bsz, seq, _ = x_prompt.shape
    dbsz, dseq, _ = x_sample.shape
    past = cache_k.shape[2]
    lb_all = jnp.cumsum(jax.nn.softmax(hgrn_lb.astype(F32), axis=0), axis=0)
    w_out_e = w_out_even[0].astype(BF16)
    wts = dict(
        ln_mix=ln_mix.astype(F32), ln_mlp=ln_mlp.astype(F32), ln_final=ln_final.astype(F32),
        w_in_even=w_in_even[0].astype(BF16), lb=lb_all[0], hgrn_norm=hgrn_norm[0].astype(F32),
        s5=s5_weights(ssm_a_re[0], ssm_a_im[0], ssm_log_dt[0], ssm_b_re[0], ssm_b_im[0],
                      ssm_c_re[0], ssm_c_im[0], ssm_d[0]),
        w_glu=w_glu[0].astype(BF16), w_out_even_a=w_out_e[:A_WIDTH], w_out_even_b=w_out_e[A_WIDTH:],
        w_in_odd=w_in_odd[0].astype(BF16), w_out_odd=w_out_odd[0].astype(BF16),
        w_up=w_up.astype(BF16), w_down=w_down.astype(BF16))

    zeros_s = jnp.zeros((1, A_HEADS, A_DV, A_DK), F32)
    zeros_x = jnp.zeros((1, S5_GROUPS, S5_STATE), F32)
    _, m_s, m_re, m_im, m_k, m_v = _run_stream(
        meta_tokens.astype(F32), 1, N_META, wts, zeros_s, zeros_x, zeros_x, None, None, 0)

    m_kp = _pad_rows(m_k.astype(BF16)[None], ATT_K)
    m_vp = _pad_rows(m_v.astype(BF16)[None], ATT_K)
    y_p, s_p, re_p, im_p, k_p, v_p = _run_stream(
        x_prompt.reshape(bsz * seq, D_MODEL), bsz, seq, wts, m_s,
        jnp.broadcast_to(m_re, (bsz, S5_GROUPS, S5_STATE)), jnp.broadcast_to(m_im, (bsz, S5_GROUPS, S5_STATE)),
        m_kp, m_vp, N_META)

    ck = cache_k[0].reshape(dbsz, past, D_MODEL).astype(BF16)
    cv = cache_v[0].reshape(dbsz, past, D_MODEL).astype(BF16)
    y_s, s_s, re_s, im_s, k_s, v_s = _run_stream(
        x_sample.reshape(dbsz * dseq, D_MODEL), dbsz, dseq, wts,
        jnp.swapaxes(state_hgrn[0].astype(F32), -1, -2), state_ssm_re[0], state_ssm_im[0],
        _pad_rows(ck, ATT_K), _pad_rows(cv, ATT_K), past)

    def with_meta(meta_rows, main):
        meta_b = jnp.broadcast_to(meta_rows[None], (bsz, N_META, D_MODEL))
        full = jnp.concatenate([meta_b, main.reshape(bsz, seq, D_MODEL)], axis=1)
        return full.reshape(1, bsz, N_META + seq, C_HEADS, C_HD)

    return (y_p.reshape(bsz, seq, D_MODEL), y_s.reshape(dbsz, dseq, D_MODEL),
            jnp.swapaxes(s_p, -1, -2)[None], jnp.swapaxes(s_s, -1, -2)[None],
            re_p[None], im_p[None], re_s[None], im_s[None],
            with_meta(m_k, k_p), with_meta(m_v, v_p),
            k_s.reshape(1, dbsz, dseq, C_HEADS, C_HD), v_s.reshape(1, dbsz, dseq, C_HEADS, C_HD))
```

```python
import functools
import math

import numpy as np
import jax
import jax.numpy as jnp
from jax import lax
from jax.experimental import pallas as pl
from jax.experimental.pallas import tpu as pltpu

F32 = jnp.float32
BF16 = jnp.bfloat16

D_MODEL = 1024
N_META = 16
A_HEADS = 4
A_DK = 128
A_DV = 128
A_WIDTH = 512
A_QK = A_HEADS * A_DK
B_WIDTH = 512
S5_GROUP = 16
S5_GROUPS = 32
S5_STATE = 64
S5_NSTATE = S5_GROUPS * S5_STATE
C_HEADS = 8
C_HD = 128
D_FF = 4 * D_MODEL
EPS = 1e-6

SUBLANES = 8
LANES = 128
VMEM_LIMIT_BYTES = 48 * 1024 * 1024

HGRN_CHUNK = 64
HGRN_STEP_ROWS = 256
S5_STEP_T = 64
ATT_BLOCK = 128
ATT_SLAB = 256
ATT_DEAD_LOG_WEIGHT = -104.0
NEG_BIG = -1e30


def _cparams(*sem):
    return pltpu.CompilerParams(dimension_semantics=sem, vmem_limit_bytes=VMEM_LIMIT_BYTES)


def _sigmoid(x):
    return 1.0 / (1.0 + jnp.exp(-x))


def _rms_scale(x):
    return lax.rsqrt(jnp.mean(x * x, axis=-1, keepdims=True) + EPS)


def _norm_matmul_kernel(x_ref, g_ref, w_ref, *out_refs, cols, nchunk):
    x = x_ref[...]
    h = (x * _rms_scale(x) * g_ref[...]).astype(BF16)
    for c0, width in sorted(set(cols)):
        for c in range(0, width, nchunk):
            res = jnp.dot(h, w_ref[:, c0 + c:c0 + c + nchunk], preferred_element_type=F32)
            for o_ref, col in zip(out_refs, cols):
                if col == (c0, width):
                    o_ref[:, c:c + nchunk] = res.astype(o_ref.dtype)


def norm_matmul(x, g, w_bf16, outs, *, tm=512, nchunk=512):
    m, d = x.shape
    tm = min(tm, m)
    kern = functools.partial(_norm_matmul_kernel, cols=[(c0, wd) for c0, wd, _ in outs], nchunk=nchunk)
    return pl.pallas_call(
        kern,
        grid=(pl.cdiv(m, tm),),
        in_specs=[pl.BlockSpec((tm, d), lambda i: (i, 0)),
                  pl.BlockSpec((1, d), lambda i: (0, 0)),
                  pl.BlockSpec(w_bf16.shape, lambda i: (0, 0))],
        out_specs=[pl.BlockSpec((tm, wd), lambda i: (i, 0)) for _, wd, _ in outs],
        out_shape=[jax.ShapeDtypeStruct((m, wd), dt) for _, wd, dt in outs],
        compiler_params=_cparams("parallel"),
        name="norm_matmul",
    )(x, g.reshape(1, d), w_bf16)


def _even_out_kernel(x_ref, oa_ref, yg_ref, wglu_ref, wa_ref, wb_ref, o_ref):
    yg = yg_ref[...]
    gate = _sigmoid(jnp.dot(yg.astype(BF16), wglu_ref[...], preferred_element_type=F32))
    ob = (yg * gate).astype(BF16)
    o_ref[...] = (x_ref[...]
                  + jnp.dot(oa_ref[...], wa_ref[...], preferred_element_type=F32)
                  + jnp.dot(ob, wb_ref[...], preferred_element_type=F32))


def even_out(x, o_a, yg, w_glu, w_out_a, w_out_b, *, tm=512):
    m, d = x.shape
    tm = min(tm, m)
    row = lambda i: (i, 0)
    fixed = lambda i: (0, 0)
    return pl.pallas_call(
        _even_out_kernel,
        grid=(pl.cdiv(m, tm),),
        in_specs=[pl.BlockSpec((tm, d), row), pl.BlockSpec((tm, A_WIDTH), row),
                  pl.BlockSpec((tm, B_WIDTH), row), pl.BlockSpec(w_glu.shape, fixed),
                  pl.BlockSpec(w_out_a.shape, fixed), pl.BlockSpec(w_out_b.shape, fixed)],
        out_specs=pl.BlockSpec((tm, d), row),
        out_shape=jax.ShapeDtypeStruct((m, d), F32),
        compiler_params=_cparams("parallel"),
        name="even_out",
    )(x, o_a, yg, w_glu, w_out_a, w_out_b)


def _odd_out_kernel(x_ref, o_ref_in, w_ref, o_ref):
    o_ref[...] = x_ref[...] + jnp.dot(o_ref_in[...], w_ref[...], preferred_element_type=F32)


def odd_out(x, o, w_out, *, tm=512):
    m, d = x.shape
    tm = min(tm, m)
    row = lambda i: (i, 0)
    return pl.pallas_call(
        _odd_out_kernel,
        grid=(pl.cdiv(m, tm),),
        in_specs=[pl.BlockSpec((tm, d), row), pl.BlockSpec((tm, o.shape[1]), row),
                  pl.BlockSpec(w_out.shape, lambda i: (0, 0))],
        out_specs=pl.BlockSpec((tm, d), row),
        out_shape=jax.ShapeDtypeStruct((m, d), F32),
        compiler_params=_cparams("parallel"),
        name="odd_out",
    )(x, o, w_out)


def _mlp_kernel(x_ref, g_ref, wup_ref, wdn_ref, gf_ref, o_ref, h_scr, acc_scr, *, final_norm):
    j = pl.program_id(1)

    @pl.when(j == 0)
    def _():
        x = x_ref[...]
        h_scr[...] = (x * _rms_scale(x) * g_ref[...]).astype(BF16)
        acc_scr[...] = jnp.zeros_like(acc_scr)

    a = jnp.dot(h_scr[...], wup_ref[...], preferred_element_type=F32)
    a = jnp.square(jnp.maximum(a, 0.0)).astype(BF16)
    acc_scr[...] += jnp.dot(a, wdn_ref[...], preferred_element_type=F32)

    @pl.when(j == pl.num_programs(1) - 1)
    def _():
        y = x_ref[...] + acc_scr[...]
        if final_norm:
            y = y * _rms_scale(y) * gf_ref[...]
        o_ref[...] = y


def mlp(x, g, w_up, w_down, g_final, *, final_norm, tm=1024, tf=1024):
    m, d = x.shape
    tm = min(tm, m)
    ff = w_up.shape[1]
    return pl.pallas_call(
        functools.partial(_mlp_kernel, final_norm=final_norm),
        grid=(pl.cdiv(m, tm), ff // tf),
        in_specs=[pl.BlockSpec((tm, d), lambda i, j: (i, 0)),
                  pl.BlockSpec((1, d), lambda i, j: (0, 0)),
                  pl.BlockSpec((d, tf), lambda i, j: (0, j)),
                  pl.BlockSpec((tf, d), lambda i, j: (j, 0)),
                  pl.BlockSpec((1, d), lambda i, j: (0, 0))],
        out_specs=pl.BlockSpec((tm, d), lambda i, j: (i, 0)),
        out_shape=jax.ShapeDtypeStruct((m, d), F32),
        scratch_shapes=[pltpu.VMEM((tm, d), BF16), pltpu.VMEM((tm, d), F32)],
        compiler_params=_cparams("parallel", "arbitrary"),
        name="mlp",
    )(x, g.reshape(1, d), w_up, w_down, g_final.reshape(1, d))


def _split3(x):
    hi = x.astype(BF16)
    r1 = x - hi.astype(F32)
    mid = r1.astype(BF16)
    lo = (r1 - mid.astype(F32)).astype(BF16)
    return hi, mid, lo


def _hgrn_offdiag_mask(chunk):
    n_blk = chunk // SUBLANES
    seg = np.concatenate([np.full(SUBLANES * i, i) for i in range(1, n_blk)])
    blk = np.repeat(np.arange(1, n_blk), SUBLANES)
    return (blk[:, None] == seg[None, :]).astype(np.float32)


def _hgrn_kernel(*refs, chunk, n_sub):
    n_blk = chunk // SUBLANES
    row_scr = [refs[len(refs) - 3 * (n_sub - u):len(refs) - 3 * (n_sub - u - 1)] for u in range(n_sub)]
    refs = refs[:-3 * n_sub]
    if n_blk > 1:
        q_ref, zf_ref, iv_ref, g_ref, lb_ref, gn_ref, s0_ref, mask_ref, o_ref, sout_ref, s_scr = refs
    else:
        q_ref, zf_ref, iv_ref, g_ref, lb_ref, gn_ref, s0_ref, o_ref, sout_ref, s_scr = refs
    step = pl.program_id(1)

    @pl.when(step == 0)
    def _():
        s_scr[...] = s0_ref[...]

    lb = lb_ref[...]
    gn = gn_ref[...]
    rows = lax.broadcasted_iota(jnp.int32, (chunk, chunk), 0)
    cols = lax.broadcasted_iota(jnp.int32, (chunk, chunk), 1)
    tri = jnp.where(rows >= cols, 1.0, 0.0).astype(BF16)
    hrow = lax.broadcasted_iota(jnp.int32, (A_QK, A_QK), 0) // A_DK
    hcol = lax.broadcasted_iota(jnp.int32, (A_QK, A_QK), 1) // A_DK
    head_ones = jnp.where(hrow == hcol, 1.0, 0.0).astype(BF16)
    sub = lax.broadcasted_iota(jnp.int32, (SUBLANES, A_QK), 0)
    head_lanes = [slice(hh * A_DK, (hh + 1) * A_DK) for hh in range(A_HEADS)]
    nt = (((1,), (1,)), ((), ()))
    tn = (((0,), (0,)), ((), ()))

    def pairwise_phase(cc, b_scr, k_scr, iv_scr):
        r0 = cc * chunk
        q = q_ref[pl.ds(r0, chunk), :]
        zf = zf_ref[pl.ds(r0, chunk), :]
        iv = iv_ref[pl.ds(r0, chunk), :]
        g = g_ref[pl.ds(r0, chunk), :]
        f = lb + (1.0 - lb) * _sigmoid(zf)
        logf = jnp.log(f)
        kk = 1.0 - f
        hi, mid, lo = _split3(logf)
        b = (jnp.dot(tri, hi, preferred_element_type=F32)
             + jnp.dot(tri, mid, preferred_element_type=F32)
             + jnp.dot(tri, lo, preferred_element_type=F32))
        b_scr[...] = b
        k_scr[...] = kk
        iv_scr[...] = iv

        ws = []
        for i in range(n_blk):
            b_i = b[SUBLANES * i:SUBLANES * (i + 1)]
            q_i = q[SUBLANES * i:SUBLANES * (i + 1)]
            for s in range(SUBLANES):
                row = SUBLANES * i + s
                dlt = jnp.where(sub >= s, b_i - b_scr[row:row + 1, :], NEG_BIG)
                ws.append(jnp.exp(dlt) * q_i * k_scr[row:row + 1, :])
        w = jnp.concatenate(ws, axis=0).astype(BF16)
        return r0, q, iv, g, kk, b, w, b_scr, iv_scr

    def matmul_phase(r0, q, iv, g, kk, b, w, b_scr, iv_scr):
        att = jnp.dot(w, head_ones, preferred_element_type=F32)
        o_blocks = []
        for i in range(n_blk):
            acc = jnp.zeros((SUBLANES, A_WIDTH), F32)
            for s in range(SUBLANES):
                row = SUBLANES * i + s
                acc = acc + att[SUBLANES * row:SUBLANES * (row + 1)] * iv_scr[row:row + 1, :]
            o_blocks.append(acc)
        o = jnp.concatenate(o_blocks, axis=0)

        if n_blk > 1:
            qt, kh, ivs = [], [], []
            for i in range(1, n_blk):
                n_s = SUBLANES * i
                r_i = b_scr[n_s - 1:n_s, :]
                qt.append(q[n_s:n_s + SUBLANES] * jnp.exp(b[n_s:n_s + SUBLANES] - r_i))
                kh.append(kk[:n_s] * jnp.exp(r_i - b[:n_s]))
                ivs.append(iv[:n_s])
            qt = jnp.concatenate(qt, axis=0).astype(BF16)
            kh = jnp.concatenate(kh, axis=0).astype(BF16)
            ivs = jnp.concatenate(ivs, axis=0).astype(BF16)
            mask = mask_ref[...]
            o_off = []
            for l in head_lanes:
                a = lax.dot_general(qt[:, l], kh[:, l], nt, preferred_element_type=F32)
                o_off.append(jnp.dot((a * mask).astype(BF16), ivs[:, l], preferred_element_type=F32))
            o_off = jnp.concatenate(o_off, axis=1)
            o = o + jnp.concatenate([jnp.zeros((SUBLANES, A_WIDTH), F32), o_off], axis=0)

        qh = (q * jnp.exp(b)).astype(BF16)
        b_last = b[chunk - 1:chunk, :]
        kd = (kk * jnp.exp(b_last - b)).astype(BF16)
        dec = jnp.exp(b_last)
        ivb = iv.astype(BF16)
        o_inter = []
        for hh, l in enumerate(head_lanes):
            s_t = s_scr[hh]
            o_inter.append(lax.dot_general(qh[:, l], s_t.astype(BF16), nt, preferred_element_type=F32))
            upd = lax.dot_general(ivb[:, l], kd[:, l], tn, preferred_element_type=F32)
            s_scr[hh] = s_t * dec[:, l] + upd
        o = o + jnp.concatenate(o_inter, axis=1)

        o = jnp.concatenate([o[:, l] * _rms_scale(o[:, l]) * gn for l in head_lanes], axis=1)
        o = o * (g * _sigmoid(g))
        o_ref[pl.ds(r0, chunk), :] = o.astype(o_ref.dtype)

    pending = None
    for u in range(n_sub):
        ctx = pairwise_phase(u, *row_scr[u])
        if pending is not None:
            matmul_phase(*pending)
        pending = ctx
    matmul_phase(*pending)

    @pl.when(step == pl.num_programs(1) - 1)
    def _():
        sout_ref[...] = s_scr[...]


def hgrn(q, zf, iv, g, lb, gnorm, s0_t, bsz, t_len):
    chunk = HGRN_CHUNK if t_len % HGRN_CHUNK == 0 else SUBLANES
    step_rows = min(HGRN_STEP_ROWS, t_len)
    n_steps = t_len // step_rows
    tok = pl.BlockSpec((step_rows, A_QK), lambda b, c: (b * n_steps + c, 0))
    s0_b = (lambda b: b) if s0_t.shape[0] == bsz else (lambda b: 0)
    in_specs = [tok, tok, tok, tok,
                pl.BlockSpec((1, A_QK), lambda b, c: (0, 0)),
                pl.BlockSpec((1, A_DV), lambda b, c: (0, 0)),
                pl.BlockSpec((None, A_HEADS, A_DV, A_DK), lambda b, c: (s0_b(b), 0, 0, 0))]
    args = [q, zf, iv, g, lb.reshape(1, A_QK), gnorm.reshape(1, A_DV), s0_t]
    if chunk > SUBLANES:
        mask = jnp.asarray(_hgrn_offdiag_mask(chunk))
        in_specs.append(pl.BlockSpec(mask.shape, lambda b, c: (0, 0)))
        args.append(mask)
    return pl.pallas_call(
        functools.partial(_hgrn_kernel, chunk=chunk, n_sub=step_rows // chunk),
        grid=(bsz, n_steps),
        in_specs=in_specs,
        out_specs=[tok, pl.BlockSpec((None, A_HEADS, A_DV, A_DK), lambda b, c: (b, 0, 0, 0))],
        out_shape=[jax.ShapeDtypeStruct((bsz * t_len, A_WIDTH), BF16),
                   jax.ShapeDtypeStruct((bsz, A_HEADS, A_DV, A_DK), F32)],
        scratch_shapes=([pltpu.VMEM((A_HEADS, A_DV, A_DK), F32)]
                        + [pltpu.VMEM((chunk, A_QK), F32)] * (3 * (step_rows // chunk))),
        compiler_params=_cparams("parallel", "arbitrary"),
        name="hgrn",
    )(*args)


def s5_weights(a_re, a_im, log_dt, b_re, b_im, c_re, c_im, d_skip):
    f32 = F32
    ar = a_re.astype(f32)
    ai = a_im.astype(f32)
    dt = jnp.exp(log_dt.astype(f32))[:, None]
    mag = jnp.exp(dt * ar)
    abar_re = mag * jnp.cos(dt * ai)
    abar_im = mag * jnp.sin(dt * ai)
    den = ar * ar + ai * ai
    zr = ((abar_re - 1.0) * ar + abar_im * ai) / den
    zi = (abar_im * ar - (abar_re - 1.0) * ai) / den
    br, bi = b_re.astype(f32), b_im.astype(f32)
    bb_re = zr[..., None] * br - zi[..., None] * bi
    bb_im = zr[..., None] * bi + zi[..., None] * br
    eye = jnp.eye(S5_GROUPS, dtype=f32)

    def in_proj(bb):
        return jnp.einsum('gnp,gh->gphn', bb, eye).reshape(B_WIDTH, S5_NSTATE)

    def out_proj(c):
        return jnp.einsum('gpn,gh->gnhp', c, eye).reshape(S5_NSTATE, B_WIDTH)

    n_sb = B_WIDTH // LANES
    sw = S5_NSTATE // n_sb

    def diag_blocks(full, rows, cols):
        return jnp.stack([full[c * rows:(c + 1) * rows, c * cols:(c + 1) * cols] for c in range(n_sb)])

    b_blk = jnp.concatenate([diag_blocks(in_proj(bb_re), LANES, sw), diag_blocks(in_proj(bb_im), LANES, sw)],
                            axis=2).astype(BF16)
    c_blk = jnp.concatenate([diag_blocks(out_proj(c_re.astype(f32)), sw, LANES),
                             -diag_blocks(out_proj(c_im.astype(f32)), sw, LANES)],
                            axis=1).astype(BF16)
    return dict(b_blk=b_blk, c_blk=c_blk, a_re=abar_re.reshape(1, S5_NSTATE),
                a_im=abar_im.reshape(1, S5_NSTATE), d=d_skip.astype(f32).reshape(1, B_WIDTH))


def _s5_kernel(u_ref, x0re_ref, x0im_ref, bblk_ref, cblk_ref, are_ref, aim_ref, d_ref,
               y_ref, fre_ref, fim_ref, u_scr, bu_scr, xre_scr, xim_scr, *, tt, nb):
    step = pl.program_id(0)

    @pl.when(step == 0)
    def _():
        xre_scr[...] = x0re_ref[...]
        xim_scr[...] = x0im_ref[...]

    n_lt = B_WIDTH // LANES
    u_bt = u_ref[...].reshape(nb * tt, B_WIDTH)
    for c in range(n_lt):
        u_scr[c] = u_bt[:, c * LANES:(c + 1) * LANES]
    sw = S5_NSTATE // n_lt
    for c in range(n_lt):
        st_l = slice(c * sw, (c + 1) * sw)
        ch_l = slice(c * LANES, (c + 1) * LANES)
        u_tb = jnp.concatenate([u_scr[c, pl.ds(t, nb, stride=tt), :] for t in range(tt)], axis=0)
        bu_scr[c] = jnp.dot(u_tb.astype(BF16), bblk_ref[c], preferred_element_type=F32)
        a_re = jnp.broadcast_to(are_ref[:, st_l], (nb, sw))
        a_im = jnp.broadcast_to(aim_ref[:, st_l], (nb, sw))

        def body(t, st, c=c, a_re=a_re, a_im=a_im):
            x_re, x_im = st
            r0 = pl.multiple_of(t * nb, nb)
            n_re = a_re * x_re - a_im * x_im + bu_scr[c, pl.ds(r0, nb), :sw]
            n_im = a_re * x_im + a_im * x_re + bu_scr[c, pl.ds(r0, nb), sw:]
            bu_scr[c, pl.ds(r0, nb), :sw] = n_re
            bu_scr[c, pl.ds(r0, nb), sw:] = n_im
            return n_re, n_im

        x_re, x_im = lax.fori_loop(0, tt, body, (xre_scr[:, st_l], xim_scr[:, st_l]))
        xre_scr[:, st_l] = x_re
        xim_scr[:, st_l] = x_im

        y = (jnp.dot(bu_scr[c].astype(BF16), cblk_ref[c], preferred_element_type=F32)
             + d_ref[:, ch_l] * u_tb)
        y = 0.5 * y * (1.0 + lax.erf(y * (1.0 / math.sqrt(2.0))))
        for t in range(tt):
            u_scr[c, pl.ds(t, nb, stride=tt), :] = y[nb * t:nb * (t + 1)]
    y_ref[...] = jnp.concatenate([u_scr[c] for c in range(n_lt)], axis=1).reshape(nb, tt, B_WIDTH)

    @pl.when(step == pl.num_programs(0) - 1)
    def _():
        fre_ref[...] = xre_scr[...]
        fim_ref[...] = xim_scr[...]


def s5(u, wts, x0_re, x0_im, bsz, t_len):
    nb = SUBLANES
    tt = min(S5_STEP_T, t_len)
    u3 = u.reshape(bsz, t_len, B_WIDTH)
    x0 = [x.astype(F32).reshape(bsz, S5_NSTATE) for x in (x0_re, x0_im)]
    if bsz < nb:
        u3 = jnp.pad(u3, ((0, nb - bsz), (0, 0), (0, 0)))
        x0 = [jnp.pad(x, ((0, nb - bsz), (0, 0))) for x in x0]
    fixed = lambda i: (0, 0)
    fixed3 = lambda i: (0, 0, 0)
    n_sb = B_WIDTH // LANES
    sw2 = 2 * S5_NSTATE // n_sb
    tok = pl.BlockSpec((nb, tt, B_WIDTH), lambda i: (0, i, 0))
    state = pl.BlockSpec((nb, S5_NSTATE), fixed)
    yg, fre, fim = pl.pallas_call(
        functools.partial(_s5_kernel, tt=tt, nb=nb),
        grid=(t_len // tt,),
        in_specs=[tok, state, state,
                  pl.BlockSpec((n_sb, LANES, sw2), fixed3), pl.BlockSpec((n_sb, sw2, LANES), fixed3),
                  pl.BlockSpec((1, S5_NSTATE), fixed), pl.BlockSpec((1, S5_NSTATE), fixed),
                  pl.BlockSpec((1, B_WIDTH), fixed)],
        out_specs=[tok, state, state],
        out_shape=[jax.ShapeDtypeStruct((nb, t_len, B_WIDTH), F32),
                   jax.ShapeDtypeStruct((nb, S5_NSTATE), F32), jax.ShapeDtypeStruct((nb, S5_NSTATE), F32)],
        scratch_shapes=[pltpu.VMEM((n_sb, nb * tt, LANES), F32),
                        pltpu.VMEM((n_sb, nb * tt, sw2), F32),
                        pltpu.VMEM((nb, S5_NSTATE), F32), pltpu.VMEM((nb, S5_NSTATE), F32)],
        compiler_params=_cparams("arbitrary"),
        name="s5",
    )(u3, x0[0], x0[1], wts['b_blk'], wts['c_blk'], wts['a_re'], wts['a_im'], wts['d'])
    yg = yg[:bsz].reshape(bsz * t_len, B_WIDTH)
    return (yg, fre[:bsz].reshape(bsz, S5_GROUPS, S5_STATE), fim[:bsz].reshape(bsz, S5_GROUPS, S5_STATE))


def _sb_attn_kernel(*refs, heads, n_past, p_valid, scale):
    if n_past:
        q_ref, k_ref, v_ref, pk_ref, pv_ref, o_ref, carry_scr, acc_scr = refs
    else:
        q_ref, k_ref, v_ref, o_ref, carry_scr, acc_scr = refs
    blk = ATT_BLOCK
    i = pl.program_id(1)

    def suffix_matrix(tk):
        srow = lax.broadcasted_iota(jnp.int32, (tk, tk), 0)
        scol = lax.broadcasted_iota(jnp.int32, (tk, tk), 1)
        return jnp.concatenate([jnp.where(srow > scol, 1.0, 0.0), jnp.ones((tk, blk), F32)],
                               axis=1).astype(BF16)

    suffix = {tk: suffix_matrix(tk) for tk in (blk, ATT_SLAB)}
    rows = lax.broadcasted_iota(jnp.int32, (blk, blk), 0)
    cols = lax.broadcasted_iota(jnp.int32, (blk, blk), 1)

    carry_scr[...] = jnp.zeros_like(carry_scr)
    acc_scr[...] = jnp.zeros_like(acc_scr)
    head_lanes = [slice(hh * C_HD, (hh + 1) * C_HD) for hh in range(heads)]
    n_rows = heads * blk

    def per_head(x, fn):
        return jnp.concatenate([fn(x[hh * blk:(hh + 1) * blk]) for hh in range(heads)], axis=0)

    def visit(kr, vr, rj, tk, mask):
        z = jnp.concatenate(
            [lax.dot_general(q_ref[:, l], kr[pl.ds(rj, tk), l], (((1,), (1,)), ((), ())),
                             preferred_element_type=F32) for l in head_lanes], axis=0) * scale
        log_b = jnp.minimum(z, 0.0) - jnp.log(1.0 + jnp.exp(-jnp.abs(z)))
        x = log_b - z
        if mask is not None:
            x = per_head(x, lambda xh: jnp.where(mask, xh, 0.0))
        hi = x.astype(BF16)
        lo = (x - hi.astype(F32)).astype(BF16)
        cs = jnp.dot(jnp.concatenate([hi, lo], axis=0), suffix[tk], preferred_element_type=F32)
        cs = cs[:n_rows] + cs[n_rows:]
        carry = carry_scr[...]
        w = jnp.exp(log_b + cs[:, :tk] + jnp.concatenate([carry] * (tk // blk), axis=1))
        if mask is not None:
            w = per_head(w, lambda wh: jnp.where(mask, wh, 0.0))
        w = w.astype(BF16)
        acc_scr[...] += jnp.concatenate(
            [jnp.dot(w[hh * blk:(hh + 1) * blk], vr[pl.ds(rj, tk), l], preferred_element_type=F32)
             for hh, l in enumerate(head_lanes)], axis=0)
        carry = carry + cs[:, tk:]
        carry_scr[...] = carry
        return jnp.max(carry)

    live = visit(k_ref, v_ref, pl.multiple_of(i * blk, blk), blk, cols < rows)

    def slab_body(st):
        j, _ = st
        return j - 2, visit(k_ref, v_ref, pl.multiple_of((j - 1) * blk, blk), ATT_SLAB, None)

    def block_body(st):
        j, _ = st
        return j - 1, visit(k_ref, v_ref, pl.multiple_of(j * blk, blk), blk, None)

    def alive_from(first):
        return lambda st: jnp.logical_and(st[0] >= first, st[1] > ATT_DEAD_LOG_WEIGHT)

    j, live = lax.while_loop(alive_from(1), slab_body, (i - 1, live))
    _, live = lax.while_loop(alive_from(0), block_body, (j, live))

    if n_past:
        def past_body(st):
            j, _ = st
            return j - 1, visit(pk_ref, pv_ref, pl.multiple_of(j * blk, blk), blk, (cols + j * blk) < p_valid)

        lax.while_loop(alive_from(0), past_body, (jnp.int32(n_past - 1), live))

    for hh in range(heads):
        o_ref[:, head_lanes[hh]] = acc_scr[hh * blk:(hh + 1) * blk, :].astype(o_ref.dtype)


def sb_attention(q, k, v, past_k, past_v, p_valid):
    bsz, t_len, width = q.shape
    n_past = 0 if past_k is None else past_k.shape[1] // ATT_BLOCK
    q_spec = pl.BlockSpec((None, ATT_BLOCK, width), lambda b, i: (b, i, 0))
    kv_spec = pl.BlockSpec((None, t_len, width), lambda b, i: (b, 0, 0), pipeline_mode=pl.Buffered(1))
    in_specs = [q_spec, kv_spec, kv_spec]
    args = [q, k, v]
    if n_past:
        pb = (lambda b: b) if past_k.shape[0] == bsz else (lambda b: 0)
        p_spec = pl.BlockSpec((None, past_k.shape[1], width), lambda b, i: (pb(b), 0, 0),
                              pipeline_mode=pl.Buffered(1))
        in_specs += [p_spec, p_spec]
        args += [past_k, past_v]
    return pl.pallas_call(
        functools.partial(_sb_attn_kernel, heads=C_HEADS, n_past=n_past, p_valid=p_valid,
                          scale=C_HD ** -0.5),
        grid=(bsz, t_len // ATT_BLOCK),
        in_specs=in_specs,
        out_specs=q_spec,
        out_shape=jax.ShapeDtypeStruct((bsz, t_len, width), BF16),
        scratch_shapes=[pltpu.VMEM((C_HEADS * ATT_BLOCK, C_HD), F32)] * 2,
        compiler_params=_cparams("parallel", "arbitrary"),
        name="sb_attention",
    )(*args)


def _pad_rows(x, mult):
    t = x.shape[1]
    tp = -(-t // mult) * mult
    return x if tp == t else jnp.pad(x, ((0, 0), (0, tp - t), (0, 0)))


def _run_stream(x, bsz, t_len, wts, hgrn_s0_t, ssm0_re, ssm0_im, past_k, past_v, p_valid):
    q, zf, iv, g, u = norm_matmul(
        x, wts['ln_mix'][0], wts['w_in_even'],
        [(0, A_QK, F32), (A_QK, A_QK, F32), (2 * A_QK, A_WIDTH, F32),
         (2 * A_QK + A_WIDTH, A_WIDTH, F32), (2 * A_QK + 2 * A_WIDTH, B_WIDTH, F32)])
    o_a, s_t = hgrn(q, zf, iv, g, wts['lb'], wts['hgrn_norm'], hgrn_s0_t, bsz, t_len)
    yg, x_re, x_im = s5(u, wts['s5'], ssm0_re, ssm0_im, bsz, t_len)
    x = even_out(x, o_a, yg, wts['w_glu'], wts['w_out_even_a'], wts['w_out_even_b'])
    x = mlp(x, wts['ln_mlp'][0], wts['w_up'][0], wts['w_down'][0], wts['ln_final'], final_norm=False)
    qb, kb, vb, k32, v32 = norm_matmul(
        x, wts['ln_mix'][1], wts['w_in_odd'],
        [(0, D_MODEL, BF16), (D_MODEL, D_MODEL, BF16), (2 * D_MODEL, D_MODEL, BF16),
         (D_MODEL, D_MODEL, F32), (2 * D_MODEL, D_MODEL, F32)])
    shp = (bsz, t_len, D_MODEL)
    o = sb_attention(_pad_rows(qb.reshape(shp), ATT_BLOCK), _pad_rows(kb.reshape(shp), ATT_BLOCK),
                     _pad_rows(vb.reshape(shp), ATT_BLOCK), past_k, past_v, p_valid)
    o = o[:, :t_len].reshape(bsz * t_len, D_MODEL)
    x = odd_out(x, o, wts['w_out_odd'])
    y = mlp(x, wts['ln_mlp'][1], wts['w_up'][1], wts['w_down'][1], wts['ln_final'], final_norm=True)
    return y, s_t, x_re, x_im, k32, v32


def kernel(x_prompt, x_sample, state_hgrn, state_ssm_re, state_ssm_im, cache_k, cache_v, meta_tokens,
           ln_mix, ln_mlp, ln_final, w_in_even, hgrn_lb, hgrn_norm, ssm_a_re, ssm_a_im, ssm_log_dt,
           ssm_b_re, ssm_b_im, ssm_c_re, ssm_c_im, ssm_d, w_glu, w_out_even, w_in_odd, w_out_odd,
           w_up, w_down):
    bsz, seq, _ = x_prompt.shape
    dbsz, dseq, _ = x_sample.shape
    past = cache_k.shape[2]
    lb_all = jnp.cumsum(jax.nn.softmax(hgrn_lb.astype(F32), axis=0), axis=0)
    w_out_e = w_out_even[0].astype(BF16)
    wts = dict(
        ln_mix=ln_mix.astype(F32), ln_mlp=ln_mlp.astype(F32), ln_final=ln_final.astype(F32),
        w_in_even=w_in_even[0].astype(BF16), lb=lb_all[0], hgrn_norm=hgrn_norm[0].astype(F32),
        s5=s5_weights(ssm_a_re[0], ssm_a_im[0], ssm_log_dt[0], ssm_b_re[0], ssm_b_im[0],
                      ssm_c_re[0], ssm_c_im[0], ssm_d[0]),
        w_glu=w_glu[0].astype(BF16), w_out_even_a=w_out_e[:A_WIDTH], w_out_even_b=w_out_e[A_WIDTH:],
        w_in_odd=w_in_odd[0].astype(BF16), w_out_odd=w_out_odd[0].astype(BF16),
        w_up=w_up.astype(BF16), w_down=w_down.astype(BF16))

    zeros_s = jnp.zeros((1, A_HEADS, A_DV, A_DK), F32)
    zeros_x = jnp.zeros((1, S5_GROUPS, S5_STATE), F32)
    _, m_s, m_re, m_im, m_k, m_v = _run_stream(
        meta_tokens.astype(F32), 1, N_META, wts, zeros_s, zeros_x, zeros_x, None, None, 0)

    m_kp = _pad_rows(m_k.astype(BF16)[None], ATT_BLOCK)
    m_vp = _pad_rows(m_v.astype(BF16)[None], ATT_BLOCK)
    y_p, s_p, re_p, im_p, k_p, v_p = _run_stream(
        x_prompt.reshape(bsz * seq, D_MODEL), bsz, seq, wts, m_s,
        jnp.broadcast_to(m_re, (bsz, S5_GROUPS, S5_STATE)), jnp.broadcast_to(m_im, (bsz, S5_GROUPS, S5_STATE)),
        m_kp, m_vp, N_META)

    ck = cache_k[0].reshape(dbsz, past, D_MODEL).astype(BF16)
    cv = cache_v[0].reshape(dbsz, past, D_MODEL).astype(BF16)
    y_s, s_s, re_s, im_s, k_s, v_s = _run_stream(
        x_sample.reshape(dbsz * dseq, D_MODEL), dbsz, dseq, wts,
        jnp.swapaxes(state_hgrn[0].astype(F32), -1, -2), state_ssm_re[0], state_ssm_im[0],
        _pad_rows(ck, ATT_BLOCK), _pad_rows(cv, ATT_BLOCK), past)

    def with_meta(meta_rows, main):
        meta_b = jnp.broadcast_to(meta_rows[None], (bsz, N_META, D_MODEL))
        full = jnp.concatenate([meta_b, main.reshape(bsz, seq, D_MODEL)], axis=1)
        return full.reshape(1, bsz, N_META + seq, C_HEADS, C_HD)

    return (y_p.reshape(bsz, seq, D_MODEL), y_s.reshape(dbsz, dseq, D_MODEL),
            jnp.swapaxes(s_p, -1, -2)[None], jnp.swapaxes(s_s, -1, -2)[None],
            re_p[None], im_p[None], re_s[None], im_s[None],
            with_meta(m_k, k_p), with_meta(m_v, v_p),
            k_s.reshape(1, dbsz, dseq, C_HEADS, C_HD), v_s.reshape(1, dbsz, dseq, C_HEADS, C_HD))
```

```python
import functools
import math

import numpy as np
import jax
import jax.numpy as jnp
from jax import lax
from jax.experimental import pallas as pl
from jax.experimental.pallas import tpu as pltpu

F32 = jnp.float32
BF16 = jnp.bfloat16

D_MODEL = 1024
N_META = 16
A_HEADS = 4
A_DK = 128
A_DV = 128
A_WIDTH = 512
A_QK = A_HEADS * A_DK
B_WIDTH = 512
S5_GROUP = 16
S5_GROUPS = 32
S5_STATE = 64
S5_NSTATE = S5_GROUPS * S5_STATE
C_HEADS = 8
C_HD = 128
D_FF = 4 * D_MODEL
EPS = 1e-6

SUBLANES = 8
LANES = 128
VMEM_LIMIT_BYTES = 48 * 1024 * 1024

HGRN_CHUNK = 64
HGRN_STEP_ROWS = 256
S5_STEP_T = 64
ATT_BLOCK = 128
ATT_SLAB = 256
ATT_DEAD_LOG_WEIGHT = -104.0
NEG_BIG = -1e30


def _cparams(*sem):
    return pltpu.CompilerParams(dimension_semantics=sem, vmem_limit_bytes=VMEM_LIMIT_BYTES)


def _sigmoid(x):
    return 1.0 / (1.0 + jnp.exp(-x))


def _rms_scale(x):
    return lax.rsqrt(jnp.mean(x * x, axis=-1, keepdims=True) + EPS)


def _norm_matmul_kernel(x_ref, g_ref, w_ref, *out_refs, cols, nchunk):
    x = x_ref[...]
    h = (x * _rms_scale(x) * g_ref[...]).astype(BF16)
    for c0, width in sorted(set(cols)):
        for c in range(0, width, nchunk):
            res = jnp.dot(h, w_ref[:, c0 + c:c0 + c + nchunk], preferred_element_type=F32)
            for o_ref, col in zip(out_refs, cols):
                if col == (c0, width):
                    o_ref[:, c:c + nchunk] = res.astype(o_ref.dtype)


def norm_matmul(x, g, w_bf16, outs, *, tm=512, nchunk=512):
    m, d = x.shape
    tm = min(tm, m)
    kern = functools.partial(_norm_matmul_kernel, cols=[(c0, wd) for c0, wd, _ in outs], nchunk=nchunk)
    return pl.pallas_call(
        kern,
        grid=(pl.cdiv(m, tm),),
        in_specs=[pl.BlockSpec((tm, d), lambda i: (i, 0)),
                  pl.BlockSpec((1, d), lambda i: (0, 0)),
                  pl.BlockSpec(w_bf16.shape, lambda i: (0, 0))],
        out_specs=[pl.BlockSpec((tm, wd), lambda i: (i, 0)) for _, wd, _ in outs],
        out_shape=[jax.ShapeDtypeStruct((m, wd), dt) for _, wd, dt in outs],
        compiler_params=_cparams("parallel"),
        name="norm_matmul",
    )(x, g.reshape(1, d), w_bf16)


def _qkv_shifted_kernel(x_ref, g_ref, w_ref, lead_ref, qb_ref, kb_ref, vb_ref, kf_ref, vf_ref, carry_scr,
                        *, n_lead, nchunk):
    t = pl.program_id(1)
    nt = pl.num_programs(1) - 1
    tm, width = qb_ref.shape

    @pl.when(t == 0)
    def _():
        carry_scr[...] = lead_ref[...]

    for n, f_ref in enumerate((kf_ref, vf_ref)):
        f_ref[:n_lead, :] = carry_scr[n]

    @pl.when(t < nt)
    def _():
        x = x_ref[...]
        h = (x * _rms_scale(x) * g_ref[...]).astype(BF16)
        for n, (b_ref, f_ref) in enumerate(((qb_ref, None), (kb_ref, kf_ref), (vb_ref, vf_ref))):
            for c in range(0, width, nchunk):
                res = jnp.dot(h, w_ref[:, n * width + c:n * width + c + nchunk], preferred_element_type=F32)
                b_ref[:, c:c + nchunk] = res.astype(b_ref.dtype)
                if f_ref is not None:
                    f_ref[n_lead:, c:c + nchunk] = res[:tm - n_lead]
                    carry_scr[n - 1, :, c:c + nchunk] = res[tm - n_lead:]


def qkv_shifted(x, g, w_bf16, lead_k, lead_v, bsz, t_len, *, tm=512, nchunk=512):
    m, d = x.shape
    width = w_bf16.shape[1] // 3
    n_lead = lead_k.shape[0]
    tm = math.gcd(tm, t_len)
    nt = t_len // tm
    tok = lambda b, t: (b * nt + jnp.minimum(t, nt - 1), 0)
    return pl.pallas_call(
        functools.partial(_qkv_shifted_kernel, n_lead=n_lead, nchunk=nchunk),
        grid=(bsz, nt + 1),
        in_specs=[pl.BlockSpec((tm, d), tok),
                  pl.BlockSpec((1, d), lambda b, t: (0, 0)),
                  pl.BlockSpec(w_bf16.shape, lambda b, t: (0, 0)),
                  pl.BlockSpec((2, n_lead, width), lambda b, t: (0, 0, 0))],
        out_specs=[pl.BlockSpec((tm, width), tok)] * 3
                  + [pl.BlockSpec((None, tm, width), lambda b, t: (b, t, 0))] * 2,
        out_shape=[jax.ShapeDtypeStruct((m, width), BF16)] * 3
                  + [jax.ShapeDtypeStruct((bsz, n_lead + t_len, width), F32)] * 2,
        scratch_shapes=[pltpu.VMEM((2, n_lead, width), F32)],
        compiler_params=_cparams("parallel", "arbitrary"),
        name="qkv_shifted",
    )(x, g.reshape(1, d), w_bf16, jnp.stack([lead_k, lead_v]))


def _even_out_kernel(x_ref, oa_ref, yg_ref, wglu_ref, wa_ref, wb_ref, o_ref):
    yg = yg_ref[...]
    gate = _sigmoid(jnp.dot(yg.astype(BF16), wglu_ref[...], preferred_element_type=F32))
    ob = (yg.astype(F32) * gate).astype(BF16)
    o_ref[...] = (x_ref[...]
                  + jnp.dot(oa_ref[...], wa_ref[...], preferred_element_type=F32)
                  + jnp.dot(ob, wb_ref[...], preferred_element_type=F32))


def even_out(x, o_a, yg, w_glu, w_out_a, w_out_b, *, tm=512):
    m, d = x.shape
    tm = min(tm, m)
    row = lambda i: (i, 0)
    fixed = lambda i: (0, 0)
    return pl.pallas_call(
        _even_out_kernel,
        grid=(pl.cdiv(m, tm),),
        in_specs=[pl.BlockSpec((tm, d), row), pl.BlockSpec((tm, A_WIDTH), row),
                  pl.BlockSpec((tm, B_WIDTH), row), pl.BlockSpec(w_glu.shape, fixed),
                  pl.BlockSpec(w_out_a.shape, fixed), pl.BlockSpec(w_out_b.shape, fixed)],
        out_specs=pl.BlockSpec((tm, d), row),
        out_shape=jax.ShapeDtypeStruct((m, d), F32),
        compiler_params=_cparams("parallel"),
        name="even_out",
    )(x, o_a, yg, w_glu, w_out_a, w_out_b)


def _odd_out_kernel(x_ref, o_ref_in, w_ref, o_ref):
    o_ref[...] = x_ref[...] + jnp.dot(o_ref_in[...], w_ref[...], preferred_element_type=F32)


def odd_out(x, o, w_out, *, tm=512):
    m, d = x.shape
    tm = min(tm, m)
    row = lambda i: (i, 0)
    return pl.pallas_call(
        _odd_out_kernel,
        grid=(pl.cdiv(m, tm),),
        in_specs=[pl.BlockSpec((tm, d), row), pl.BlockSpec((tm, o.shape[1]), row),
                  pl.BlockSpec(w_out.shape, lambda i: (0, 0))],
        out_specs=pl.BlockSpec((tm, d), row),
        out_shape=jax.ShapeDtypeStruct((m, d), F32),
        compiler_params=_cparams("parallel"),
        name="odd_out",
    )(x, o, w_out)


def _mlp_kernel(x_ref, g_ref, wup_ref, wdn_ref, gf_ref, o_ref, h_scr, acc_scr, *, final_norm):
    j = pl.program_id(1)

    @pl.when(j == 0)
    def _():
        x = x_ref[...]
        h_scr[...] = (x * _rms_scale(x) * g_ref[...]).astype(BF16)
        acc_scr[...] = jnp.zeros_like(acc_scr)

    a = jnp.dot(h_scr[...], wup_ref[...], preferred_element_type=F32)
    a = jnp.square(jnp.maximum(a, 0.0)).astype(BF16)
    acc_scr[...] += jnp.dot(a, wdn_ref[...], preferred_element_type=F32)

    @pl.when(j == pl.num_programs(1) - 1)
    def _():
        y = x_ref[...] + acc_scr[...]
        if final_norm:
            y = y * _rms_scale(y) * gf_ref[...]
        o_ref[...] = y


def mlp(x, g, w_up, w_down, g_final, *, final_norm, tm=1024, tf=1024):
    m, d = x.shape
    tm = min(tm, m)
    ff = w_up.shape[1]
    return pl.pallas_call(
        functools.partial(_mlp_kernel, final_norm=final_norm),
        grid=(pl.cdiv(m, tm), ff // tf),
        in_specs=[pl.BlockSpec((tm, d), lambda i, j: (i, 0)),
                  pl.BlockSpec((1, d), lambda i, j: (0, 0)),
                  pl.BlockSpec((d, tf), lambda i, j: (0, j)),
                  pl.BlockSpec((tf, d), lambda i, j: (j, 0)),
                  pl.BlockSpec((1, d), lambda i, j: (0, 0))],
        out_specs=pl.BlockSpec((tm, d), lambda i, j: (i, 0)),
        out_shape=jax.ShapeDtypeStruct((m, d), F32),
        scratch_shapes=[pltpu.VMEM((tm, d), BF16), pltpu.VMEM((tm, d), F32)],
        compiler_params=_cparams("parallel", "arbitrary"),
        name="mlp",
    )(x, g.reshape(1, d), w_up, w_down, g_final.reshape(1, d))


def _split3(x):
    hi = x.astype(BF16)
    r1 = x - hi.astype(F32)
    mid = r1.astype(BF16)
    lo = (r1 - mid.astype(F32)).astype(BF16)
    return hi, mid, lo


def _hgrn_offdiag_mask(chunk):
    n_blk = chunk // SUBLANES
    seg = np.concatenate([np.full(SUBLANES * i, i) for i in range(1, n_blk)])
    blk = np.repeat(np.arange(1, n_blk), SUBLANES)
    return (blk[:, None] == seg[None, :]).astype(np.float32)


def _hgrn_kernel(*refs, chunk, n_sub):
    n_blk = chunk // SUBLANES
    row_scr = [refs[len(refs) - 3 * (n_sub - u):len(refs) - 3 * (n_sub - u - 1)] for u in range(n_sub)]
    refs = refs[:-3 * n_sub]
    if n_blk > 1:
        q_ref, zf_ref, iv_ref, g_ref, lb_ref, gn_ref, s0_ref, mask_ref, o_ref, sout_ref, s_scr = refs
    else:
        q_ref, zf_ref, iv_ref, g_ref, lb_ref, gn_ref, s0_ref, o_ref, sout_ref, s_scr = refs
    step = pl.program_id(1)

    @pl.when(step == 0)
    def _():
        s_scr[...] = s0_ref[...]

    lb = lb_ref[...]
    gn = gn_ref[...]
    rows = lax.broadcasted_iota(jnp.int32, (chunk, chunk), 0)
    cols = lax.broadcasted_iota(jnp.int32, (chunk, chunk), 1)
    tri = jnp.where(rows >= cols, 1.0, 0.0).astype(BF16)
    hrow = lax.broadcasted_iota(jnp.int32, (A_QK, A_QK), 0) // A_DK
    hcol = lax.broadcasted_iota(jnp.int32, (A_QK, A_QK), 1) // A_DK
    head_ones = jnp.where(hrow == hcol, 1.0, 0.0).astype(BF16)
    sub = lax.broadcasted_iota(jnp.int32, (SUBLANES, A_QK), 0)
    head_lanes = [slice(hh * A_DK, (hh + 1) * A_DK) for hh in range(A_HEADS)]
    nt = (((1,), (1,)), ((), ()))
    tn = (((0,), (0,)), ((), ()))

    def pairwise_phase(cc, b_scr, k_scr, iv_scr):
        r0 = cc * chunk
        q = q_ref[pl.ds(r0, chunk), :].astype(F32)
        zf = zf_ref[pl.ds(r0, chunk), :]
        iv = iv_ref[pl.ds(r0, chunk), :].astype(F32)
        g = g_ref[pl.ds(r0, chunk), :]
        f = lb + (1.0 - lb) * _sigmoid(zf)
        logf = jnp.log(f)
        kk = 1.0 - f
        hi, mid, lo = _split3(logf)
        b = (jnp.dot(tri, hi, preferred_element_type=F32)
             + jnp.dot(tri, mid, preferred_element_type=F32)
             + jnp.dot(tri, lo, preferred_element_type=F32))
        b_scr[...] = b
        k_scr[...] = kk
        iv_scr[...] = iv

        ws = []
        for i in range(n_blk):
            b_i = b[SUBLANES * i:SUBLANES * (i + 1)]
            q_i = q[SUBLANES * i:SUBLANES * (i + 1)]
            for s in range(SUBLANES):
                row = SUBLANES * i + s
                dlt = jnp.where(sub >= s, b_i - b_scr[row:row + 1, :], NEG_BIG)
                ws.append(jnp.exp(dlt) * q_i * k_scr[row:row + 1, :])
        w = jnp.concatenate(ws, axis=0).astype(BF16)
        return r0, q, iv, g, kk, b, w, b_scr, iv_scr

    def matmul_phase(r0, q, iv, g, kk, b, w, b_scr, iv_scr):
        att = jnp.dot(w, head_ones, preferred_element_type=F32)
        o_blocks = []
        for i in range(n_blk):
            acc = jnp.zeros((SUBLANES, A_WIDTH), F32)
            for s in range(SUBLANES):
                row = SUBLANES * i + s
                acc = acc + att[SUBLANES * row:SUBLANES * (row + 1)] * iv_scr[row:row + 1, :]
            o_blocks.append(acc)
        o = jnp.concatenate(o_blocks, axis=0)

        if n_blk > 1:
            qt, kh, ivs = [], [], []
            for i in range(1, n_blk):
                n_s = SUBLANES * i
                r_i = b_scr[n_s - 1:n_s, :]
                qt.append(q[n_s:n_s + SUBLANES] * jnp.exp(b[n_s:n_s + SUBLANES] - r_i))
                kh.append(kk[:n_s] * jnp.exp(r_i - b[:n_s]))
                ivs.append(iv[:n_s])
            qt = jnp.concatenate(qt, axis=0).astype(BF16)
            kh = jnp.concatenate(kh, axis=0).astype(BF16)
            ivs = jnp.concatenate(ivs, axis=0).astype(BF16)
            mask = mask_ref[...]
            o_off = []
            for l in head_lanes:
                a = lax.dot_general(qt[:, l], kh[:, l], nt, preferred_element_type=F32)
                o_off.append(jnp.dot((a * mask).astype(BF16), ivs[:, l], preferred_element_type=F32))
            o_off = jnp.concatenate(o_off, axis=1)
            o = o + jnp.concatenate([jnp.zeros((SUBLANES, A_WIDTH), F32), o_off], axis=0)

        qh = (q * jnp.exp(b)).astype(BF16)
        b_last = b[chunk - 1:chunk, :]
        kd = (kk * jnp.exp(b_last - b)).astype(BF16)
        dec = jnp.exp(b_last)
        ivb = iv.astype(BF16)
        o_inter = []
        for hh, l in enumerate(head_lanes):
            s_t = s_scr[hh]
            o_inter.append(lax.dot_general(qh[:, l], s_t.astype(BF16), nt, preferred_element_type=F32))
            upd = lax.dot_general(ivb[:, l], kd[:, l], tn, preferred_element_type=F32)
            s_scr[hh] = s_t * dec[:, l] + upd
        o = o + jnp.concatenate(o_inter, axis=1)

        o = jnp.concatenate([o[:, l] * _rms_scale(o[:, l]) * gn for l in head_lanes], axis=1)
        o = o * (g * _sigmoid(g))
        o_ref[pl.ds(r0, chunk), :] = o.astype(o_ref.dtype)

    pending = None
    for u in range(n_sub):
        ctx = pairwise_phase(u, *row_scr[u])
        if pending is not None:
            matmul_phase(*pending)
        pending = ctx
    matmul_phase(*pending)

    @pl.when(step == pl.num_programs(1) - 1)
    def _():
        sout_ref[...] = s_scr[...]


def hgrn(q, zf, iv, g, lb, gnorm, s0_t, bsz, t_len):
    chunk = HGRN_CHUNK if t_len % HGRN_CHUNK == 0 else SUBLANES
    step_rows = min(HGRN_STEP_ROWS, t_len)
    n_steps = t_len // step_rows
    tok = pl.BlockSpec((step_rows, A_QK), lambda b, c: (b * n_steps + c, 0))
    s0_b = (lambda b: b) if s0_t.shape[0] == bsz else (lambda b: 0)
    in_specs = [tok, tok, tok, tok,
                pl.BlockSpec((1, A_QK), lambda b, c: (0, 0)),
                pl.BlockSpec((1, A_DV), lambda b, c: (0, 0)),
                pl.BlockSpec((None, A_HEADS, A_DV, A_DK), lambda b, c: (s0_b(b), 0, 0, 0))]
    args = [q, zf, iv, g, lb.reshape(1, A_QK), gnorm.reshape(1, A_DV), s0_t]
    if chunk > SUBLANES:
        mask = jnp.asarray(_hgrn_offdiag_mask(chunk))
        in_specs.append(pl.BlockSpec(mask.shape, lambda b, c: (0, 0)))
        args.append(mask)
    return pl.pallas_call(
        functools.partial(_hgrn_kernel, chunk=chunk, n_sub=step_rows // chunk),
        grid=(bsz, n_steps),
        in_specs=in_specs,
        out_specs=[tok, pl.BlockSpec((None, A_HEADS, A_DV, A_DK), lambda b, c: (b, 0, 0, 0))],
        out_shape=[jax.ShapeDtypeStruct((bsz * t_len, A_WIDTH), BF16),
                   jax.ShapeDtypeStruct((bsz, A_HEADS, A_DV, A_DK), F32)],
        scratch_shapes=([pltpu.VMEM((A_HEADS, A_DV, A_DK), F32)]
                        + [pltpu.VMEM((chunk, A_QK), F32)] * (3 * (step_rows // chunk))),
        compiler_params=_cparams("parallel", "arbitrary"),
        name="hgrn",
    )(*args)


def s5_weights(a_re, a_im, log_dt, b_re, b_im, c_re, c_im, d_skip):
    f32 = F32
    ar = a_re.astype(f32)
    ai = a_im.astype(f32)
    dt = jnp.exp(log_dt.astype(f32))[:, None]
    mag = jnp.exp(dt * ar)
    abar_re = mag * jnp.cos(dt * ai)
    abar_im = mag * jnp.sin(dt * ai)
    den = ar * ar + ai * ai
    zr = ((abar_re - 1.0) * ar + abar_im * ai) / den
    zi = (abar_im * ar - (abar_re - 1.0) * ai) / den
    br, bi = b_re.astype(f32), b_im.astype(f32)
    bb_re = zr[..., None] * br - zi[..., None] * bi
    bb_im = zr[..., None] * bi + zi[..., None] * br
    eye = jnp.eye(S5_GROUPS, dtype=f32)

    def in_proj(bb):
        return jnp.einsum('gnp,gh->gphn', bb, eye).reshape(B_WIDTH, S5_NSTATE)

    def out_proj(c):
        return jnp.einsum('gpn,gh->gnhp', c, eye).reshape(S5_NSTATE, B_WIDTH)

    n_sb = B_WIDTH // LANES
    sw = S5_NSTATE // n_sb

    def diag_blocks(full, rows, cols):
        return jnp.stack([full[c * rows:(c + 1) * rows, c * cols:(c + 1) * cols] for c in range(n_sb)])

    b_blk = jnp.concatenate([diag_blocks(in_proj(bb_re), LANES, sw), diag_blocks(in_proj(bb_im), LANES, sw)],
                            axis=2).astype(BF16)
    c_blk = jnp.concatenate([diag_blocks(out_proj(c_re.astype(f32)), sw, LANES),
                             -diag_blocks(out_proj(c_im.astype(f32)), sw, LANES)],
                            axis=1).astype(BF16)
    return dict(b_blk=b_blk, c_blk=c_blk, a_re=abar_re.reshape(1, S5_NSTATE),
                a_im=abar_im.reshape(1, S5_NSTATE), d=d_skip.astype(f32).reshape(1, B_WIDTH))


def _s5_kernel(u_ref, x0re_ref, x0im_ref, bblk_ref, cblk_ref, are_ref, aim_ref, d_ref,
               y_ref, fre_ref, fim_ref, u_scr, bu_scr, xre_scr, xim_scr, *, tt, nb):
    step = pl.program_id(0)

    @pl.when(step == 0)
    def _():
        xre_scr[...] = x0re_ref[...]
        xim_scr[...] = x0im_ref[...]

    n_lt = B_WIDTH // LANES
    u_bt = u_ref[...].reshape(nb * tt, B_WIDTH).astype(F32)
    for c in range(n_lt):
        u_scr[c] = u_bt[:, c * LANES:(c + 1) * LANES]
    sw = S5_NSTATE // n_lt
    for c in range(n_lt):
        st_l = slice(c * sw, (c + 1) * sw)
        ch_l = slice(c * LANES, (c + 1) * LANES)
        u_tb = jnp.concatenate([u_scr[c, pl.ds(t, nb, stride=tt), :] for t in range(tt)], axis=0)
        bu_scr[c] = jnp.dot(u_tb.astype(BF16), bblk_ref[c], preferred_element_type=F32)
        a_re = jnp.broadcast_to(are_ref[:, st_l], (nb, sw))
        a_im = jnp.broadcast_to(aim_ref[:, st_l], (nb, sw))

        def body(t, st, c=c, a_re=a_re, a_im=a_im):
            x_re, x_im = st
            r0 = pl.multiple_of(t * nb, nb)
            n_re = a_re * x_re - a_im * x_im + bu_scr[c, pl.ds(r0, nb), :sw]
            n_im = a_re * x_im + a_im * x_re + bu_scr[c, pl.ds(r0, nb), sw:]
            bu_scr[c, pl.ds(r0, nb), :sw] = n_re
            bu_scr[c, pl.ds(r0, nb), sw:] = n_im
            return n_re, n_im

        x_re, x_im = lax.fori_loop(0, tt, body, (xre_scr[:, st_l], xim_scr[:, st_l]))
        xre_scr[:, st_l] = x_re
        xim_scr[:, st_l] = x_im

        y = (jnp.dot(bu_scr[c].astype(BF16), cblk_ref[c], preferred_element_type=F32)
             + d_ref[:, ch_l] * u_tb)
        y = 0.5 * y * (1.0 + lax.erf(y * (1.0 / math.sqrt(2.0))))
        for t in range(tt):
            u_scr[c, pl.ds(t, nb, stride=tt), :] = y[nb * t:nb * (t + 1)]
    y_bt = jnp.concatenate([u_scr[c] for c in range(n_lt)], axis=1).reshape(nb, tt, B_WIDTH)
    y_ref[...] = y_bt.astype(y_ref.dtype)

    @pl.when(step == pl.num_programs(0) - 1)
    def _():
        fre_ref[...] = xre_scr[...]
        fim_ref[...] = xim_scr[...]


def s5(u, wts, x0_re, x0_im, bsz, t_len):
    nb = SUBLANES
    tt = min(S5_STEP_T, t_len)
    u3 = u.reshape(bsz, t_len, B_WIDTH)
    x0 = [x.astype(F32).reshape(bsz, S5_NSTATE) for x in (x0_re, x0_im)]
    if bsz < nb:
        u3 = jnp.pad(u3, ((0, nb - bsz), (0, 0), (0, 0)))
        x0 = [jnp.pad(x, ((0, nb - bsz), (0, 0))) for x in x0]
    fixed = lambda i: (0, 0)
    fixed3 = lambda i: (0, 0, 0)
    n_sb = B_WIDTH // LANES
    sw2 = 2 * S5_NSTATE // n_sb
    tok = pl.BlockSpec((nb, tt, B_WIDTH), lambda i: (0, i, 0))
    state = pl.BlockSpec((nb, S5_NSTATE), fixed)
    yg, fre, fim = pl.pallas_call(
        functools.partial(_s5_kernel, tt=tt, nb=nb),
        grid=(t_len // tt,),
        in_specs=[tok, state, state,
                  pl.BlockSpec((n_sb, LANES, sw2), fixed3), pl.BlockSpec((n_sb, sw2, LANES), fixed3),
                  pl.BlockSpec((1, S5_NSTATE), fixed), pl.BlockSpec((1, S5_NSTATE), fixed),
                  pl.BlockSpec((1, B_WIDTH), fixed)],
        out_specs=[tok, state, state],
        out_shape=[jax.ShapeDtypeStruct((nb, t_len, B_WIDTH), u.dtype),
                   jax.ShapeDtypeStruct((nb, S5_NSTATE), F32), jax.ShapeDtypeStruct((nb, S5_NSTATE), F32)],
        scratch_shapes=[pltpu.VMEM((n_sb, nb * tt, LANES), F32),
                        pltpu.VMEM((n_sb, nb * tt, sw2), F32),
                        pltpu.VMEM((nb, S5_NSTATE), F32), pltpu.VMEM((nb, S5_NSTATE), F32)],
        compiler_params=_cparams("arbitrary"),
        name="s5",
    )(u3, x0[0], x0[1], wts['b_blk'], wts['c_blk'], wts['a_re'], wts['a_im'], wts['d'])
    yg = yg[:bsz].reshape(bsz * t_len, B_WIDTH)
    return (yg, fre[:bsz].reshape(bsz, S5_GROUPS, S5_STATE), fim[:bsz].reshape(bsz, S5_GROUPS, S5_STATE))


def _sb_attn_kernel(*refs, heads, n_past, p_valid, scale):
    if n_past:
        q_ref, k_ref, v_ref, pk_ref, pv_ref, o_ref, carry_scr, acc_scr = refs
    else:
        q_ref, k_ref, v_ref, o_ref, carry_scr, acc_scr = refs
    blk = ATT_BLOCK
    i = pl.program_id(1)

    def suffix_matrix(tk):
        srow = lax.broadcasted_iota(jnp.int32, (tk, tk), 0)
        scol = lax.broadcasted_iota(jnp.int32, (tk, tk), 1)
        return jnp.concatenate([jnp.where(srow > scol, 1.0, 0.0), jnp.ones((tk, blk), F32)],
                               axis=1).astype(BF16)

    suffix = {tk: suffix_matrix(tk) for tk in (blk, ATT_SLAB)}
    rows = lax.broadcasted_iota(jnp.int32, (blk, blk), 0)
    cols = lax.broadcasted_iota(jnp.int32, (blk, blk), 1)

    carry_scr[...] = jnp.zeros_like(carry_scr)
    acc_scr[...] = jnp.zeros_like(acc_scr)
    head_lanes = [slice(hh * C_HD, (hh + 1) * C_HD) for hh in range(heads)]
    n_rows = heads * blk

    def per_head(x, fn):
        return jnp.concatenate([fn(x[hh * blk:(hh + 1) * blk]) for hh in range(heads)], axis=0)

    def visit(kr, vr, rj, tk, mask):
        z = jnp.concatenate(
            [lax.dot_general(q_ref[:, l], kr[pl.ds(rj, tk), l], (((1,), (1,)), ((), ())),
                             preferred_element_type=F32) for l in head_lanes], axis=0) * scale
        log_b = jnp.minimum(z, 0.0) - jnp.log(1.0 + jnp.exp(-jnp.abs(z)))
        x = log_b - z
        if mask is not None:
            x = per_head(x, lambda xh: jnp.where(mask, xh, 0.0))
        hi = x.astype(BF16)
        lo = (x - hi.astype(F32)).astype(BF16)
        cs = jnp.dot(jnp.concatenate([hi, lo], axis=0), suffix[tk], preferred_element_type=F32)
        cs = cs[:n_rows] + cs[n_rows:]
        carry = carry_scr[...]
        w = jnp.exp(log_b + cs[:, :tk] + jnp.concatenate([carry] * (tk // blk), axis=1))
        if mask is not None:
            w = per_head(w, lambda wh: jnp.where(mask, wh, 0.0))
        w = w.astype(BF16)
        acc_scr[...] += jnp.concatenate(
            [jnp.dot(w[hh * blk:(hh + 1) * blk], vr[pl.ds(rj, tk), l], preferred_element_type=F32)
             for hh, l in enumerate(head_lanes)], axis=0)
        carry = carry + cs[:, tk:]
        carry_scr[...] = carry
        return jnp.max(carry)

    live = visit(k_ref, v_ref, pl.multiple_of(i * blk, blk), blk, cols < rows)

    def slab_body(st):
        j, _ = st
        return j - 2, visit(k_ref, v_ref, pl.multiple_of((j - 1) * blk, blk), ATT_SLAB, None)

    def block_body(st):
        j, _ = st
        return j - 1, visit(k_ref, v_ref, pl.multiple_of(j * blk, blk), blk, None)

    def alive_from(first):
        return lambda st: jnp.logical_and(st[0] >= first, st[1] > ATT_DEAD_LOG_WEIGHT)

    j, live = lax.while_loop(alive_from(1), slab_body, (i - 1, live))
    _, live = lax.while_loop(alive_from(0), block_body, (j, live))

    if n_past:
        def past_body(st):
            j, _ = st
            return j - 1, visit(pk_ref, pv_ref, pl.multiple_of(j * blk, blk), blk, (cols + j * blk) < p_valid)

        lax.while_loop(alive_from(0), past_body, (jnp.int32(n_past - 1), live))

    for hh in range(heads):
        o_ref[:, head_lanes[hh]] = acc_scr[hh * blk:(hh + 1) * blk, :].astype(o_ref.dtype)


def sb_attention(q, k, v, past_k, past_v, p_valid):
    bsz, t_len, width = q.shape
    n_past = 0 if past_k is None else past_k.shape[1] // ATT_BLOCK
    q_spec = pl.BlockSpec((None, ATT_BLOCK, width), lambda b, i: (b, i, 0))
    kv_spec = pl.BlockSpec((None, t_len, width), lambda b, i: (b, 0, 0))
    in_specs = [q_spec, kv_spec, kv_spec]
    args = [q, k, v]
    if n_past:
        pb = (lambda b: b) if past_k.shape[0] == bsz else (lambda b: 0)
        p_spec = pl.BlockSpec((None, past_k.shape[1], width), lambda b, i: (pb(b), 0, 0),
                              pipeline_mode=pl.Buffered(1))
        in_specs += [p_spec, p_spec]
        args += [past_k, past_v]
    return pl.pallas_call(
        functools.partial(_sb_attn_kernel, heads=C_HEADS, n_past=n_past, p_valid=p_valid,
                          scale=C_HD ** -0.5),
        grid=(bsz, t_len // ATT_BLOCK),
        in_specs=in_specs,
        out_specs=q_spec,
        out_shape=jax.ShapeDtypeStruct((bsz, t_len, width), BF16),
        scratch_shapes=[pltpu.VMEM((C_HEADS * ATT_BLOCK, C_HD), F32)] * 2,
        compiler_params=_cparams("parallel", "arbitrary"),
        name="sb_attention",
    )(*args)


def _pad_rows(x, mult):
    t = x.shape[1]
    tp = -(-t // mult) * mult
    return x if tp == t else jnp.pad(x, ((0, 0), (0, tp - t), (0, 0)))


def _run_stream(x, bsz, t_len, wts, hgrn_s0_t, ssm0_re, ssm0_im, past_k, past_v, p_valid, lead_kv=None):
    act = BF16 if t_len % HGRN_CHUNK == 0 else F32
    q, zf, iv, g, u = norm_matmul(
        x, wts['ln_mix'][0], wts['w_in_even'],
        [(0, A_QK, act), (A_QK, A_QK, F32), (2 * A_QK, A_WIDTH, act),
         (2 * A_QK + A_WIDTH, A_WIDTH, F32), (2 * A_QK + 2 * A_WIDTH, B_WIDTH, act)])
    o_a, s_t = hgrn(q, zf, iv, g, wts['lb'], wts['hgrn_norm'], hgrn_s0_t, bsz, t_len)
    yg, x_re, x_im = s5(u, wts['s5'], ssm0_re, ssm0_im, bsz, t_len)
    x = even_out(x, o_a, yg, wts['w_glu'], wts['w_out_even_a'], wts['w_out_even_b'])
    x = mlp(x, wts['ln_mlp'][0], wts['w_up'][0], wts['w_down'][0], wts['ln_final'], final_norm=False)
    if lead_kv is None:
        qb, kb, vb, k32, v32 = norm_matmul(
            x, wts['ln_mix'][1], wts['w_in_odd'],
            [(0, D_MODEL, BF16), (D_MODEL, D_MODEL, BF16), (2 * D_MODEL, D_MODEL, BF16),
             (D_MODEL, D_MODEL, F32), (2 * D_MODEL, D_MODEL, F32)])
    else:
        qb, kb, vb, k32, v32 = qkv_shifted(x, wts['ln_mix'][1], wts['w_in_odd'], *lead_kv, bsz, t_len)
    shp = (bsz, t_len, D_MODEL)
    o = sb_attention(_pad_rows(qb.reshape(shp), ATT_BLOCK), _pad_rows(kb.reshape(shp), ATT_BLOCK),
                     _pad_rows(vb.reshape(shp), ATT_BLOCK), past_k, past_v, p_valid)
    o = o[:, :t_len].reshape(bsz * t_len, D_MODEL)
    x = odd_out(x, o, wts['w_out_odd'])
    y = mlp(x, wts['ln_mlp'][1], wts['w_up'][1], wts['w_down'][1], wts['ln_final'], final_norm=True)
    return y, s_t, x_re, x_im, k32, v32


def kernel(x_prompt, x_sample, state_hgrn, state_ssm_re, state_ssm_im, cache_k, cache_v, meta_tokens,
           ln_mix, ln_mlp, ln_final, w_in_even, hgrn_lb, hgrn_norm, ssm_a_re, ssm_a_im, ssm_log_dt,
           ssm_b_re, ssm_b_im, ssm_c_re, ssm_c_im, ssm_d, w_glu, w_out_even, w_in_odd, w_out_odd,
           w_up, w_down):
    bsz, seq, _ = x_prompt.shape
    dbsz, dseq, _ = x_sample.shape
    past = cache_k.shape[2]
    lb_all = jnp.cumsum(jax.nn.softmax(hgrn_lb.astype(F32), axis=0), axis=0)
    w_out_e = w_out_even[0].astype(BF16)
    wts = dict(
        ln_mix=ln_mix.astype(F32), ln_mlp=ln_mlp.astype(F32), ln_final=ln_final.astype(F32),
        w_in_even=w_in_even[0].astype(BF16), lb=lb_all[0], hgrn_norm=hgrn_norm[0].astype(F32),
        s5=s5_weights(ssm_a_re[0], ssm_a_im[0], ssm_log_dt[0], ssm_b_re[0], ssm_b_im[0],
                      ssm_c_re[0], ssm_c_im[0], ssm_d[0]),
        w_glu=w_glu[0].astype(BF16), w_out_even_a=w_out_e[:A_WIDTH], w_out_even_b=w_out_e[A_WIDTH:],
        w_in_odd=w_in_odd[0].astype(BF16), w_out_odd=w_out_odd[0].astype(BF16),
        w_up=w_up.astype(BF16), w_down=w_down.astype(BF16))

    zeros_s = jnp.zeros((1, A_HEADS, A_DV, A_DK), F32)
    zeros_x = jnp.zeros((1, S5_GROUPS, S5_STATE), F32)
    _, m_s, m_re, m_im, m_k, m_v = _run_stream(
        meta_tokens.astype(F32), 1, N_META, wts, zeros_s, zeros_x, zeros_x, None, None, 0)

    m_kp = _pad_rows(m_k.astype(BF16)[None], ATT_BLOCK)
    m_vp = _pad_rows(m_v.astype(BF16)[None], ATT_BLOCK)
    y_p, s_p, re_p, im_p, k_p, v_p = _run_stream(
        x_prompt.reshape(bsz * seq, D_MODEL), bsz, seq, wts, m_s,
        jnp.broadcast_to(m_re, (bsz, S5_GROUPS, S5_STATE)), jnp.broadcast_to(m_im, (bsz, S5_GROUPS, S5_STATE)),
        m_kp, m_vp, N_META, lead_kv=(m_k, m_v))

    ck = cache_k[0].reshape(dbsz, past, D_MODEL).astype(BF16)
    cv = cache_v[0].reshape(dbsz, past, D_MODEL).astype(BF16)
    y_s, s_s, re_s, im_s, k_s, v_s = _run_stream(
        x_sample.reshape(dbsz * dseq, D_MODEL), dbsz, dseq, wts,
        jnp.swapaxes(state_hgrn[0].astype(F32), -1, -2), state_ssm_re[0], state_ssm_im[0],
        _pad_rows(ck, ATT_BLOCK), _pad_rows(cv, ATT_BLOCK), past)

    kv_shape = (1, bsz, N_META + seq, C_HEADS, C_HD)
    return (y_p.reshape(bsz, seq, D_MODEL), y_s.reshape(dbsz, dseq, D_MODEL),
            jnp.swapaxes(s_p, -1, -2)[None], jnp.swapaxes(s_s, -1, -2)[None],
            re_p[None], im_p[None], re_s[None], im_s[None],
            k_p.reshape(kv_shape), v_p.reshape(kv_shape),
            k_s.reshape(1, dbsz, dseq, C_HEADS, C_HD), v_s.reshape(1, dbsz, dseq, C_HEADS, C_HD))
```

```python
import functools
import math

import numpy as np
import jax
import jax.numpy as jnp
from jax import lax
from jax.experimental import pallas as pl
from jax.experimental.pallas import tpu as pltpu

F32 = jnp.float32
BF16 = jnp.bfloat16

D_MODEL = 1024
N_META = 16
A_HEADS = 4
A_DK = 128
A_DV = 128
A_WIDTH = 512
A_QK = A_HEADS * A_DK
B_WIDTH = 512
S5_GROUP = 16
S5_GROUPS = 32
S5_STATE = 64
S5_NSTATE = S5_GROUPS * S5_STATE
C_HEADS = 8
C_HD = 128
D_FF = 4 * D_MODEL
EPS = 1e-6

SUBLANES = 8
LANES = 128
VMEM_LIMIT_BYTES = 48 * 1024 * 1024

HGRN_CHUNK = 64
HGRN_STEP_ROWS = 256
S5_STEP_T = 64
ATT_BLOCK = 128
ATT_SLAB = 256
ATT_DEAD_LOG_WEIGHT = -104.0
NEG_BIG = -1e30


def _cparams(*sem):
    return pltpu.CompilerParams(dimension_semantics=sem, vmem_limit_bytes=VMEM_LIMIT_BYTES)


def _sigmoid(x):
    return 1.0 / (1.0 + jnp.exp(-x))


def _rms_scale(x):
    return lax.rsqrt(jnp.mean(x * x, axis=-1, keepdims=True) + EPS)


def _cast_kernel(x_ref, o_ref):
    o_ref[...] = x_ref[...].astype(o_ref.dtype)


def cast_bf16(w, *, block_elems=2 * 1024 * 1024):
    w2 = w.reshape(-1, w.shape[-1])
    r, c = w2.shape
    rows = min(block_elems // c, r)
    out = pl.pallas_call(
        _cast_kernel,
        grid=(pl.cdiv(r, rows),),
        in_specs=[pl.BlockSpec((rows, c), lambda i: (i, 0))],
        out_specs=pl.BlockSpec((rows, c), lambda i: (i, 0)),
        out_shape=jax.ShapeDtypeStruct((r, c), BF16),
        compiler_params=_cparams("parallel"),
        name="cast_bf16",
    )(w2)
    return out.reshape(w.shape)


def _norm_matmul_kernel(x_ref, g_ref, w_ref, *out_refs, cols, nchunk):
    x = x_ref[...]
    h = (x * _rms_scale(x) * g_ref[...]).astype(BF16)
    for c0, width in sorted(set(cols)):
        for c in range(0, width, nchunk):
            res = jnp.dot(h, w_ref[:, c0 + c:c0 + c + nchunk], preferred_element_type=F32)
            for o_ref, col in zip(out_refs, cols):
                if col == (c0, width):
                    o_ref[:, c:c + nchunk] = res.astype(o_ref.dtype)


def norm_matmul(x, g, w_bf16, outs, *, tm=512, nchunk=512):
    m, d = x.shape
    tm = min(tm, m)
    kern = functools.partial(_norm_matmul_kernel, cols=[(c0, wd) for c0, wd, _ in outs], nchunk=nchunk)
    return pl.pallas_call(
        kern,
        grid=(pl.cdiv(m, tm),),
        in_specs=[pl.BlockSpec((tm, d), lambda i: (i, 0)),
                  pl.BlockSpec((1, d), lambda i: (0, 0)),
                  pl.BlockSpec(w_bf16.shape, lambda i: (0, 0))],
        out_specs=[pl.BlockSpec((tm, wd), lambda i: (i, 0)) for _, wd, _ in outs],
        out_shape=[jax.ShapeDtypeStruct((m, wd), dt) for _, wd, dt in outs],
        compiler_params=_cparams("parallel"),
        name="norm_matmul",
    )(x, g.reshape(1, d), w_bf16)


def _qkv_shifted_kernel(x_ref, g_ref, w_ref, lead_ref, qb_ref, kb_ref, vb_ref, kf_ref, vf_ref, carry_scr,
                        *, n_lead, nchunk):
    t = pl.program_id(1)
    nt = pl.num_programs(1) - 1
    tm, width = qb_ref.shape

    @pl.when(t == 0)
    def _():
        carry_scr[...] = lead_ref[...]

    for n, f_ref in enumerate((kf_ref, vf_ref)):
        f_ref[:n_lead, :] = carry_scr[n]

    @pl.when(t < nt)
    def _():
        x = x_ref[...]
        h = (x * _rms_scale(x) * g_ref[...]).astype(BF16)
        for n, (b_ref, f_ref) in enumerate(((qb_ref, None), (kb_ref, kf_ref), (vb_ref, vf_ref))):
            for c in range(0, width, nchunk):
                res = jnp.dot(h, w_ref[:, n * width + c:n * width + c + nchunk], preferred_element_type=F32)
                b_ref[:, c:c + nchunk] = res.astype(b_ref.dtype)
                if f_ref is not None:
                    f_ref[n_lead:, c:c + nchunk] = res[:tm - n_lead]
                    carry_scr[n - 1, :, c:c + nchunk] = res[tm - n_lead:]


def qkv_shifted(x, g, w_bf16, lead_k, lead_v, bsz, t_len, *, tm=512, nchunk=512):
    m, d = x.shape
    width = w_bf16.shape[1] // 3
    n_lead = lead_k.shape[0]
    tm = math.gcd(tm, t_len)
    nt = t_len // tm
    tok = lambda b, t: (b * nt + jnp.minimum(t, nt - 1), 0)
    return pl.pallas_call(
        functools.partial(_qkv_shifted_kernel, n_lead=n_lead, nchunk=nchunk),
        grid=(bsz, nt + 1),
        in_specs=[pl.BlockSpec((tm, d), tok),
                  pl.BlockSpec((1, d), lambda b, t: (0, 0)),
                  pl.BlockSpec(w_bf16.shape, lambda b, t: (0, 0)),
                  pl.BlockSpec((2, n_lead, width), lambda b, t: (0, 0, 0))],
        out_specs=[pl.BlockSpec((tm, width), tok)] * 3
                  + [pl.BlockSpec((None, tm, width), lambda b, t: (b, t, 0))] * 2,
        out_shape=[jax.ShapeDtypeStruct((m, width), BF16)] * 3
                  + [jax.ShapeDtypeStruct((bsz, n_lead + t_len, width), F32)] * 2,
        scratch_shapes=[pltpu.VMEM((2, n_lead, width), F32)],
        compiler_params=_cparams("parallel", "arbitrary"),
        name="qkv_shifted",
    )(x, g.reshape(1, d), w_bf16, jnp.stack([lead_k, lead_v]))


def _even_out_kernel(x_ref, oa_ref, yg_ref, wglu_ref, wa_ref, wb_ref, o_ref):
    yg = yg_ref[...]
    gate = _sigmoid(jnp.dot(yg.astype(BF16), wglu_ref[...], preferred_element_type=F32))
    ob = (yg.astype(F32) * gate).astype(BF16)
    o_ref[...] = (x_ref[...]
                  + jnp.dot(oa_ref[...], wa_ref[...], preferred_element_type=F32)
                  + jnp.dot(ob, wb_ref[...], preferred_element_type=F32))


def even_out(x, o_a, yg, w_glu, w_out_a, w_out_b, *, tm=512):
    m, d = x.shape
    tm = min(tm, m)
    row = lambda i: (i, 0)
    fixed = lambda i: (0, 0)
    return pl.pallas_call(
        _even_out_kernel,
        grid=(pl.cdiv(m, tm),),
        in_specs=[pl.BlockSpec((tm, d), row), pl.BlockSpec((tm, A_WIDTH), row),
                  pl.BlockSpec((tm, B_WIDTH), row), pl.BlockSpec(w_glu.shape, fixed),
                  pl.BlockSpec(w_out_a.shape, fixed), pl.BlockSpec(w_out_b.shape, fixed)],
        out_specs=pl.BlockSpec((tm, d), row),
        out_shape=jax.ShapeDtypeStruct((m, d), F32),
        compiler_params=_cparams("parallel"),
        name="even_out",
    )(x, o_a, yg, w_glu, w_out_a, w_out_b)


def _odd_out_kernel(x_ref, o_ref_in, w_ref, o_ref):
    o_ref[...] = x_ref[...] + jnp.dot(o_ref_in[...], w_ref[...], preferred_element_type=F32)


def odd_out(x, o, w_out, *, tm=512):
    m, d = x.shape
    tm = min(tm, m)
    row = lambda i: (i, 0)
    return pl.pallas_call(
        _odd_out_kernel,
        grid=(pl.cdiv(m, tm),),
        in_specs=[pl.BlockSpec((tm, d), row), pl.BlockSpec((tm, o.shape[1]), row),
                  pl.BlockSpec(w_out.shape, lambda i: (0, 0))],
        out_specs=pl.BlockSpec((tm, d), row),
        out_shape=jax.ShapeDtypeStruct((m, d), F32),
        compiler_params=_cparams("parallel"),
        name="odd_out",
    )(x, o, w_out)


def _mlp_kernel(x_ref, g_ref, wup_ref, wdn_ref, gf_ref, o_ref, h_scr, acc_scr, *, final_norm):
    j = pl.program_id(1)

    @pl.when(j == 0)
    def _():
        x = x_ref[...]
        h_scr[...] = (x * _rms_scale(x) * g_ref[...]).astype(BF16)
        acc_scr[...] = jnp.zeros_like(acc_scr)

    a = jnp.dot(h_scr[...], wup_ref[...], preferred_element_type=F32)
    a = jnp.square(jnp.maximum(a, 0.0)).astype(BF16)
    acc_scr[...] += jnp.dot(a, wdn_ref[...], preferred_element_type=F32)

    @pl.when(j == pl.num_programs(1) - 1)
    def _():
        y = x_ref[...] + acc_scr[...]
        if final_norm:
            y = y * _rms_scale(y) * gf_ref[...]
        o_ref[...] = y


def mlp(x, g, w_up, w_down, g_final, *, final_norm, tm=1024, tf=1024):
    m, d = x.shape
    tm = min(tm, m)
    ff = w_up.shape[1]
    return pl.pallas_call(
        functools.partial(_mlp_kernel, final_norm=final_norm),
        grid=(pl.cdiv(m, tm), ff // tf),
        in_specs=[pl.BlockSpec((tm, d), lambda i, j: (i, 0)),
                  pl.BlockSpec((1, d), lambda i, j: (0, 0)),
                  pl.BlockSpec((d, tf), lambda i, j: (0, j)),
                  pl.BlockSpec((tf, d), lambda i, j: (j, 0)),
                  pl.BlockSpec((1, d), lambda i, j: (0, 0))],
        out_specs=pl.BlockSpec((tm, d), lambda i, j: (i, 0)),
        out_shape=jax.ShapeDtypeStruct((m, d), F32),
        scratch_shapes=[pltpu.VMEM((tm, d), BF16), pltpu.VMEM((tm, d), F32)],
        compiler_params=_cparams("parallel", "arbitrary"),
        name="mlp",
    )(x, g.reshape(1, d), w_up, w_down, g_final.reshape(1, d))


def _split3(x):
    hi = x.astype(BF16)
    r1 = x - hi.astype(F32)
    mid = r1.astype(BF16)
    lo = (r1 - mid.astype(F32)).astype(BF16)
    return hi, mid, lo


def _hgrn_offdiag_mask(chunk):
    n_blk = chunk // SUBLANES
    seg = np.concatenate([np.full(SUBLANES * i, i) for i in range(1, n_blk)])
    blk = np.repeat(np.arange(1, n_blk), SUBLANES)
    return (blk[:, None] == seg[None, :]).astype(np.float32)


def _hgrn_kernel(*refs, chunk, n_sub):
    n_blk = chunk // SUBLANES
    row_scr = [refs[len(refs) - 3 * (n_sub - u):len(refs) - 3 * (n_sub - u - 1)] for u in range(n_sub)]
    refs = refs[:-3 * n_sub]
    if n_blk > 1:
        q_ref, zf_ref, iv_ref, g_ref, lb_ref, gn_ref, s0_ref, mask_ref, o_ref, sout_ref, s_scr = refs
    else:
        q_ref, zf_ref, iv_ref, g_ref, lb_ref, gn_ref, s0_ref, o_ref, sout_ref, s_scr = refs
    step = pl.program_id(1)

    @pl.when(step == 0)
    def _():
        s_scr[...] = s0_ref[...]

    lb = lb_ref[...]
    gn = gn_ref[...]
    rows = lax.broadcasted_iota(jnp.int32, (chunk, chunk), 0)
    cols = lax.broadcasted_iota(jnp.int32, (chunk, chunk), 1)
    tri = jnp.where(rows >= cols, 1.0, 0.0).astype(BF16)
    hrow = lax.broadcasted_iota(jnp.int32, (A_QK, A_QK), 0) // A_DK
    hcol = lax.broadcasted_iota(jnp.int32, (A_QK, A_QK), 1) // A_DK
    head_ones = jnp.where(hrow == hcol, 1.0, 0.0).astype(BF16)
    sub = lax.broadcasted_iota(jnp.int32, (SUBLANES, A_QK), 0)
    head_lanes = [slice(hh * A_DK, (hh + 1) * A_DK) for hh in range(A_HEADS)]
    nt = (((1,), (1,)), ((), ()))
    tn = (((0,), (0,)), ((), ()))

    def pairwise_phase(cc, b_scr, k_scr, iv_scr):
        r0 = cc * chunk
        q = q_ref[pl.ds(r0, chunk), :].astype(F32)
        zf = zf_ref[pl.ds(r0, chunk), :]
        iv = iv_ref[pl.ds(r0, chunk), :].astype(F32)
        g = g_ref[pl.ds(r0, chunk), :]
        f = lb + (1.0 - lb) * _sigmoid(zf)
        logf = jnp.log(f)
        kk = 1.0 - f
        hi, mid, lo = _split3(logf)
        b = (jnp.dot(tri, hi, preferred_element_type=F32)
             + jnp.dot(tri, mid, preferred_element_type=F32)
             + jnp.dot(tri, lo, preferred_element_type=F32))
        b_scr[...] = b
        k_scr[...] = kk
        iv_scr[...] = iv

        ws = []
        for i in range(n_blk):
            b_i = b[SUBLANES * i:SUBLANES * (i + 1)]
            q_i = q[SUBLANES * i:SUBLANES * (i + 1)]
            for s in range(SUBLANES):
                row = SUBLANES * i + s
                dlt = jnp.where(sub >= s, b_i - b_scr[row:row + 1, :], NEG_BIG)
                ws.append(jnp.exp(dlt) * q_i * k_scr[row:row + 1, :])
        w = jnp.concatenate(ws, axis=0).astype(BF16)
        return r0, q, iv, g, kk, b, w, b_scr, iv_scr

    def matmul_phase(r0, q, iv, g, kk, b, w, b_scr, iv_scr):
        att = jnp.dot(w, head_ones, preferred_element_type=F32)
        o_blocks = []
        for i in range(n_blk):
            acc = jnp.zeros((SUBLANES, A_WIDTH), F32)
            for s in range(SUBLANES):
                row = SUBLANES * i + s
                acc = acc + att[SUBLANES * row:SUBLANES * (row + 1)] * iv_scr[row:row + 1, :]
            o_blocks.append(acc)
        o = jnp.concatenate(o_blocks, axis=0)

        if n_blk > 1:
            qt, kh, ivs = [], [], []
            for i in range(1, n_blk):
                n_s = SUBLANES * i
                r_i = b_scr[n_s - 1:n_s, :]
                qt.append(q[n_s:n_s + SUBLANES] * jnp.exp(b[n_s:n_s + SUBLANES] - r_i))
                kh.append(kk[:n_s] * jnp.exp(r_i - b[:n_s]))
                ivs.append(iv[:n_s])
            qt = jnp.concatenate(qt, axis=0).astype(BF16)
            kh = jnp.concatenate(kh, axis=0).astype(BF16)
            ivs = jnp.concatenate(ivs, axis=0).astype(BF16)
            mask = mask_ref[...]
            o_off = []
            for l in head_lanes:
                a = lax.dot_general(qt[:, l], kh[:, l], nt, preferred_element_type=F32)
                o_off.append(jnp.dot((a * mask).astype(BF16), ivs[:, l], preferred_element_type=F32))
            o_off = jnp.concatenate(o_off, axis=1)
            o = o + jnp.concatenate([jnp.zeros((SUBLANES, A_WIDTH), F32), o_off], axis=0)

        qh = (q * jnp.exp(b)).astype(BF16)
        b_last = b[chunk - 1:chunk, :]
        kd = (kk * jnp.exp(b_last - b)).astype(BF16)
        dec = jnp.exp(b_last)
        ivb = iv.astype(BF16)
        o_inter = []
        for hh, l in enumerate(head_lanes):
            s_t = s_scr[hh]
            o_inter.append(lax.dot_general(qh[:, l], s_t.astype(BF16), nt, preferred_element_type=F32))
            upd = lax.dot_general(ivb[:, l], kd[:, l], tn, preferred_element_type=F32)
            s_scr[hh] = s_t * dec[:, l] + upd
        o = o + jnp.concatenate(o_inter, axis=1)

        o = jnp.concatenate([o[:, l] * _rms_scale(o[:, l]) * gn for l in head_lanes], axis=1)
        o = o * (g * _sigmoid(g))
        o_ref[pl.ds(r0, chunk), :] = o.astype(o_ref.dtype)

    pending = None
    for u in range(n_sub):
        ctx = pairwise_phase(u, *row_scr[u])
        if pending is not None:
            matmul_phase(*pending)
        pending = ctx
    matmul_phase(*pending)

    @pl.when(step == pl.num_programs(1) - 1)
    def _():
        sout_ref[...] = s_scr[...]


def hgrn(q, zf, iv, g, lb, gnorm, s0_t, bsz, t_len):
    chunk = HGRN_CHUNK if t_len % HGRN_CHUNK == 0 else SUBLANES
    step_rows = min(HGRN_STEP_ROWS, t_len)
    n_steps = t_len // step_rows
    tok = pl.BlockSpec((step_rows, A_QK), lambda b, c: (b * n_steps + c, 0))
    s0_b = (lambda b: b) if s0_t.shape[0] == bsz else (lambda b: 0)
    in_specs = [tok, tok, tok, tok,
                pl.BlockSpec((1, A_QK), lambda b, c: (0, 0)),
                pl.BlockSpec((1, A_DV), lambda b, c: (0, 0)),
                pl.BlockSpec((None, A_HEADS, A_DV, A_DK), lambda b, c: (s0_b(b), 0, 0, 0))]
    args = [q, zf, iv, g, lb.reshape(1, A_QK), gnorm.reshape(1, A_DV), s0_t]
    if chunk > SUBLANES:
        mask = jnp.asarray(_hgrn_offdiag_mask(chunk))
        in_specs.append(pl.BlockSpec(mask.shape, lambda b, c: (0, 0)))
        args.append(mask)
    return pl.pallas_call(
        functools.partial(_hgrn_kernel, chunk=chunk, n_sub=step_rows // chunk),
        grid=(bsz, n_steps),
        in_specs=in_specs,
        out_specs=[tok, pl.BlockSpec((None, A_HEADS, A_DV, A_DK), lambda b, c: (b, 0, 0, 0))],
        out_shape=[jax.ShapeDtypeStruct((bsz * t_len, A_WIDTH), BF16),
                   jax.ShapeDtypeStruct((bsz, A_HEADS, A_DV, A_DK), F32)],
        scratch_shapes=([pltpu.VMEM((A_HEADS, A_DV, A_DK), F32)]
                        + [pltpu.VMEM((chunk, A_QK), F32)] * (3 * (step_rows // chunk))),
        compiler_params=_cparams("parallel", "arbitrary"),
        name="hgrn",
    )(*args)


def s5_weights(a_re, a_im, log_dt, b_re, b_im, c_re, c_im, d_skip):
    f32 = F32
    ar = a_re.astype(f32)
    ai = a_im.astype(f32)
    dt = jnp.exp(log_dt.astype(f32))[:, None]
    mag = jnp.exp(dt * ar)
    abar_re = mag * jnp.cos(dt * ai)
    abar_im = mag * jnp.sin(dt * ai)
    den = ar * ar + ai * ai
    zr = ((abar_re - 1.0) * ar + abar_im * ai) / den
    zi = (abar_im * ar - (abar_re - 1.0) * ai) / den
    br, bi = b_re.astype(f32), b_im.astype(f32)
    bb_re = zr[..., None] * br - zi[..., None] * bi
    bb_im = zr[..., None] * bi + zi[..., None] * br
    eye = jnp.eye(S5_GROUPS, dtype=f32)

    def in_proj(bb):
        return jnp.einsum('gnp,gh->gphn', bb, eye).reshape(B_WIDTH, S5_NSTATE)

    def out_proj(c):
        return jnp.einsum('gpn,gh->gnhp', c, eye).reshape(S5_NSTATE, B_WIDTH)

    n_sb = B_WIDTH // LANES
    sw = S5_NSTATE // n_sb

    def diag_blocks(full, rows, cols):
        return jnp.stack([full[c * rows:(c + 1) * rows, c * cols:(c + 1) * cols] for c in range(n_sb)])

    b_blk = jnp.concatenate([diag_blocks(in_proj(bb_re), LANES, sw), diag_blocks(in_proj(bb_im), LANES, sw)],
                            axis=2).astype(BF16)
    c_blk = jnp.concatenate([diag_blocks(out_proj(c_re.astype(f32)), sw, LANES),
                             -diag_blocks(out_proj(c_im.astype(f32)), sw, LANES)],
                            axis=1).astype(BF16)
    return dict(b_blk=b_blk, c_blk=c_blk, a_re=abar_re.reshape(1, S5_NSTATE),
                a_im=abar_im.reshape(1, S5_NSTATE), d=d_skip.astype(f32).reshape(1, B_WIDTH))


def _s5_kernel(u_ref, x0re_ref, x0im_ref, bblk_ref, cblk_ref, are_ref, aim_ref, d_ref,
               y_ref, fre_ref, fim_ref, u_scr, bu_scr, xre_scr, xim_scr, *, tt, nb):
    step = pl.program_id(0)

    @pl.when(step == 0)
    def _():
        xre_scr[...] = x0re_ref[...]
        xim_scr[...] = x0im_ref[...]

    n_lt = B_WIDTH // LANES
    u_bt = u_ref[...].reshape(nb * tt, B_WIDTH).astype(F32)
    for c in range(n_lt):
        u_scr[c] = u_bt[:, c * LANES:(c + 1) * LANES]
    sw = S5_NSTATE // n_lt
    for c in range(n_lt):
        st_l = slice(c * sw, (c + 1) * sw)
        ch_l = slice(c * LANES, (c + 1) * LANES)
        u_tb = jnp.concatenate([u_scr[c, pl.ds(t, nb, stride=tt), :] for t in range(tt)], axis=0)
        bu_scr[c] = jnp.dot(u_tb.astype(BF16), bblk_ref[c], preferred_element_type=F32)
        a_re = jnp.broadcast_to(are_ref[:, st_l], (nb, sw))
        a_im = jnp.broadcast_to(aim_ref[:, st_l], (nb, sw))

        def body(t, st, c=c, a_re=a_re, a_im=a_im):
            x_re, x_im = st
            r0 = pl.multiple_of(t * nb, nb)
            n_re = a_re * x_re - a_im * x_im + bu_scr[c, pl.ds(r0, nb), :sw]
            n_im = a_re * x_im + a_im * x_re + bu_scr[c, pl.ds(r0, nb), sw:]
            bu_scr[c, pl.ds(r0, nb), :sw] = n_re
            bu_scr[c, pl.ds(r0, nb), sw:] = n_im
            return n_re, n_im

        x_re, x_im = lax.fori_loop(0, tt, body, (xre_scr[:, st_l], xim_scr[:, st_l]))
        xre_scr[:, st_l] = x_re
        xim_scr[:, st_l] = x_im

        y = (jnp.dot(bu_scr[c].astype(BF16), cblk_ref[c], preferred_element_type=F32)
             + d_ref[:, ch_l] * u_tb)
        y = 0.5 * y * (1.0 + lax.erf(y * (1.0 / math.sqrt(2.0))))
        for t in range(tt):
            u_scr[c, pl.ds(t, nb, stride=tt), :] = y[nb * t:nb * (t + 1)]
    y_bt = jnp.concatenate([u_scr[c] for c in range(n_lt)], axis=1).reshape(nb, tt, B_WIDTH)
    y_ref[...] = y_bt.astype(y_ref.dtype)

    @pl.when(step == pl.num_programs(0) - 1)
    def _():
        fre_ref[...] = xre_scr[...]
        fim_ref[...] = xim_scr[...]


def s5(u, wts, x0_re, x0_im, bsz, t_len):
    nb = SUBLANES
    tt = min(S5_STEP_T, t_len)
    u3 = u.reshape(bsz, t_len, B_WIDTH)
    x0 = [x.astype(F32).reshape(bsz, S5_NSTATE) for x in (x0_re, x0_im)]
    if bsz < nb:
        u3 = jnp.pad(u3, ((0, nb - bsz), (0, 0), (0, 0)))
        x0 = [jnp.pad(x, ((0, nb - bsz), (0, 0))) for x in x0]
    fixed = lambda i: (0, 0)
    fixed3 = lambda i: (0, 0, 0)
    n_sb = B_WIDTH // LANES
    sw2 = 2 * S5_NSTATE // n_sb
    tok = pl.BlockSpec((nb, tt, B_WIDTH), lambda i: (0, i, 0))
    state = pl.BlockSpec((nb, S5_NSTATE), fixed)
    yg, fre, fim = pl.pallas_call(
        functools.partial(_s5_kernel, tt=tt, nb=nb),
        grid=(t_len // tt,),
        in_specs=[tok, state, state,
                  pl.BlockSpec((n_sb, LANES, sw2), fixed3), pl.BlockSpec((n_sb, sw2, LANES), fixed3),
                  pl.BlockSpec((1, S5_NSTATE), fixed), pl.BlockSpec((1, S5_NSTATE), fixed),
                  pl.BlockSpec((1, B_WIDTH), fixed)],
        out_specs=[tok, state, state],
        out_shape=[jax.ShapeDtypeStruct((nb, t_len, B_WIDTH), u.dtype),
                   jax.ShapeDtypeStruct((nb, S5_NSTATE), F32), jax.ShapeDtypeStruct((nb, S5_NSTATE), F32)],
        scratch_shapes=[pltpu.VMEM((n_sb, nb * tt, LANES), F32),
                        pltpu.VMEM((n_sb, nb * tt, sw2), F32),
                        pltpu.VMEM((nb, S5_NSTATE), F32), pltpu.VMEM((nb, S5_NSTATE), F32)],
        compiler_params=_cparams("arbitrary"),
        name="s5",
    )(u3, x0[0], x0[1], wts['b_blk'], wts['c_blk'], wts['a_re'], wts['a_im'], wts['d'])
    yg = yg[:bsz].reshape(bsz * t_len, B_WIDTH)
    return (yg, fre[:bsz].reshape(bsz, S5_GROUPS, S5_STATE), fim[:bsz].reshape(bsz, S5_GROUPS, S5_STATE))


def _sb_attn_kernel(*refs, heads, n_past, p_valid, scale):
    if n_past:
        q_ref, k_ref, v_ref, pk_ref, pv_ref, o_ref, carry_scr, acc_scr = refs
    else:
        q_ref, k_ref, v_ref, o_ref, carry_scr, acc_scr = refs
    blk = ATT_BLOCK
    i = pl.program_id(1)

    def suffix_matrix(tk):
        srow = lax.broadcasted_iota(jnp.int32, (2 * tk, tk), 0)
        scol = lax.broadcasted_iota(jnp.int32, (2 * tk, tk), 1)
        return jnp.where(jnp.where(srow >= tk, srow - tk, srow) > scol, 1.0, 0.0).astype(BF16)

    suffix = {tk: suffix_matrix(tk) for tk in (blk, ATT_SLAB)}
    rows = lax.broadcasted_iota(jnp.int32, (blk, blk), 0)
    cols = lax.broadcasted_iota(jnp.int32, (blk, blk), 1)

    carry_scr[...] = jnp.zeros_like(carry_scr)
    acc_scr[...] = jnp.zeros_like(acc_scr)
    head_lanes = [slice(hh * C_HD, (hh + 1) * C_HD) for hh in range(heads)]
    n_rows = heads * blk

    def per_head(x, fn):
        return jnp.concatenate([fn(x[hh * blk:(hh + 1) * blk]) for hh in range(heads)], axis=0)

    def visit(kr, vr, rj, specs):
        zs = []
        for s, (tk, _) in enumerate(specs):
            q_rows = slice(s * blk, (s + 1) * blk)
            zs.append(jnp.concatenate(
                [lax.dot_general(q_ref[q_rows, l], kr[pl.ds(rj, tk), l], (((1,), (1,)), ((), ())),
                                 preferred_element_type=F32) for l in head_lanes], axis=0) * scale)
        mids = []
        for z, (tk, mask) in zip(zs, specs):
            log_b = jnp.minimum(z, 0.0) - jnp.log(1.0 + jnp.exp(-jnp.abs(z)))
            x = log_b - z
            if mask is not None:
                x = per_head(x, lambda xh, mask=mask: jnp.where(mask, xh, 0.0))
            hi = x.astype(BF16)
            lo = (x - hi.astype(F32)).astype(BF16)
            cs = jnp.dot(jnp.concatenate([hi, lo], axis=1), suffix[tk], preferred_element_type=F32)
            total = jnp.broadcast_to(cs[:, 0:1] + x[:, 0:1], (n_rows, blk))
            mids.append((log_b, cs, total))
        live = None
        for s, ((log_b, cs, total), (tk, mask)) in enumerate(zip(mids, specs)):
            carry = carry_scr[s]
            w = jnp.exp(log_b + cs + jnp.concatenate([carry] * (tk // blk), axis=1))
            if mask is not None:
                w = per_head(w, lambda wh, mask=mask: jnp.where(mask, wh, 0.0))
            w = w.astype(BF16)
            acc_scr[s] += jnp.concatenate(
                [jnp.dot(w[hh * blk:(hh + 1) * blk], vr[pl.ds(rj, tk), l], preferred_element_type=F32)
                 for hh, l in enumerate(head_lanes)], axis=0)
            carry = carry + total
            carry_scr[s] = carry
            m = jnp.max(carry)
            live = m if live is None else jnp.maximum(live, m)
        return live

    rows2 = lax.broadcasted_iota(jnp.int32, (blk, ATT_SLAB), 0)
    cols2 = lax.broadcasted_iota(jnp.int32, (blk, ATT_SLAB), 1)
    live = visit(k_ref, v_ref, pl.multiple_of(i * ATT_SLAB, ATT_SLAB),
                 [(blk, cols < rows), (ATT_SLAB, cols2 < rows2 + blk)])

    def slab_body(st):
        j, _ = st
        return j - 2, visit(k_ref, v_ref, pl.multiple_of((j - 1) * blk, blk), [(ATT_SLAB, None)] * 2)

    def alive_from(first):
        return lambda st: jnp.logical_and(st[0] >= first, st[1] > ATT_DEAD_LOG_WEIGHT)

    _, live = lax.while_loop(alive_from(1), slab_body, (2 * i - 1, live))

    if n_past:
        def past_body(st):
            j, _ = st
            return j - 1, visit(pk_ref, pv_ref, pl.multiple_of(j * blk, blk),
                                [(blk, (cols + j * blk) < p_valid)] * 2)

        lax.while_loop(alive_from(0), past_body, (jnp.int32(n_past - 1), live))

    for s in range(2):
        for hh in range(heads):
            o_ref[s * blk:(s + 1) * blk, head_lanes[hh]] = (
                acc_scr[s, hh * blk:(hh + 1) * blk, :].astype(o_ref.dtype))


def sb_attention(q, k, v, past_k, past_v, p_valid):
    bsz, t_len, width = q.shape
    n_past = 0 if past_k is None else past_k.shape[1] // ATT_BLOCK
    q_spec = pl.BlockSpec((None, ATT_SLAB, width), lambda b, i: (b, i, 0))
    kv_spec = pl.BlockSpec((None, t_len, width), lambda b, i: (b, 0, 0), pipeline_mode=pl.Buffered(1))
    in_specs = [q_spec, kv_spec, kv_spec]
    args = [q, k, v]
    if n_past:
        pb = (lambda b: b) if past_k.shape[0] == bsz else (lambda b: 0)
        p_spec = pl.BlockSpec((None, past_k.shape[1], width), lambda b, i: (pb(b), 0, 0),
                              pipeline_mode=pl.Buffered(1))
        in_specs += [p_spec, p_spec]
        args += [past_k, past_v]
    return pl.pallas_call(
        functools.partial(_sb_attn_kernel, heads=C_HEADS, n_past=n_past, p_valid=p_valid,
                          scale=C_HD ** -0.5),
        grid=(bsz, t_len // ATT_SLAB),
        in_specs=in_specs,
        out_specs=q_spec,
        out_shape=jax.ShapeDtypeStruct((bsz, t_len, width), BF16),
        scratch_shapes=[pltpu.VMEM((2, C_HEADS * ATT_BLOCK, C_HD), F32)] * 2,
        compiler_params=_cparams("parallel", "arbitrary"),
        name="sb_attention",
    )(*args)


def _pad_rows(x, mult):
    t = x.shape[1]
    tp = -(-t // mult) * mult
    return x if tp == t else jnp.pad(x, ((0, 0), (0, tp - t), (0, 0)))


def _run_stream(x, bsz, t_len, wts, hgrn_s0_t, ssm0_re, ssm0_im, past_k, past_v, p_valid, lead_kv=None):
    act = BF16 if t_len % HGRN_CHUNK == 0 else F32
    q, zf, iv, g, u = norm_matmul(
        x, wts['ln_mix'][0], wts['w_in_even'],
        [(0, A_QK, act), (A_QK, A_QK, F32), (2 * A_QK, A_WIDTH, act),
         (2 * A_QK + A_WIDTH, A_WIDTH, F32), (2 * A_QK + 2 * A_WIDTH, B_WIDTH, act)])
    o_a, s_t = hgrn(q, zf, iv, g, wts['lb'], wts['hgrn_norm'], hgrn_s0_t, bsz, t_len)
    yg, x_re, x_im = s5(u, wts['s5'], ssm0_re, ssm0_im, bsz, t_len)
    x = even_out(x, o_a, yg, wts['w_glu'], wts['w_out_even_a'], wts['w_out_even_b'])
    x = mlp(x, wts['ln_mlp'][0], wts['w_up'][0], wts['w_down'][0], wts['ln_final'], final_norm=False)
    if lead_kv is None:
        qb, kb, vb, k32, v32 = norm_matmul(
            x, wts['ln_mix'][1], wts['w_in_odd'],
            [(0, D_MODEL, BF16), (D_MODEL, D_MODEL, BF16), (2 * D_MODEL, D_MODEL, BF16),
             (D_MODEL, D_MODEL, F32), (2 * D_MODEL, D_MODEL, F32)])
    else:
        qb, kb, vb, k32, v32 = qkv_shifted(x, wts['ln_mix'][1], wts['w_in_odd'], *lead_kv, bsz, t_len)
    shp = (bsz, t_len, D_MODEL)
    o = sb_attention(_pad_rows(qb.reshape(shp), ATT_SLAB), _pad_rows(kb.reshape(shp), ATT_SLAB),
                     _pad_rows(vb.reshape(shp), ATT_SLAB), past_k, past_v, p_valid)
    o = o[:, :t_len].reshape(bsz * t_len, D_MODEL)
    x = odd_out(x, o, wts['w_out_odd'])
    y = mlp(x, wts['ln_mlp'][1], wts['w_up'][1], wts['w_down'][1], wts['ln_final'], final_norm=True)
    return y, s_t, x_re, x_im, k32, v32


def kernel(x_prompt, x_sample, state_hgrn, state_ssm_re, state_ssm_im, cache_k, cache_v, meta_tokens,
           ln_mix, ln_mlp, ln_final, w_in_even, hgrn_lb, hgrn_norm, ssm_a_re, ssm_a_im, ssm_log_dt,
           ssm_b_re, ssm_b_im, ssm_c_re, ssm_c_im, ssm_d, w_glu, w_out_even, w_in_odd, w_out_odd,
           w_up, w_down):
    bsz, seq, _ = x_prompt.shape
    dbsz, dseq, _ = x_sample.shape
    past = cache_k.shape[2]
    lb_all = jnp.cumsum(jax.nn.softmax(hgrn_lb.astype(F32), axis=0), axis=0)
    w_out_e = w_out_even[0].astype(BF16)
    wts = dict(
        ln_mix=ln_mix.astype(F32), ln_mlp=ln_mlp.astype(F32), ln_final=ln_final.astype(F32),
        w_in_even=w_in_even[0].astype(BF16), lb=lb_all[0], hgrn_norm=hgrn_norm[0].astype(F32),
        s5=s5_weights(ssm_a_re[0], ssm_a_im[0], ssm_log_dt[0], ssm_b_re[0], ssm_b_im[0],
                      ssm_c_re[0], ssm_c_im[0], ssm_d[0]),
        w_glu=w_glu[0].astype(BF16), w_out_even_a=w_out_e[:A_WIDTH], w_out_even_b=w_out_e[A_WIDTH:],
        w_in_odd=w_in_odd[0].astype(BF16), w_out_odd=w_out_odd[0].astype(BF16),
        w_up=cast_bf16(w_up), w_down=cast_bf16(w_down))

    zeros_s = jnp.zeros((1, A_HEADS, A_DV, A_DK), F32)
    zeros_x = jnp.zeros((1, S5_GROUPS, S5_STATE), F32)
    _, m_s, m_re, m_im, m_k, m_v = _run_stream(
        meta_tokens.astype(F32), 1, N_META, wts, zeros_s, zeros_x, zeros_x, None, None, 0)

    m_kp = _pad_rows(m_k.astype(BF16)[None], ATT_BLOCK)
    m_vp = _pad_rows(m_v.astype(BF16)[None], ATT_BLOCK)
    y_p, s_p, re_p, im_p, k_p, v_p = _run_stream(
        x_prompt.reshape(bsz * seq, D_MODEL), bsz, seq, wts, m_s,
        jnp.broadcast_to(m_re, (bsz, S5_GROUPS, S5_STATE)), jnp.broadcast_to(m_im, (bsz, S5_GROUPS, S5_STATE)),
        m_kp, m_vp, N_META, lead_kv=(m_k, m_v))

    ck = cache_k[0].reshape(dbsz, past, D_MODEL).astype(BF16)
    cv = cache_v[0].reshape(dbsz, past, D_MODEL).astype(BF16)
    y_s, s_s, re_s, im_s, k_s, v_s = _run_stream(
        x_sample.reshape(dbsz * dseq, D_MODEL), dbsz, dseq, wts,
        jnp.swapaxes(state_hgrn[0].astype(F32), -1, -2), state_ssm_re[0], state_ssm_im[0],
        _pad_rows(ck, ATT_BLOCK), _pad_rows(cv, ATT_BLOCK), past)

    kv_shape = (1, bsz, N_META + seq, C_HEADS, C_HD)
    return (y_p.reshape(bsz, seq, D_MODEL), y_s.reshape(dbsz, dseq, D_MODEL),
            jnp.swapaxes(s_p, -1, -2)[None], jnp.swapaxes(s_s, -1, -2)[None],
            re_p[None], im_p[None], re_s[None], im_s[None],
            k_p.reshape(kv_shape), v_p.reshape(kv_shape),
            k_s.reshape(1, dbsz, dseq, C_HEADS, C_HD), v_s.reshape(1, dbsz, dseq, C_HEADS, C_HD))
```

```python
import functools
import math

import numpy as np
import jax
import jax.numpy as jnp
from jax import lax
from jax.experimental import pallas as pl
from jax.experimental.pallas import tpu as pltpu

F32 = jnp.float32
BF16 = jnp.bfloat16

D_MODEL = 1024
N_META = 16
A_HEADS = 4
A_DK = 128
A_DV = 128
A_WIDTH = 512
A_QK = A_HEADS * A_DK
B_WIDTH = 512
S5_GROUP = 16
S5_GROUPS = 32
S5_STATE = 64
S5_NSTATE = S5_GROUPS * S5_STATE
C_HEADS = 8
C_HD = 128
D_FF = 4 * D_MODEL
EPS = 1e-6

SUBLANES = 8
LANES = 128
VMEM_LIMIT_BYTES = 48 * 1024 * 1024

HGRN_CHUNK = 64
HGRN_STEP_ROWS = 256
S5_STEP_T = 64
ATT_BLOCK = 128
ATT_SLAB = 256
ATT_DEAD_LOG_WEIGHT = -104.0
NEG_BIG = -1e30


def _cparams(*sem):
    return pltpu.CompilerParams(dimension_semantics=sem, vmem_limit_bytes=VMEM_LIMIT_BYTES)


def _sigmoid(x):
    return 1.0 / (1.0 + jnp.exp(-x))


def _rms_scale(x):
    return lax.rsqrt(jnp.mean(x * x, axis=-1, keepdims=True) + EPS)


def _norm_matmul_kernel(x_ref, g_ref, w_ref, *out_refs, cols, nchunk):
    x = x_ref[...]
    h = (x * _rms_scale(x) * g_ref[...]).astype(BF16)
    for c0, width in sorted(set(cols)):
        for c in range(0, width, nchunk):
            res = jnp.dot(h, w_ref[:, c0 + c:c0 + c + nchunk], preferred_element_type=F32)
            for o_ref, col in zip(out_refs, cols):
                if col == (c0, width):
                    o_ref[:, c:c + nchunk] = res.astype(o_ref.dtype)


def norm_matmul(x, g, w_bf16, outs, *, tm=512, nchunk=512):
    m, d = x.shape
    tm = min(tm, m)
    kern = functools.partial(_norm_matmul_kernel, cols=[(c0, wd) for c0, wd, _ in outs], nchunk=nchunk)
    return pl.pallas_call(
        kern,
        grid=(pl.cdiv(m, tm),),
        in_specs=[pl.BlockSpec((tm, d), lambda i: (i, 0)),
                  pl.BlockSpec((1, d), lambda i: (0, 0)),
                  pl.BlockSpec(w_bf16.shape, lambda i: (0, 0))],
        out_specs=[pl.BlockSpec((tm, wd), lambda i: (i, 0)) for _, wd, _ in outs],
        out_shape=[jax.ShapeDtypeStruct((m, wd), dt) for _, wd, dt in outs],
        compiler_params=_cparams("parallel"),
        name="norm_matmul",
    )(x, g.reshape(1, d), w_bf16)


def _qkv_shifted_kernel(x_ref, g_ref, w_ref, lead_ref, qb_ref, kb_ref, vb_ref, kf_ref, vf_ref, carry_scr,
                        *, n_lead, nchunk):
    t = pl.program_id(1)
    nt = pl.num_programs(1) - 1
    tm, width = qb_ref.shape

    @pl.when(t == 0)
    def _():
        carry_scr[...] = lead_ref[...]

    for n, f_ref in enumerate((kf_ref, vf_ref)):
        f_ref[:n_lead, :] = carry_scr[n]

    @pl.when(t < nt)
    def _():
        x = x_ref[...]
        h = (x * _rms_scale(x) * g_ref[...]).astype(BF16)
        for n, (b_ref, f_ref) in enumerate(((qb_ref, None), (kb_ref, kf_ref), (vb_ref, vf_ref))):
            for c in range(0, width, nchunk):
                res = jnp.dot(h, w_ref[:, n * width + c:n * width + c + nchunk], preferred_element_type=F32)
                b_ref[:, c:c + nchunk] = res.astype(b_ref.dtype)
                if f_ref is not None:
                    f_ref[n_lead:, c:c + nchunk] = res[:tm - n_lead]
                    carry_scr[n - 1, :, c:c + nchunk] = res[tm - n_lead:]


def qkv_shifted(x, g, w_bf16, lead_k, lead_v, bsz, t_len, *, tm=512, nchunk=512):
    m, d = x.shape
    width = w_bf16.shape[1] // 3
    n_lead = lead_k.shape[0]
    tm = math.gcd(tm, t_len)
    nt = t_len // tm
    tok = lambda b, t: (b * nt + jnp.minimum(t, nt - 1), 0)
    return pl.pallas_call(
        functools.partial(_qkv_shifted_kernel, n_lead=n_lead, nchunk=nchunk),
        grid=(bsz, nt + 1),
        in_specs=[pl.BlockSpec((tm, d), tok),
                  pl.BlockSpec((1, d), lambda b, t: (0, 0)),
                  pl.BlockSpec(w_bf16.shape, lambda b, t: (0, 0)),
                  pl.BlockSpec((2, n_lead, width), lambda b, t: (0, 0, 0))],
        out_specs=[pl.BlockSpec((tm, width), tok)] * 3
                  + [pl.BlockSpec((None, tm, width), lambda b, t: (b, t, 0))] * 2,
        out_shape=[jax.ShapeDtypeStruct((m, width), BF16)] * 3
                  + [jax.ShapeDtypeStruct((bsz, n_lead + t_len, width), F32)] * 2,
        scratch_shapes=[pltpu.VMEM((2, n_lead, width), F32)],
        compiler_params=_cparams("parallel", "arbitrary"),
        name="qkv_shifted",
    )(x, g.reshape(1, d), w_bf16, jnp.stack([lead_k, lead_v]))


def _layer_tail_kernel(*refs, n_mix, final_norm):
    x_ref = refs[0]
    mix = refs[1:1 + n_mix]
    g_ref, wup_ref, wdn_ref, gf_ref, o_ref, h_scr = refs[1 + n_mix:]
    j = pl.program_id(1)

    @pl.when(j == 0)
    def _():
        if n_mix == 5:
            oa_ref, yg_ref, wglu_ref, wa_ref, wb_ref = mix
            yg = yg_ref[...]
            gate = _sigmoid(jnp.dot(yg.astype(BF16), wglu_ref[...], preferred_element_type=F32))
            ob = (yg.astype(F32) * gate).astype(BF16)
            x1 = (x_ref[...] + jnp.dot(oa_ref[...], wa_ref[...], preferred_element_type=F32)
                  + jnp.dot(ob, wb_ref[...], preferred_element_type=F32))
        else:
            om_ref, wo_ref = mix
            x1 = x_ref[...] + jnp.dot(om_ref[...], wo_ref[...], preferred_element_type=F32)
        h_scr[...] = (x1 * _rms_scale(x1) * g_ref[...]).astype(BF16)
        o_ref[...] = x1

    a = jnp.dot(h_scr[...], wup_ref[...], preferred_element_type=F32)
    a = jnp.square(jnp.maximum(a, 0.0)).astype(BF16)
    o_ref[...] += jnp.dot(a, wdn_ref[...], preferred_element_type=F32)

    if final_norm:
        @pl.when(j == pl.num_programs(1) - 1)
        def _():
            y = o_ref[...]
            o_ref[...] = y * _rms_scale(y) * gf_ref[...]


def layer_tail(x, mix_rows, mix_weights, g, w_up, w_down, g_final, *, final_norm, tm=1024, tf=1024):
    m, d = x.shape
    tm = min(tm, m)
    ff = w_up.shape[1]
    row = lambda i, j: (i, 0)
    fixed = lambda i, j: (0, 0)
    once = dict(pipeline_mode=pl.Buffered(1))
    return pl.pallas_call(
        functools.partial(_layer_tail_kernel, n_mix=len(mix_rows) + len(mix_weights), final_norm=final_norm),
        grid=(pl.cdiv(m, tm), ff // tf),
        in_specs=([pl.BlockSpec((tm, d), row)]
                  + [pl.BlockSpec((tm, r.shape[1]), row) for r in mix_rows]
                  + [pl.BlockSpec(w.shape, fixed, **once) for w in mix_weights]
                  + [pl.BlockSpec((1, d), fixed),
                     pl.BlockSpec((d, tf), lambda i, j: (0, j)),
                     pl.BlockSpec((tf, d), lambda i, j: (j, 0)),
                     pl.BlockSpec((1, d), fixed)]),
        out_specs=pl.BlockSpec((tm, d), row),
        out_shape=jax.ShapeDtypeStruct((m, d), F32),
        scratch_shapes=[pltpu.VMEM((tm, d), BF16)],
        compiler_params=_cparams("parallel", "arbitrary"),
        name="layer_tail",
    )(x, *mix_rows, *mix_weights, g.reshape(1, d), w_up, w_down, g_final.reshape(1, d))


def _split3(x):
    hi = x.astype(BF16)
    r1 = x - hi.astype(F32)
    mid = r1.astype(BF16)
    lo = (r1 - mid.astype(F32)).astype(BF16)
    return hi, mid, lo


def _hgrn_offdiag_mask(chunk):
    n_blk = chunk // SUBLANES
    seg = np.concatenate([np.full(SUBLANES * i, i) for i in range(1, n_blk)])
    blk = np.repeat(np.arange(1, n_blk), SUBLANES)
    return (blk[:, None] == seg[None, :]).astype(np.float32)


def _hgrn_kernel(*refs, chunk, n_sub):
    n_blk = chunk // SUBLANES
    row_scr = [refs[len(refs) - 3 * (n_sub - u):len(refs) - 3 * (n_sub - u - 1)] for u in range(n_sub)]
    refs = refs[:-3 * n_sub]
    if n_blk > 1:
        q_ref, zf_ref, iv_ref, g_ref, lb_ref, gn_ref, s0_ref, mask_ref, o_ref, sout_ref, s_scr = refs
    else:
        q_ref, zf_ref, iv_ref, g_ref, lb_ref, gn_ref, s0_ref, o_ref, sout_ref, s_scr = refs
    step = pl.program_id(1)

    @pl.when(step == 0)
    def _():
        s_scr[...] = s0_ref[...]

    lb = lb_ref[...]
    gn = gn_ref[...]
    rows = lax.broadcasted_iota(jnp.int32, (chunk, chunk), 0)
    cols = lax.broadcasted_iota(jnp.int32, (chunk, chunk), 1)
    tri = jnp.where(rows >= cols, 1.0, 0.0).astype(BF16)
    hrow = lax.broadcasted_iota(jnp.int32, (A_QK, A_QK), 0) // A_DK
    hcol = lax.broadcasted_iota(jnp.int32, (A_QK, A_QK), 1) // A_DK
    head_ones = jnp.where(hrow == hcol, 1.0, 0.0).astype(BF16)
    sub = lax.broadcasted_iota(jnp.int32, (SUBLANES, A_QK), 0)
    head_lanes = [slice(hh * A_DK, (hh + 1) * A_DK) for hh in range(A_HEADS)]
    nt = (((1,), (1,)), ((), ()))
    tn = (((0,), (0,)), ((), ()))

    def pairwise_phase(cc, b_scr, k_scr, iv_scr):
        r0 = cc * chunk
        q = q_ref[pl.ds(r0, chunk), :].astype(F32)
        zf = zf_ref[pl.ds(r0, chunk), :]
        iv = iv_ref[pl.ds(r0, chunk), :].astype(F32)
        g = g_ref[pl.ds(r0, chunk), :]
        f = lb + (1.0 - lb) * _sigmoid(zf)
        logf = jnp.log(f)
        kk = 1.0 - f
        hi, mid, lo = _split3(logf)
        b = (jnp.dot(tri, hi, preferred_element_type=F32)
             + jnp.dot(tri, mid, preferred_element_type=F32)
             + jnp.dot(tri, lo, preferred_element_type=F32))
        b_scr[...] = b
        k_scr[...] = kk
        iv_scr[...] = iv

        ws = []
        for i in range(n_blk):
            b_i = b[SUBLANES * i:SUBLANES * (i + 1)]
            q_i = q[SUBLANES * i:SUBLANES * (i + 1)]
            for s in range(SUBLANES):
                row = SUBLANES * i + s
                dlt = jnp.where(sub >= s, b_i - b_scr[row:row + 1, :], NEG_BIG)
                ws.append(jnp.exp(dlt) * q_i * k_scr[row:row + 1, :])
        w = jnp.concatenate(ws, axis=0).astype(BF16)
        return r0, q, iv, g, kk, b, w, b_scr, iv_scr

    def matmul_phase(r0, q, iv, g, kk, b, w, b_scr, iv_scr):
        att = jnp.dot(w, head_ones, preferred_element_type=F32)
        o_blocks = []
        for i in range(n_blk):
            acc = jnp.zeros((SUBLANES, A_WIDTH), F32)
            for s in range(SUBLANES):
                row = SUBLANES * i + s
                acc = acc + att[SUBLANES * row:SUBLANES * (row + 1)] * iv_scr[row:row + 1, :]
            o_blocks.append(acc)
        o = jnp.concatenate(o_blocks, axis=0)

        if n_blk > 1:
            qt, kh, ivs = [], [], []
            for i in range(1, n_blk):
                n_s = SUBLANES * i
                r_i = b_scr[n_s - 1:n_s, :]
                qt.append(q[n_s:n_s + SUBLANES] * jnp.exp(b[n_s:n_s + SUBLANES] - r_i))
                kh.append(kk[:n_s] * jnp.exp(r_i - b[:n_s]))
                ivs.append(iv[:n_s])
            qt = jnp.concatenate(qt, axis=0).astype(BF16)
            kh = jnp.concatenate(kh, axis=0).astype(BF16)
            ivs = jnp.concatenate(ivs, axis=0).astype(BF16)
            mask = mask_ref[...]
            o_off = []
            for l in head_lanes:
                a = lax.dot_general(qt[:, l], kh[:, l], nt, preferred_element_type=F32)
                o_off.append(jnp.dot((a * mask).astype(BF16), ivs[:, l], preferred_element_type=F32))
            o_off = jnp.concatenate(o_off, axis=1)
            o = o + jnp.concatenate([jnp.zeros((SUBLANES, A_WIDTH), F32), o_off], axis=0)

        qh = (q * jnp.exp(b)).astype(BF16)
        b_last = b[chunk - 1:chunk, :]
        kd = (kk * jnp.exp(b_last - b)).astype(BF16)
        dec = jnp.exp(b_last)
        ivb = iv.astype(BF16)
        o_inter = []
        for hh, l in enumerate(head_lanes):
            s_t = s_scr[hh]
            o_inter.append(lax.dot_general(qh[:, l], s_t.astype(BF16), nt, preferred_element_type=F32))
            upd = lax.dot_general(ivb[:, l], kd[:, l], tn, preferred_element_type=F32)
            s_scr[hh] = s_t * dec[:, l] + upd
        o = o + jnp.concatenate(o_inter, axis=1)

        o = jnp.concatenate([o[:, l] * _rms_scale(o[:, l]) * gn for l in head_lanes], axis=1)
        o = o * (g * _sigmoid(g))
        o_ref[pl.ds(r0, chunk), :] = o.astype(o_ref.dtype)

    pending = None
    for u in range(n_sub):
        ctx = pairwise_phase(u, *row_scr[u])
        if pending is not None:
            matmul_phase(*pending)
        pending = ctx
    matmul_phase(*pending)

    @pl.when(step == pl.num_programs(1) - 1)
    def _():
        sout_ref[...] = s_scr[...]


def hgrn(q, zf, iv, g, lb, gnorm, s0_t, bsz, t_len):
    chunk = HGRN_CHUNK if t_len % HGRN_CHUNK == 0 else SUBLANES
    step_rows = min(HGRN_STEP_ROWS, t_len)
    n_steps = t_len // step_rows
    tok = pl.BlockSpec((step_rows, A_QK), lambda b, c: (b * n_steps + c, 0))
    s0_b = (lambda b: b) if s0_t.shape[0] == bsz else (lambda b: 0)
    in_specs = [tok, tok, tok, tok,
                pl.BlockSpec((1, A_QK), lambda b, c: (0, 0)),
                pl.BlockSpec((1, A_DV), lambda b, c: (0, 0)),
                pl.BlockSpec((None, A_HEADS, A_DV, A_DK), lambda b, c: (s0_b(b), 0, 0, 0))]
    args = [q, zf, iv, g, lb.reshape(1, A_QK), gnorm.reshape(1, A_DV), s0_t]
    if chunk > SUBLANES:
        mask = jnp.asarray(_hgrn_offdiag_mask(chunk))
        in_specs.append(pl.BlockSpec(mask.shape, lambda b, c: (0, 0)))
        args.append(mask)
    return pl.pallas_call(
        functools.partial(_hgrn_kernel, chunk=chunk, n_sub=step_rows // chunk),
        grid=(bsz, n_steps),
        in_specs=in_specs,
        out_specs=[tok, pl.BlockSpec((None, A_HEADS, A_DV, A_DK), lambda b, c: (b, 0, 0, 0))],
        out_shape=[jax.ShapeDtypeStruct((bsz * t_len, A_WIDTH), BF16),
                   jax.ShapeDtypeStruct((bsz, A_HEADS, A_DV, A_DK), F32)],
        scratch_shapes=([pltpu.VMEM((A_HEADS, A_DV, A_DK), F32)]
                        + [pltpu.VMEM((chunk, A_QK), F32)] * (3 * (step_rows // chunk))),
        compiler_params=_cparams("parallel", "arbitrary"),
        name="hgrn",
    )(*args)


def s5_weights(a_re, a_im, log_dt, b_re, b_im, c_re, c_im, d_skip):
    f32 = F32
    ar = a_re.astype(f32)
    ai = a_im.astype(f32)
    dt = jnp.exp(log_dt.astype(f32))[:, None]
    mag = jnp.exp(dt * ar)
    abar_re = mag * jnp.cos(dt * ai)
    abar_im = mag * jnp.sin(dt * ai)
    den = ar * ar + ai * ai
    zr = ((abar_re - 1.0) * ar + abar_im * ai) / den
    zi = (abar_im * ar - (abar_re - 1.0) * ai) / den
    br, bi = b_re.astype(f32), b_im.astype(f32)
    bb_re = zr[..., None] * br - zi[..., None] * bi
    bb_im = zr[..., None] * bi + zi[..., None] * br
    eye = jnp.eye(S5_GROUPS, dtype=f32)

    def in_proj(bb):
        return jnp.einsum('gnp,gh->gphn', bb, eye).reshape(B_WIDTH, S5_NSTATE)

    def out_proj(c):
        return jnp.einsum('gpn,gh->gnhp', c, eye).reshape(S5_NSTATE, B_WIDTH)

    n_sb = B_WIDTH // LANES
    sw = S5_NSTATE // n_sb

    def diag_blocks(full, rows, cols):
        return jnp.stack([full[c * rows:(c + 1) * rows, c * cols:(c + 1) * cols] for c in range(n_sb)])

    b_blk = jnp.concatenate([diag_blocks(in_proj(bb_re), LANES, sw), diag_blocks(in_proj(bb_im), LANES, sw)],
                            axis=2).astype(BF16)
    c_blk = jnp.concatenate([diag_blocks(out_proj(c_re.astype(f32)), sw, LANES),
                             -diag_blocks(out_proj(c_im.astype(f32)), sw, LANES)],
                            axis=1).astype(BF16)
    return dict(b_blk=b_blk, c_blk=c_blk, a_re=abar_re.reshape(1, S5_NSTATE),
                a_im=abar_im.reshape(1, S5_NSTATE), d=d_skip.astype(f32).reshape(1, B_WIDTH))


def _s5_kernel(u_ref, x0re_ref, x0im_ref, bblk_ref, cblk_ref, are_ref, aim_ref, d_ref,
               y_ref, fre_ref, fim_ref, u_scr, bu_scr, xre_scr, xim_scr, *, tt, nb):
    step = pl.program_id(0)

    @pl.when(step == 0)
    def _():
        xre_scr[...] = x0re_ref[...]
        xim_scr[...] = x0im_ref[...]

    n_lt = B_WIDTH // LANES
    u_bt = u_ref[...].reshape(nb * tt, B_WIDTH).astype(F32)
    for c in range(n_lt):
        u_scr[c] = u_bt[:, c * LANES:(c + 1) * LANES]
    sw = S5_NSTATE // n_lt
    for c in range(n_lt):
        st_l = slice(c * sw, (c + 1) * sw)
        ch_l = slice(c * LANES, (c + 1) * LANES)
        u_tb = jnp.concatenate([u_scr[c, pl.ds(t, nb, stride=tt), :] for t in range(tt)], axis=0)
        bu_scr[c] = jnp.dot(u_tb.astype(BF16), bblk_ref[c], preferred_element_type=F32)
        a_re = jnp.broadcast_to(are_ref[:, st_l], (nb, sw))
        a_im = jnp.broadcast_to(aim_ref[:, st_l], (nb, sw))

        def body(t, st, c=c, a_re=a_re, a_im=a_im):
            x_re, x_im = st
            r0 = pl.multiple_of(t * nb, nb)
            n_re = a_re * x_re - a_im * x_im + bu_scr[c, pl.ds(r0, nb), :sw]
            n_im = a_re * x_im + a_im * x_re + bu_scr[c, pl.ds(r0, nb), sw:]
            bu_scr[c, pl.ds(r0, nb), :sw] = n_re
            bu_scr[c, pl.ds(r0, nb), sw:] = n_im
            return n_re, n_im

        x_re, x_im = lax.fori_loop(0, tt, body, (xre_scr[:, st_l], xim_scr[:, st_l]))
        xre_scr[:, st_l] = x_re
        xim_scr[:, st_l] = x_im

        y = (jnp.dot(bu_scr[c].astype(BF16), cblk_ref[c], preferred_element_type=F32)
             + d_ref[:, ch_l] * u_tb)
        y = 0.5 * y * (1.0 + lax.erf(y * (1.0 / math.sqrt(2.0))))
        for t in range(tt):
            u_scr[c, pl.ds(t, nb, stride=tt), :] = y[nb * t:nb * (t + 1)]
    y_bt = jnp.concatenate([u_scr[c] for c in range(n_lt)], axis=1).reshape(nb, tt, B_WIDTH)
    y_ref[...] = y_bt.astype(y_ref.dtype)

    @pl.when(step == pl.num_programs(0) - 1)
    def _():
        fre_ref[...] = xre_scr[...]
        fim_ref[...] = xim_scr[...]


def s5(u, wts, x0_re, x0_im, bsz, t_len):
    nb = SUBLANES
    tt = min(S5_STEP_T, t_len)
    u3 = u.reshape(bsz, t_len, B_WIDTH)
    x0 = [x.astype(F32).reshape(bsz, S5_NSTATE) for x in (x0_re, x0_im)]
    if bsz < nb:
        u3 = jnp.pad(u3, ((0, nb - bsz), (0, 0), (0, 0)))
        x0 = [jnp.pad(x, ((0, nb - bsz), (0, 0))) for x in x0]
    fixed = lambda i: (0, 0)
    fixed3 = lambda i: (0, 0, 0)
    n_sb = B_WIDTH // LANES
    sw2 = 2 * S5_NSTATE // n_sb
    tok = pl.BlockSpec((nb, tt, B_WIDTH), lambda i: (0, i, 0))
    state = pl.BlockSpec((nb, S5_NSTATE), fixed)
    yg, fre, fim = pl.pallas_call(
        functools.partial(_s5_kernel, tt=tt, nb=nb),
        grid=(t_len // tt,),
        in_specs=[tok, state, state,
                  pl.BlockSpec((n_sb, LANES, sw2), fixed3), pl.BlockSpec((n_sb, sw2, LANES), fixed3),
                  pl.BlockSpec((1, S5_NSTATE), fixed), pl.BlockSpec((1, S5_NSTATE), fixed),
                  pl.BlockSpec((1, B_WIDTH), fixed)],
        out_specs=[tok, state, state],
        out_shape=[jax.ShapeDtypeStruct((nb, t_len, B_WIDTH), u.dtype),
                   jax.ShapeDtypeStruct((nb, S5_NSTATE), F32), jax.ShapeDtypeStruct((nb, S5_NSTATE), F32)],
        scratch_shapes=[pltpu.VMEM((n_sb, nb * tt, LANES), F32),
                        pltpu.VMEM((n_sb, nb * tt, sw2), F32),
                        pltpu.VMEM((nb, S5_NSTATE), F32), pltpu.VMEM((nb, S5_NSTATE), F32)],
        compiler_params=_cparams("arbitrary"),
        name="s5",
    )(u3, x0[0], x0[1], wts['b_blk'], wts['c_blk'], wts['a_re'], wts['a_im'], wts['d'])
    yg = yg[:bsz].reshape(bsz * t_len, B_WIDTH)
    return (yg, fre[:bsz].reshape(bsz, S5_GROUPS, S5_STATE), fim[:bsz].reshape(bsz, S5_GROUPS, S5_STATE))


def _sb_attn_kernel(*refs, heads, n_past, p_valid, scale):
    if n_past:
        q_ref, k_ref, v_ref, pk_ref, pv_ref, o_ref, carry_scr, acc_scr = refs
    else:
        q_ref, k_ref, v_ref, o_ref, carry_scr, acc_scr = refs
    blk = ATT_BLOCK
    i = pl.program_id(1)

    def suffix_matrix(tk):
        srow = lax.broadcasted_iota(jnp.int32, (2 * tk, tk), 0)
        scol = lax.broadcasted_iota(jnp.int32, (2 * tk, tk), 1)
        return jnp.where(jnp.where(srow >= tk, srow - tk, srow) > scol, 1.0, 0.0).astype(BF16)

    suffix = {tk: suffix_matrix(tk) for tk in (blk, ATT_SLAB)}
    rows = lax.broadcasted_iota(jnp.int32, (blk, blk), 0)
    cols = lax.broadcasted_iota(jnp.int32, (blk, blk), 1)

    carry_scr[...] = jnp.zeros_like(carry_scr)
    acc_scr[...] = jnp.zeros_like(acc_scr)
    head_lanes = [slice(hh * C_HD, (hh + 1) * C_HD) for hh in range(heads)]
    n_rows = heads * blk

    def per_head(x, fn):
        return jnp.concatenate([fn(x[hh * blk:(hh + 1) * blk]) for hh in range(heads)], axis=0)

    def visit(kr, vr, rj, specs):
        zs = []
        for s, (tk, _) in enumerate(specs):
            q_rows = slice(s * blk, (s + 1) * blk)
            zs.append(jnp.concatenate(
                [lax.dot_general(q_ref[q_rows, l], kr[pl.ds(rj, tk), l].astype(BF16), (((1,), (1,)), ((), ())),
                                 preferred_element_type=F32) for l in head_lanes], axis=0) * scale)
        mids = []
        for z, (tk, mask) in zip(zs, specs):
            log_b = jnp.minimum(z, 0.0) - jnp.log(1.0 + jnp.exp(-jnp.abs(z)))
            x = log_b - z
            if mask is not None:
                x = per_head(x, lambda xh, mask=mask: jnp.where(mask, xh, 0.0))
            hi = x.astype(BF16)
            lo = (x - hi.astype(F32)).astype(BF16)
            cs = jnp.dot(jnp.concatenate([hi, lo], axis=1), suffix[tk], preferred_element_type=F32)
            total = jnp.broadcast_to(cs[:, 0:1] + x[:, 0:1], (n_rows, blk))
            mids.append((log_b, cs, total))
        live = None
        for s, ((log_b, cs, total), (tk, mask)) in enumerate(zip(mids, specs)):
            carry = carry_scr[s]
            w = jnp.exp(log_b + cs + jnp.concatenate([carry] * (tk // blk), axis=1))
            if mask is not None:
                w = per_head(w, lambda wh, mask=mask: jnp.where(mask, wh, 0.0))
            w = w.astype(BF16)
            acc_scr[s] += jnp.concatenate(
                [jnp.dot(w[hh * blk:(hh + 1) * blk], vr[pl.ds(rj, tk), l].astype(BF16),
                         preferred_element_type=F32) for hh, l in enumerate(head_lanes)], axis=0)
            carry = carry + total
            carry_scr[s] = carry
            m = jnp.max(carry)
            live = m if live is None else jnp.maximum(live, m)
        return live

    rows2 = lax.broadcasted_iota(jnp.int32, (blk, ATT_SLAB), 0)
    cols2 = lax.broadcasted_iota(jnp.int32, (blk, ATT_SLAB), 1)
    live = visit(k_ref, v_ref, pl.multiple_of(i * ATT_SLAB, ATT_SLAB),
                 [(blk, cols < rows), (ATT_SLAB, cols2 < rows2 + blk)])

    def slab_body(st):
        j, _ = st
        return j - 2, visit(k_ref, v_ref, pl.multiple_of((j - 1) * blk, blk), [(ATT_SLAB, None)] * 2)

    def alive_from(first):
        return lambda st: jnp.logical_and(st[0] >= first, st[1] > ATT_DEAD_LOG_WEIGHT)

    _, live = lax.while_loop(alive_from(1), slab_body, (2 * i - 1, live))

    if n_past:
        def past_body(st):
            j, _ = st
            return j - 1, visit(pk_ref, pv_ref, pl.multiple_of(j * blk, blk),
                                [(blk, (cols + j * blk) < p_valid)] * 2)

        lax.while_loop(alive_from(0), past_body, (jnp.int32(n_past - 1), live))

    for s in range(2):
        for hh in range(heads):
            o_ref[s * blk:(s + 1) * blk, head_lanes[hh]] = (
                acc_scr[s, hh * blk:(hh + 1) * blk, :].astype(o_ref.dtype))


def sb_attention(q, k, v, past_k, past_v, p_valid):
    bsz, t_len, width = q.shape
    n_past = 0 if past_k is None else past_k.shape[1] // ATT_BLOCK
    q_spec = pl.BlockSpec((None, ATT_SLAB, width), lambda b, i: (b, i, 0))
    kv_spec = pl.BlockSpec((None, t_len, width), lambda b, i: (b, 0, 0), pipeline_mode=pl.Buffered(1))
    in_specs = [q_spec, kv_spec, kv_spec]
    args = [q, k, v]
    if n_past:
        pb = (lambda b: b) if past_k.shape[0] == bsz else (lambda b: 0)
        p_spec = pl.BlockSpec((None, past_k.shape[1], width), lambda b, i: (pb(b), 0, 0),
                              pipeline_mode=pl.Buffered(1))
        in_specs += [p_spec, p_spec]
        args += [past_k, past_v]
    return pl.pallas_call(
        functools.partial(_sb_attn_kernel, heads=C_HEADS, n_past=n_past, p_valid=p_valid,
                          scale=C_HD ** -0.5),
        grid=(bsz, t_len // ATT_SLAB),
        in_specs=in_specs,
        out_specs=q_spec,
        out_shape=jax.ShapeDtypeStruct((bsz, t_len, width), BF16),
        scratch_shapes=[pltpu.VMEM((2, C_HEADS * ATT_BLOCK, C_HD), F32)] * 2,
        compiler_params=_cparams("parallel", "arbitrary"),
        name="sb_attention",
    )(*args)


def _pad_rows(x, mult):
    t = x.shape[1]
    tp = -(-t // mult) * mult
    return x if tp == t else jnp.pad(x, ((0, 0), (0, tp - t), (0, 0)))


def _run_stream(x, bsz, t_len, wts, hgrn_s0_t, ssm0_re, ssm0_im, past_k, past_v, p_valid, lead_kv=None):
    act = BF16 if t_len % HGRN_CHUNK == 0 else F32
    q, zf, iv, g, u = norm_matmul(
        x, wts['ln_mix'][0], wts['w_in_even'],
        [(0, A_QK, act), (A_QK, A_QK, F32), (2 * A_QK, A_WIDTH, act),
         (2 * A_QK + A_WIDTH, A_WIDTH, F32), (2 * A_QK + 2 * A_WIDTH, B_WIDTH, act)])
    o_a, s_t = hgrn(q, zf, iv, g, wts['lb'], wts['hgrn_norm'], hgrn_s0_t, bsz, t_len)
    yg, x_re, x_im = s5(u, wts['s5'], ssm0_re, ssm0_im, bsz, t_len)
    x = layer_tail(x, [o_a, yg], [wts['w_glu'], wts['w_out_even_a'], wts['w_out_even_b']],
                   wts['ln_mlp'][0], wts['w_up'][0], wts['w_down'][0], wts['ln_final'], final_norm=False)
    if lead_kv is None:
        qb, kb, vb, k32, v32 = norm_matmul(
            x, wts['ln_mix'][1], wts['w_in_odd'],
            [(0, D_MODEL, BF16), (D_MODEL, D_MODEL, BF16), (2 * D_MODEL, D_MODEL, BF16),
             (D_MODEL, D_MODEL, F32), (2 * D_MODEL, D_MODEL, F32)])
    else:
        qb, kb, vb, k32, v32 = qkv_shifted(x, wts['ln_mix'][1], wts['w_in_odd'], *lead_kv, bsz, t_len)
    shp = (bsz, t_len, D_MODEL)
    o = sb_attention(_pad_rows(qb.reshape(shp), ATT_SLAB), _pad_rows(kb.reshape(shp), ATT_SLAB),
                     _pad_rows(vb.reshape(shp), ATT_SLAB), past_k, past_v, p_valid)
    o = o[:, :t_len].reshape(bsz * t_len, D_MODEL)
    y = layer_tail(x, [o], [wts['w_out_odd']],
                   wts['ln_mlp'][1], wts['w_up'][1], wts['w_down'][1], wts['ln_final'], final_norm=True)
    return y, s_t, x_re, x_im, k32, v32


def kernel(x_prompt, x_sample, state_hgrn, state_ssm_re, state_ssm_im, cache_k, cache_v, meta_tokens,
           ln_mix, ln_mlp, ln_final, w_in_even, hgrn_lb, hgrn_norm, ssm_a_re, ssm_a_im, ssm_log_dt,
           ssm_b_re, ssm_b_im, ssm_c_re, ssm_c_im, ssm_d, w_glu, w_out_even, w_in_odd, w_out_odd,
           w_up, w_down):
    bsz, seq, _ = x_prompt.shape
    dbsz, dseq, _ = x_sample.shape
    past = cache_k.shape[2]
    lb_all = jnp.cumsum(jax.nn.softmax(hgrn_lb.astype(F32), axis=0), axis=0)
    w_out_e = w_out_even[0].astype(BF16)
    wts = dict(
        ln_mix=ln_mix.astype(F32), ln_mlp=ln_mlp.astype(F32), ln_final=ln_final.astype(F32),
        w_in_even=w_in_even[0].astype(BF16), lb=lb_all[0], hgrn_norm=hgrn_norm[0].astype(F32),
        s5=s5_weights(ssm_a_re[0], ssm_a_im[0], ssm_log_dt[0], ssm_b_re[0], ssm_b_im[0],
                      ssm_c_re[0], ssm_c_im[0], ssm_d[0]),
        w_glu=w_glu[0].astype(BF16), w_out_even_a=w_out_e[:A_WIDTH], w_out_even_b=w_out_e[A_WIDTH:],
        w_in_odd=w_in_odd[0].astype(BF16), w_out_odd=w_out_odd[0].astype(BF16),
        w_up=w_up.astype(BF16), w_down=w_down.astype(BF16))

    zeros_s = jnp.zeros((1, A_HEADS, A_DV, A_DK), F32)
    zeros_x = jnp.zeros((1, S5_GROUPS, S5_STATE), F32)
    _, m_s, m_re, m_im, m_k, m_v = _run_stream(
        meta_tokens.astype(F32), 1, N_META, wts, zeros_s, zeros_x, zeros_x, None, None, 0)

    m_kp = _pad_rows(m_k.astype(BF16)[None], ATT_BLOCK)
    m_vp = _pad_rows(m_v.astype(BF16)[None], ATT_BLOCK)
    y_p, s_p, re_p, im_p, k_p, v_p = _run_stream(
        x_prompt.reshape(bsz * seq, D_MODEL), bsz, seq, wts, m_s,
        jnp.broadcast_to(m_re, (bsz, S5_GROUPS, S5_STATE)), jnp.broadcast_to(m_im, (bsz, S5_GROUPS, S5_STATE)),
        m_kp, m_vp, N_META, lead_kv=(m_k, m_v))

    ck = cache_k[0].reshape(dbsz, past, D_MODEL)
    cv = cache_v[0].reshape(dbsz, past, D_MODEL)
    y_s, s_s, re_s, im_s, k_s, v_s = _run_stream(
        x_sample.reshape(dbsz * dseq, D_MODEL), dbsz, dseq, wts,
        jnp.swapaxes(state_hgrn[0].astype(F32), -1, -2), state_ssm_re[0], state_ssm_im[0],
        _pad_rows(ck, ATT_BLOCK), _pad_rows(cv, ATT_BLOCK), past)

    kv_shape = (1, bsz, N_META + seq, C_HEADS, C_HD)
    return (y_p.reshape(bsz, seq, D_MODEL), y_s.reshape(dbsz, dseq, D_MODEL),
            jnp.swapaxes(s_p, -1, -2)[None], jnp.swapaxes(s_s, -1, -2)[None],
            re_p[None], im_p[None], re_s[None], im_s[None],
            k_p.reshape(kv_shape), v_p.reshape(kv_shape),
            k_s.reshape(1, dbsz, dseq, C_HEADS, C_HD), v_s.reshape(1, dbsz, dseq, C_HEADS, C_HD))
```

```python
import functools
import math

import numpy as np
import jax
import jax.numpy as jnp
from jax import lax
from jax.experimental import pallas as pl
from jax.experimental.pallas import tpu as pltpu

F32 = jnp.float32
BF16 = jnp.bfloat16

D_MODEL = 1024
N_META = 16
A_HEADS = 4
A_DK = 128
A_DV = 128
A_WIDTH = 512
A_QK = A_HEADS * A_DK
B_WIDTH = 512
S5_GROUP = 16
S5_GROUPS = 32
S5_STATE = 64
S5_NSTATE = S5_GROUPS * S5_STATE
C_HEADS = 8
C_HD = 128
D_FF = 4 * D_MODEL
EPS = 1e-6

SUBLANES = 8
LANES = 128
VMEM_LIMIT_BYTES = 48 * 1024 * 1024

HGRN_CHUNK = 64
HGRN_STEP_ROWS = 512
LAYER_TAIL_SUB_ROWS = 256
S5_STEP_T = 64
ATT_BLOCK = 128
ATT_SLAB = 256
ATT_DEAD_LOG_WEIGHT = -104.0
NEG_BIG = -1e30


def _cparams(*sem):
    return pltpu.CompilerParams(dimension_semantics=sem, vmem_limit_bytes=VMEM_LIMIT_BYTES)


def _sigmoid(x):
    return 1.0 / (1.0 + jnp.exp(-x))


def _rms_scale(x):
    return lax.rsqrt(jnp.mean(x * x, axis=-1, keepdims=True) + EPS)


def _norm_matmul_kernel(x_ref, g_ref, w_ref, *out_refs, cols, nchunk):
    x = x_ref[...]
    h = (x * _rms_scale(x) * g_ref[...]).astype(BF16)
    for c0, width in sorted(set(cols)):
        for c in range(0, width, nchunk):
            res = jnp.dot(h, w_ref[:, c0 + c:c0 + c + nchunk], preferred_element_type=F32)
            for o_ref, col in zip(out_refs, cols):
                if col == (c0, width):
                    o_ref[:, c:c + nchunk] = res.astype(o_ref.dtype)


def norm_matmul(x, g, w_bf16, outs, *, tm=512, nchunk=512):
    m, d = x.shape
    tm = min(tm, m)
    kern = functools.partial(_norm_matmul_kernel, cols=[(c0, wd) for c0, wd, _ in outs], nchunk=nchunk)
    return pl.pallas_call(
        kern,
        grid=(pl.cdiv(m, tm),),
        in_specs=[pl.BlockSpec((tm, d), lambda i: (i, 0)),
                  pl.BlockSpec((1, d), lambda i: (0, 0)),
                  pl.BlockSpec(w_bf16.shape, lambda i: (0, 0))],
        out_specs=[pl.BlockSpec((tm, wd), lambda i: (i, 0)) for _, wd, _ in outs],
        out_shape=[jax.ShapeDtypeStruct((m, wd), dt) for _, wd, dt in outs],
        compiler_params=_cparams("parallel"),
        name="norm_matmul",
    )(x, g.reshape(1, d), w_bf16)


def _qkv_shifted_kernel(x_ref, g_ref, w_ref, lead_ref, qb_ref, kb_ref, vb_ref, kf_ref, vf_ref, carry_scr,
                        *, n_lead, nchunk):
    t = pl.program_id(1)
    nt = pl.num_programs(1) - 1
    tm, width = qb_ref.shape

    @pl.when(t == 0)
    def _():
        carry_scr[...] = lead_ref[...]

    for n, f_ref in enumerate((kf_ref, vf_ref)):
        f_ref[:n_lead, :] = carry_scr[n]

    @pl.when(t < nt)
    def _():
        x = x_ref[...]
        h = (x * _rms_scale(x) * g_ref[...]).astype(BF16)
        for n, (b_ref, f_ref) in enumerate(((qb_ref, None), (kb_ref, kf_ref), (vb_ref, vf_ref))):
            for c in range(0, width, nchunk):
                res = jnp.dot(h, w_ref[:, n * width + c:n * width + c + nchunk], preferred_element_type=F32)
                b_ref[:, c:c + nchunk] = res.astype(b_ref.dtype)
                if f_ref is not None:
                    f_ref[n_lead:, c:c + nchunk] = res[:tm - n_lead]
                    carry_scr[n - 1, :, c:c + nchunk] = res[tm - n_lead:]


def qkv_shifted(x, g, w_bf16, lead_k, lead_v, bsz, t_len, *, tm=512, nchunk=512):
    m, d = x.shape
    width = w_bf16.shape[1] // 3
    n_lead = lead_k.shape[0]
    tm = math.gcd(tm, t_len)
    nt = t_len // tm
    tok = lambda b, t: (b * nt + jnp.minimum(t, nt - 1), 0)
    return pl.pallas_call(
        functools.partial(_qkv_shifted_kernel, n_lead=n_lead, nchunk=nchunk),
        grid=(bsz, nt + 1),
        in_specs=[pl.BlockSpec((tm, d), tok),
                  pl.BlockSpec((1, d), lambda b, t: (0, 0)),
                  pl.BlockSpec(w_bf16.shape, lambda b, t: (0, 0)),
                  pl.BlockSpec((2, n_lead, width), lambda b, t: (0, 0, 0))],
        out_specs=[pl.BlockSpec((tm, width), tok)] * 3
                  + [pl.BlockSpec((None, tm, width), lambda b, t: (b, t, 0))] * 2,
        out_shape=[jax.ShapeDtypeStruct((m, width), BF16)] * 3
                  + [jax.ShapeDtypeStruct((bsz, n_lead + t_len, width), F32)] * 2,
        scratch_shapes=[pltpu.VMEM((2, n_lead, width), F32)],
        compiler_params=_cparams("parallel", "arbitrary"),
        name="qkv_shifted",
    )(x, g.reshape(1, d), w_bf16, jnp.stack([lead_k, lead_v]))


def _layer_tail_kernel(*refs, n_mix, final_norm):
    x_ref = refs[0]
    mix = refs[1:1 + n_mix]
    g_ref, wup_ref, wdn_ref, gf_ref, o_ref, h_scr = refs[1 + n_mix:]
    j = pl.program_id(1)

    @pl.when(j == 0)
    def _():
        tm = x_ref.shape[0]
        sub = math.gcd(tm, LAYER_TAIL_SUB_ROWS)
        for r0 in range(0, tm, sub):
            rs = slice(r0, r0 + sub)
            if n_mix == 5:
                oa_ref, yg_ref, wglu_ref, wa_ref, wb_ref = mix
                yg = yg_ref[rs, :]
                gate = _sigmoid(jnp.dot(yg.astype(BF16), wglu_ref[...], preferred_element_type=F32))
                ob = (yg.astype(F32) * gate).astype(BF16)
                x1 = (x_ref[rs, :] + jnp.dot(oa_ref[rs, :], wa_ref[...], preferred_element_type=F32)
                      + jnp.dot(ob, wb_ref[...], preferred_element_type=F32))
            else:
                om_ref, wo_ref = mix
                x1 = x_ref[rs, :] + jnp.dot(om_ref[rs, :], wo_ref[...], preferred_element_type=F32)
            h_scr[rs, :] = (x1 * _rms_scale(x1) * g_ref[...]).astype(BF16)
            o_ref[rs, :] = x1

    a = jnp.dot(h_scr[...], wup_ref[...], preferred_element_type=F32)
    a = jnp.square(jnp.maximum(a, 0.0)).astype(BF16)
    o_ref[...] += jnp.dot(a, wdn_ref[...], preferred_element_type=F32)

    if final_norm:
        @pl.when(j == pl.num_programs(1) - 1)
        def _():
            y = o_ref[...]
            o_ref[...] = y * _rms_scale(y) * gf_ref[...]


def layer_tail(x, mix_rows, mix_weights, g, w_up, w_down, g_final, *, final_norm, tm=1024, tf=1024):
    m, d = x.shape
    tm = min(tm, m)
    ff = w_up.shape[1]
    row = lambda i, j: (i, 0)
    fixed = lambda i, j: (0, 0)
    once = dict(pipeline_mode=pl.Buffered(1))
    return pl.pallas_call(
        functools.partial(_layer_tail_kernel, n_mix=len(mix_rows) + len(mix_weights), final_norm=final_norm),
        grid=(pl.cdiv(m, tm), ff // tf),
        in_specs=([pl.BlockSpec((tm, d), row)]
                  + [pl.BlockSpec((tm, r.shape[1]), row) for r in mix_rows]
                  + [pl.BlockSpec(w.shape, fixed, **once) for w in mix_weights]
                  + [pl.BlockSpec((1, d), fixed),
                     pl.BlockSpec((d, tf), lambda i, j: (0, j)),
                     pl.BlockSpec((tf, d), lambda i, j: (j, 0)),
                     pl.BlockSpec((1, d), fixed)]),
        out_specs=pl.BlockSpec((tm, d), row),
        out_shape=jax.ShapeDtypeStruct((m, d), F32),
        scratch_shapes=[pltpu.VMEM((tm, d), BF16)],
        compiler_params=_cparams("parallel", "arbitrary"),
        name="layer_tail",
    )(x, *mix_rows, *mix_weights, g.reshape(1, d), w_up, w_down, g_final.reshape(1, d))


def _split3(x):
    hi = x.astype(BF16)
    r1 = x - hi.astype(F32)
    mid = r1.astype(BF16)
    lo = (r1 - mid.astype(F32)).astype(BF16)
    return hi, mid, lo


def _hgrn_offdiag_mask(chunk):
    n_blk = chunk // SUBLANES
    seg = np.concatenate([np.full(SUBLANES * i, i) for i in range(1, n_blk)])
    blk = np.repeat(np.arange(1, n_blk), SUBLANES)
    return (blk[:, None] == seg[None, :]).astype(np.float32)


def _hgrn_kernel(*refs, chunk, n_sub):
    n_blk = chunk // SUBLANES
    row_scr = [refs[len(refs) - 3 * (n_sub - u):len(refs) - 3 * (n_sub - u - 1)] for u in range(n_sub)]
    refs = refs[:-3 * n_sub]
    if n_blk > 1:
        q_ref, zf_ref, iv_ref, g_ref, lb_ref, gn_ref, s0_ref, mask_ref, o_ref, sout_ref, s_scr = refs
    else:
        q_ref, zf_ref, iv_ref, g_ref, lb_ref, gn_ref, s0_ref, o_ref, sout_ref, s_scr = refs
    step = pl.program_id(1)

    @pl.when(step == 0)
    def _():
        s_scr[...] = s0_ref[...]

    lb = lb_ref[...]
    gn = gn_ref[...]
    rows = lax.broadcasted_iota(jnp.int32, (chunk, chunk), 0)
    cols = lax.broadcasted_iota(jnp.int32, (chunk, chunk), 1)
    tri = jnp.where(rows >= cols, 1.0, 0.0).astype(BF16)
    hrow = lax.broadcasted_iota(jnp.int32, (A_QK, A_QK), 0) // A_DK
    hcol = lax.broadcasted_iota(jnp.int32, (A_QK, A_QK), 1) // A_DK
    head_ones = jnp.where(hrow == hcol, 1.0, 0.0).astype(BF16)
    sub = lax.broadcasted_iota(jnp.int32, (SUBLANES, A_QK), 0)
    head_lanes = [slice(hh * A_DK, (hh + 1) * A_DK) for hh in range(A_HEADS)]
    nt = (((1,), (1,)), ((), ()))
    tn = (((0,), (0,)), ((), ()))

    def pairwise_phase(cc, b_scr, k_scr, iv_scr):
        r0 = cc * chunk
        q = q_ref[pl.ds(r0, chunk), :].astype(F32)
        zf = zf_ref[pl.ds(r0, chunk), :]
        iv = iv_ref[pl.ds(r0, chunk), :].astype(F32)
        g = g_ref[pl.ds(r0, chunk), :]
        f = lb + (1.0 - lb) * _sigmoid(zf)
        logf = jnp.log(f)
        kk = 1.0 - f
        hi, mid, lo = _split3(logf)
        b = (jnp.dot(tri, hi, preferred_element_type=F32)
             + jnp.dot(tri, mid, preferred_element_type=F32)
             + jnp.dot(tri, lo, preferred_element_type=F32))
        b_scr[...] = b
        k_scr[...] = kk
        iv_scr[...] = iv

        ws = []
        for i in range(n_blk):
            b_i = b[SUBLANES * i:SUBLANES * (i + 1)]
            q_i = q[SUBLANES * i:SUBLANES * (i + 1)]
            for s in range(SUBLANES):
                row = SUBLANES * i + s
                dlt = jnp.where(sub >= s, b_i - b_scr[row:row + 1, :], NEG_BIG)
                ws.append(jnp.exp(dlt) * q_i * k_scr[row:row + 1, :])
        w = jnp.concatenate(ws, axis=0).astype(BF16)
        return r0, q, iv, g, kk, b, w, b_scr, iv_scr

    def matmul_phase(r0, q, iv, g, kk, b, w, b_scr, iv_scr):
        att = jnp.dot(w, head_ones, preferred_element_type=F32)
        o_blocks = []
        for i in range(n_blk):
            acc = jnp.zeros((SUBLANES, A_WIDTH), F32)
            for s in range(SUBLANES):
                row = SUBLANES * i + s
                acc = acc + att[SUBLANES * row:SUBLANES * (row + 1)] * iv_scr[row:row + 1, :]
            o_blocks.append(acc)
        o = jnp.concatenate(o_blocks, axis=0)

        if n_blk > 1:
            qt, kh, ivs = [], [], []
            for i in range(1, n_blk):
                n_s = SUBLANES * i
                r_i = b_scr[n_s - 1:n_s, :]
                qt.append(q[n_s:n_s + SUBLANES] * jnp.exp(b[n_s:n_s + SUBLANES] - r_i))
                kh.append(kk[:n_s] * jnp.exp(r_i - b[:n_s]))
                ivs.append(iv[:n_s])
            qt = jnp.concatenate(qt, axis=0).astype(BF16)
            kh = jnp.concatenate(kh, axis=0).astype(BF16)
            ivs = jnp.concatenate(ivs, axis=0).astype(BF16)
            mask = mask_ref[...]
            o_off = []
            for l in head_lanes:
                a = lax.dot_general(qt[:, l], kh[:, l], nt, preferred_element_type=F32)
                o_off.append(jnp.dot((a * mask).astype(BF16), ivs[:, l], preferred_element_type=F32))
            o_off = jnp.concatenate(o_off, axis=1)
            o = o + jnp.concatenate([jnp.zeros((SUBLANES, A_WIDTH), F32), o_off], axis=0)

        qh = (q * jnp.exp(b)).astype(BF16)
        b_last = b[chunk - 1:chunk, :]
        kd = (kk * jnp.exp(b_last - b)).astype(BF16)
        dec = jnp.exp(b_last)
        ivb = iv.astype(BF16)
        o_inter = []
        for hh, l in enumerate(head_lanes):
            s_t = s_scr[hh]
            o_inter.append(lax.dot_general(qh[:, l], s_t.astype(BF16), nt, preferred_element_type=F32))
            upd = lax.dot_general(ivb[:, l], kd[:, l], tn, preferred_element_type=F32)
            s_scr[hh] = s_t * dec[:, l] + upd
        o = o + jnp.concatenate(o_inter, axis=1)

        o = jnp.concatenate([o[:, l] * _rms_scale(o[:, l]) * gn for l in head_lanes], axis=1)
        o = o * (g * _sigmoid(g))
        o_ref[pl.ds(r0, chunk), :] = o.astype(o_ref.dtype)

    pending = None
    for u in range(n_sub):
        ctx = pairwise_phase(u, *row_scr[u])
        if pending is not None:
            matmul_phase(*pending)
        pending = ctx
    matmul_phase(*pending)

    @pl.when(step == pl.num_programs(1) - 1)
    def _():
        sout_ref[...] = s_scr[...]


def hgrn(q, zf, iv, g, lb, gnorm, s0_t, bsz, t_len):
    chunk = HGRN_CHUNK if t_len % HGRN_CHUNK == 0 else SUBLANES
    step_rows = min(HGRN_STEP_ROWS, t_len)
    n_steps = t_len // step_rows
    tok = pl.BlockSpec((step_rows, A_QK), lambda b, c: (b * n_steps + c, 0))
    s0_b = (lambda b: b) if s0_t.shape[0] == bsz else (lambda b: 0)
    in_specs = [tok, tok, tok, tok,
                pl.BlockSpec((1, A_QK), lambda b, c: (0, 0)),
                pl.BlockSpec((1, A_DV), lambda b, c: (0, 0)),
                pl.BlockSpec((None, A_HEADS, A_DV, A_DK), lambda b, c: (s0_b(b), 0, 0, 0))]
    args = [q, zf, iv, g, lb.reshape(1, A_QK), gnorm.reshape(1, A_DV), s0_t]
    if chunk > SUBLANES:
        mask = jnp.asarray(_hgrn_offdiag_mask(chunk))
        in_specs.append(pl.BlockSpec(mask.shape, lambda b, c: (0, 0)))
        args.append(mask)
    return pl.pallas_call(
        functools.partial(_hgrn_kernel, chunk=chunk, n_sub=step_rows // chunk),
        grid=(bsz, n_steps),
        in_specs=in_specs,
        out_specs=[tok, pl.BlockSpec((None, A_HEADS, A_DV, A_DK), lambda b, c: (b, 0, 0, 0))],
        out_shape=[jax.ShapeDtypeStruct((bsz * t_len, A_WIDTH), BF16),
                   jax.ShapeDtypeStruct((bsz, A_HEADS, A_DV, A_DK), F32)],
        scratch_shapes=([pltpu.VMEM((A_HEADS, A_DV, A_DK), F32)]
                        + [pltpu.VMEM((chunk, A_QK), F32)] * (3 * (step_rows // chunk))),
        compiler_params=_cparams("parallel", "arbitrary"),
        name="hgrn",
    )(*args)


def s5_weights(a_re, a_im, log_dt, b_re, b_im, c_re, c_im, d_skip):
    f32 = F32
    ar = a_re.astype(f32)
    ai = a_im.astype(f32)
    dt = jnp.exp(log_dt.astype(f32))[:, None]
    mag = jnp.exp(dt * ar)
    abar_re = mag * jnp.cos(dt * ai)
    abar_im = mag * jnp.sin(dt * ai)
    den = ar * ar + ai * ai
    zr = ((abar_re - 1.0) * ar + abar_im * ai) / den
    zi = (abar_im * ar - (abar_re - 1.0) * ai) / den
    br, bi = b_re.astype(f32), b_im.astype(f32)
    bb_re = zr[..., None] * br - zi[..., None] * bi
    bb_im = zr[..., None] * bi + zi[..., None] * br
    eye = jnp.eye(S5_GROUPS, dtype=f32)

    def in_proj(bb):
        return jnp.einsum('gnp,gh->gphn', bb, eye).reshape(B_WIDTH, S5_NSTATE)

    def out_proj(c):
        return jnp.einsum('gpn,gh->gnhp', c, eye).reshape(S5_NSTATE, B_WIDTH)

    n_sb = B_WIDTH // LANES
    sw = S5_NSTATE // n_sb

    def diag_blocks(full, rows, cols):
        return jnp.stack([full[c * rows:(c + 1) * rows, c * cols:(c + 1) * cols] for c in range(n_sb)])

    b_blk = jnp.concatenate([diag_blocks(in_proj(bb_re), LANES, sw), diag_blocks(in_proj(bb_im), LANES, sw)],
                            axis=2).astype(BF16)
    c_blk = jnp.concatenate([diag_blocks(out_proj(c_re.astype(f32)), sw, LANES),
                             -diag_blocks(out_proj(c_im.astype(f32)), sw, LANES)],
                            axis=1).astype(BF16)
    return dict(b_blk=b_blk, c_blk=c_blk, a_re=abar_re.reshape(1, S5_NSTATE),
                a_im=abar_im.reshape(1, S5_NSTATE), d=d_skip.astype(f32).reshape(1, B_WIDTH))


def _s5_kernel(u_ref, x0re_ref, x0im_ref, bblk_ref, cblk_ref, are_ref, aim_ref, d_ref,
               y_ref, fre_ref, fim_ref, u_scr, bu_scr, xre_scr, xim_scr, *, tt, nb):
    step = pl.program_id(0)

    @pl.when(step == 0)
    def _():
        xre_scr[...] = x0re_ref[...]
        xim_scr[...] = x0im_ref[...]

    n_lt = B_WIDTH // LANES
    u_bt = u_ref[...].reshape(nb * tt, B_WIDTH).astype(F32)
    for c in range(n_lt):
        u_scr[c] = u_bt[:, c * LANES:(c + 1) * LANES]
    sw = S5_NSTATE // n_lt
    for c in range(n_lt):
        st_l = slice(c * sw, (c + 1) * sw)
        ch_l = slice(c * LANES, (c + 1) * LANES)
        u_tb = jnp.concatenate([u_scr[c, pl.ds(t, nb, stride=tt), :] for t in range(tt)], axis=0)
        bu_scr[c] = jnp.dot(u_tb.astype(BF16), bblk_ref[c], preferred_element_type=F32)
        a_re = jnp.broadcast_to(are_ref[:, st_l], (nb, sw))
        a_im = jnp.broadcast_to(aim_ref[:, st_l], (nb, sw))

        def body(t, st, c=c, a_re=a_re, a_im=a_im):
            x_re, x_im = st
            r0 = pl.multiple_of(t * nb, nb)
            n_re = a_re * x_re - a_im * x_im + bu_scr[c, pl.ds(r0, nb), :sw]
            n_im = a_re * x_im + a_im * x_re + bu_scr[c, pl.ds(r0, nb), sw:]
            bu_scr[c, pl.ds(r0, nb), :sw] = n_re
            bu_scr[c, pl.ds(r0, nb), sw:] = n_im
            return n_re, n_im

        x_re, x_im = lax.fori_loop(0, tt, body, (xre_scr[:, st_l], xim_scr[:, st_l]))
        xre_scr[:, st_l] = x_re
        xim_scr[:, st_l] = x_im

        y = (jnp.dot(bu_scr[c].astype(BF16), cblk_ref[c], preferred_element_type=F32)
             + d_ref[:, ch_l] * u_tb)
        y = 0.5 * y * (1.0 + lax.erf(y * (1.0 / math.sqrt(2.0))))
        for t in range(tt):
            u_scr[c, pl.ds(t, nb, stride=tt), :] = y[nb * t:nb * (t + 1)]
    y_bt = jnp.concatenate([u_scr[c] for c in range(n_lt)], axis=1).reshape(nb, tt, B_WIDTH)
    y_ref[...] = y_bt.astype(y_ref.dtype)

    @pl.when(step == pl.num_programs(0) - 1)
    def _():
        fre_ref[...] = xre_scr[...]
        fim_ref[...] = xim_scr[...]


def s5(u, wts, x0_re, x0_im, bsz, t_len):
    nb = SUBLANES
    tt = min(S5_STEP_T, t_len)
    u3 = u.reshape(bsz, t_len, B_WIDTH)
    x0 = [x.astype(F32).reshape(bsz, S5_NSTATE) for x in (x0_re, x0_im)]
    if bsz < nb:
        u3 = jnp.pad(u3, ((0, nb - bsz), (0, 0), (0, 0)))
        x0 = [jnp.pad(x, ((0, nb - bsz), (0, 0))) for x in x0]
    fixed = lambda i: (0, 0)
    fixed3 = lambda i: (0, 0, 0)
    n_sb = B_WIDTH // LANES
    sw2 = 2 * S5_NSTATE // n_sb
    tok = pl.BlockSpec((nb, tt, B_WIDTH), lambda i: (0, i, 0))
    state = pl.BlockSpec((nb, S5_NSTATE), fixed)
    yg, fre, fim = pl.pallas_call(
        functools.partial(_s5_kernel, tt=tt, nb=nb),
        grid=(t_len // tt,),
        in_specs=[tok, state, state,
                  pl.BlockSpec((n_sb, LANES, sw2), fixed3), pl.BlockSpec((n_sb, sw2, LANES), fixed3),
                  pl.BlockSpec((1, S5_NSTATE), fixed), pl.BlockSpec((1, S5_NSTATE), fixed),
                  pl.BlockSpec((1, B_WIDTH), fixed)],
        out_specs=[tok, state, state],
        out_shape=[jax.ShapeDtypeStruct((nb, t_len, B_WIDTH), u.dtype),
                   jax.ShapeDtypeStruct((nb, S5_NSTATE), F32), jax.ShapeDtypeStruct((nb, S5_NSTATE), F32)],
        scratch_shapes=[pltpu.VMEM((n_sb, nb * tt, LANES), F32),
                        pltpu.VMEM((n_sb, nb * tt, sw2), F32),
                        pltpu.VMEM((nb, S5_NSTATE), F32), pltpu.VMEM((nb, S5_NSTATE), F32)],
        compiler_params=_cparams("arbitrary"),
        name="s5",
    )(u3, x0[0], x0[1], wts['b_blk'], wts['c_blk'], wts['a_re'], wts['a_im'], wts['d'])
    yg = yg[:bsz].reshape(bsz * t_len, B_WIDTH)
    return (yg, fre[:bsz].reshape(bsz, S5_GROUPS, S5_STATE), fim[:bsz].reshape(bsz, S5_GROUPS, S5_STATE))


def _sb_attn_kernel(*refs, heads, n_past, p_valid, scale):
    if n_past:
        q_ref, k_ref, v_ref, pk_ref, pv_ref, o_ref, carry_scr, acc_scr = refs
    else:
        q_ref, k_ref, v_ref, o_ref, carry_scr, acc_scr = refs
    blk = ATT_BLOCK
    i = pl.program_id(1)

    def suffix_matrix(tk):
        srow = lax.broadcasted_iota(jnp.int32, (2 * tk, tk), 0)
        scol = lax.broadcasted_iota(jnp.int32, (2 * tk, tk), 1)
        return jnp.where(jnp.where(srow >= tk, srow - tk, srow) > scol, 1.0, 0.0).astype(BF16)

    suffix = {tk: suffix_matrix(tk) for tk in (blk, ATT_SLAB)}
    rows = lax.broadcasted_iota(jnp.int32, (blk, blk), 0)
    cols = lax.broadcasted_iota(jnp.int32, (blk, blk), 1)

    carry_scr[...] = jnp.zeros_like(carry_scr)
    acc_scr[...] = jnp.zeros_like(acc_scr)
    head_lanes = [slice(hh * C_HD, (hh + 1) * C_HD) for hh in range(heads)]
    n_rows = heads * blk

    def per_head(x, fn):
        return jnp.concatenate([fn(x[hh * blk:(hh + 1) * blk]) for hh in range(heads)], axis=0)

    def rows_of(ref, rj, tk, hh):
        if len(ref.shape) == 2:
            return ref[pl.ds(rj, tk), head_lanes[hh]]
        return ref[pl.ds(rj, tk), hh, :].astype(BF16)

    def visit(kr, vr, rj, specs):
        zs = []
        for s, (tk, _) in enumerate(specs):
            q_rows = slice(s * blk, (s + 1) * blk)
            zs.append(jnp.concatenate(
                [lax.dot_general(q_ref[q_rows, l], rows_of(kr, rj, tk, hh), (((1,), (1,)), ((), ())),
                                 preferred_element_type=F32) for hh, l in enumerate(head_lanes)], axis=0) * scale)
        mids = []
        for z, (tk, mask) in zip(zs, specs):
            log_b = jnp.minimum(z, 0.0) - jnp.log(1.0 + jnp.exp(-jnp.abs(z)))
            x = log_b - z
            if mask is not None:
                x = per_head(x, lambda xh, mask=mask: jnp.where(mask, xh, 0.0))
            hi = x.astype(BF16)
            lo = (x - hi.astype(F32)).astype(BF16)
            cs = jnp.dot(jnp.concatenate([hi, lo], axis=1), suffix[tk], preferred_element_type=F32)
            total = jnp.broadcast_to(cs[:, 0:1] + x[:, 0:1], (n_rows, blk))
            mids.append((log_b, cs, total))
        live = None
        for s, ((log_b, cs, total), (tk, mask)) in enumerate(zip(mids, specs)):
            carry = carry_scr[s]
            w = jnp.exp(log_b + cs + jnp.concatenate([carry] * (tk // blk), axis=1))
            if mask is not None:
                w = per_head(w, lambda wh, mask=mask: jnp.where(mask, wh, 0.0))
            w = w.astype(BF16)
            acc_scr[s] += jnp.concatenate(
                [jnp.dot(w[hh * blk:(hh + 1) * blk], rows_of(vr, rj, tk, hh),
                         preferred_element_type=F32) for hh in range(heads)], axis=0)
            carry = carry + total
            carry_scr[s] = carry
            m = jnp.max(carry)
            live = m if live is None else jnp.maximum(live, m)
        return live

    rows2 = lax.broadcasted_iota(jnp.int32, (blk, ATT_SLAB), 0)
    cols2 = lax.broadcasted_iota(jnp.int32, (blk, ATT_SLAB), 1)
    live = visit(k_ref, v_ref, pl.multiple_of(i * ATT_SLAB, ATT_SLAB),
                 [(blk, cols < rows), (ATT_SLAB, cols2 < rows2 + blk)])

    def slab_body(st):
        j, _ = st
        return j - 2, visit(k_ref, v_ref, pl.multiple_of((j - 1) * blk, blk), [(ATT_SLAB, None)] * 2)

    def alive_from(first):
        return lambda st: jnp.logical_and(st[0] >= first, st[1] > ATT_DEAD_LOG_WEIGHT)

    _, live = lax.while_loop(alive_from(1), slab_body, (2 * i - 1, live))

    if n_past:
        def past_body(st):
            j, _ = st
            return j - 1, visit(pk_ref, pv_ref, pl.multiple_of(j * blk, blk),
                                [(blk, (cols + j * blk) < p_valid)] * 2)

        lax.while_loop(alive_from(0), past_body, (jnp.int32(n_past - 1), live))

    for s in range(2):
        for hh in range(heads):
            o_ref[s * blk:(s + 1) * blk, head_lanes[hh]] = (
                acc_scr[s, hh * blk:(hh + 1) * blk, :].astype(o_ref.dtype))


def sb_attention(q, k, v, past_k, past_v, p_valid):
    bsz, t_len, width = q.shape
    n_past = 0 if past_k is None else past_k.shape[1] // ATT_BLOCK
    q_spec = pl.BlockSpec((None, ATT_SLAB, width), lambda b, i: (b, i, 0))
    kv_spec = pl.BlockSpec((None, t_len, width), lambda b, i: (b, 0, 0), pipeline_mode=pl.Buffered(1))
    in_specs = [q_spec, kv_spec, kv_spec]
    args = [q, k, v]
    if n_past:
        pb = (lambda b: b) if past_k.shape[0] == bsz else (lambda b: 0)
        p_spec = pl.BlockSpec((None,) + past_k.shape[1:], lambda b, i: (pb(b), 0, 0, 0),
                              pipeline_mode=pl.Buffered(1))
        in_specs += [p_spec, p_spec]
        args += [past_k, past_v]
    return pl.pallas_call(
        functools.partial(_sb_attn_kernel, heads=C_HEADS, n_past=n_past, p_valid=p_valid,
                          scale=C_HD ** -0.5),
        grid=(bsz, t_len // ATT_SLAB),
        in_specs=in_specs,
        out_specs=q_spec,
        out_shape=jax.ShapeDtypeStruct((bsz, t_len, width), BF16),
        scratch_shapes=[pltpu.VMEM((2, C_HEADS * ATT_BLOCK, C_HD), F32)] * 2,
        compiler_params=_cparams("parallel", "arbitrary"),
        name="sb_attention",
    )(*args)


def _pad_rows(x, mult):
    t = x.shape[1]
    tp = -(-t // mult) * mult
    return x if tp == t else jnp.pad(x, ((0, 0), (0, tp - t), (0, 0)))


def _run_stream(x, bsz, t_len, wts, hgrn_s0_t, ssm0_re, ssm0_im, past_k, past_v, p_valid, lead_kv=None):
    act = BF16 if t_len % HGRN_CHUNK == 0 else F32
    q, zf, iv, g, u = norm_matmul(
        x, wts['ln_mix'][0], wts['w_in_even'],
        [(0, A_QK, act), (A_QK, A_QK, F32), (2 * A_QK, A_WIDTH, act),
         (2 * A_QK + A_WIDTH, A_WIDTH, F32), (2 * A_QK + 2 * A_WIDTH, B_WIDTH, act)])
    o_a, s_t = hgrn(q, zf, iv, g, wts['lb'], wts['hgrn_norm'], hgrn_s0_t, bsz, t_len)
    yg, x_re, x_im = s5(u, wts['s5'], ssm0_re, ssm0_im, bsz, t_len)
    x = layer_tail(x, [o_a, yg], [wts['w_glu'], wts['w_out_even_a'], wts['w_out_even_b']],
                   wts['ln_mlp'][0], wts['w_up'][0], wts['w_down'][0], wts['ln_final'], final_norm=False)
    if lead_kv is None:
        qb, kb, vb, k32, v32 = norm_matmul(
            x, wts['ln_mix'][1], wts['w_in_odd'],
            [(0, D_MODEL, BF16), (D_MODEL, D_MODEL, BF16), (2 * D_MODEL, D_MODEL, BF16),
             (D_MODEL, D_MODEL, F32), (2 * D_MODEL, D_MODEL, F32)])
    else:
        qb, kb, vb, k32, v32 = qkv_shifted(x, wts['ln_mix'][1], wts['w_in_odd'], *lead_kv, bsz, t_len)
    shp = (bsz, t_len, D_MODEL)
    o = sb_attention(_pad_rows(qb.reshape(shp), ATT_SLAB), _pad_rows(kb.reshape(shp), ATT_SLAB),
                     _pad_rows(vb.reshape(shp), ATT_SLAB), past_k, past_v, p_valid)
    o = o[:, :t_len].reshape(bsz * t_len, D_MODEL)
    y = layer_tail(x, [o], [wts['w_out_odd']],
                   wts['ln_mlp'][1], wts['w_up'][1], wts['w_down'][1], wts['ln_final'], final_norm=True)
    return y, s_t, x_re, x_im, k32, v32


def kernel(x_prompt, x_sample, state_hgrn, state_ssm_re, state_ssm_im, cache_k, cache_v, meta_tokens,
           ln_mix, ln_mlp, ln_final, w_in_even, hgrn_lb, hgrn_norm, ssm_a_re, ssm_a_im, ssm_log_dt,
           ssm_b_re, ssm_b_im, ssm_c_re, ssm_c_im, ssm_d, w_glu, w_out_even, w_in_odd, w_out_odd,
           w_up, w_down):
    bsz, seq, _ = x_prompt.shape
    dbsz, dseq, _ = x_sample.shape
    past = cache_k.shape[2]
    lb_all = jnp.cumsum(jax.nn.softmax(hgrn_lb.astype(F32), axis=0), axis=0)
    w_out_e = w_out_even[0].astype(BF16)
    wts = dict(
        ln_mix=ln_mix.astype(F32), ln_mlp=ln_mlp.astype(F32), ln_final=ln_final.astype(F32),
        w_in_even=w_in_even[0].astype(BF16), lb=lb_all[0], hgrn_norm=hgrn_norm[0].astype(F32),
        s5=s5_weights(ssm_a_re[0], ssm_a_im[0], ssm_log_dt[0], ssm_b_re[0], ssm_b_im[0],
                      ssm_c_re[0], ssm_c_im[0], ssm_d[0]),
        w_glu=w_glu[0].astype(BF16), w_out_even_a=w_out_e[:A_WIDTH], w_out_even_b=w_out_e[A_WIDTH:],
        w_in_odd=w_in_odd[0].astype(BF16), w_out_odd=w_out_odd[0].astype(BF16),
        w_up=w_up.astype(BF16), w_down=w_down.astype(BF16))

    zeros_s = jnp.zeros((1, A_HEADS, A_DV, A_DK), F32)
    zeros_x = jnp.zeros((1, S5_GROUPS, S5_STATE), F32)
    _, m_s, m_re, m_im, m_k, m_v = _run_stream(
        meta_tokens.astype(F32), 1, N_META, wts, zeros_s, zeros_x, zeros_x, None, None, 0)

    m_kp = _pad_rows(m_k[None], ATT_BLOCK).reshape(1, -1, C_HEADS, C_HD)
    m_vp = _pad_rows(m_v[None], ATT_BLOCK).reshape(1, -1, C_HEADS, C_HD)
    y_p, s_p, re_p, im_p, k_p, v_p = _run_stream(
        x_prompt.reshape(bsz * seq, D_MODEL), bsz, seq, wts, m_s,
        jnp.broadcast_to(m_re, (bsz, S5_GROUPS, S5_STATE)), jnp.broadcast_to(m_im, (bsz, S5_GROUPS, S5_STATE)),
        m_kp, m_vp, N_META, lead_kv=(m_k, m_v))

    pad_past = ((0, 0), (0, -past % ATT_BLOCK), (0, 0), (0, 0))
    y_s, s_s, re_s, im_s, k_s, v_s = _run_stream(
        x_sample.reshape(dbsz * dseq, D_MODEL), dbsz, dseq, wts,
        jnp.swapaxes(state_hgrn[0].astype(F32), -1, -2), state_ssm_re[0], state_ssm_im[0],
        jnp.pad(cache_k[0].astype(F32), pad_past), jnp.pad(cache_v[0].astype(F32), pad_past), past)

    kv_shape = (1, bsz, N_META + seq, C_HEADS, C_HD)
    return (y_p.reshape(bsz, seq, D_MODEL), y_s.reshape(dbsz, dseq, D_MODEL),
            jnp.swapaxes(s_p, -1, -2)[None], jnp.swapaxes(s_s, -1, -2)[None],
            re_p[None], im_p[None], re_s[None], im_s[None],
            k_p.reshape(kv_shape), v_p.reshape(kv_shape),
            k_s.reshape(1, dbsz, dseq, C_HEADS, C_HD), v_s.reshape(1, dbsz, dseq, C_HEADS, C_HD))
```

```python
import functools
import math

import numpy as np
import jax
import jax.numpy as jnp
from jax import lax
from jax.experimental import pallas as pl
from jax.experimental.pallas import tpu as pltpu

F32 = jnp.float32
BF16 = jnp.bfloat16

D_MODEL = 1024
N_META = 16
A_HEADS = 4
A_DK = 128
A_DV = 128
A_WIDTH = 512
A_QK = A_HEADS * A_DK
B_WIDTH = 512
S5_GROUP = 16
S5_GROUPS = 32
S5_STATE = 64
S5_NSTATE = S5_GROUPS * S5_STATE
C_HEADS = 8
C_HD = 128
D_FF = 4 * D_MODEL
EPS = 1e-6

SUBLANES = 8
LANES = 128
VMEM_LIMIT_BYTES = 48 * 1024 * 1024

HGRN_CHUNK = 64
HGRN_STEP_ROWS = 512
LAYER_TAIL_SUB_ROWS = 256
S5_STEP_T = 64
S5_SCAN_BLOCKS = 2
ATT_BLOCK = 128
ATT_SLAB = 256
ATT_DEAD_LOG_WEIGHT = -104.0
NEG_BIG = -1e30


def _cparams(*sem):
    return pltpu.CompilerParams(dimension_semantics=sem, vmem_limit_bytes=VMEM_LIMIT_BYTES)


def _sigmoid(x):
    return 1.0 / (1.0 + jnp.exp(-x))


def _rms_scale(x):
    return lax.rsqrt(jnp.mean(x * x, axis=-1, keepdims=True) + EPS)


def _norm_matmul_kernel(x_ref, g_ref, w_ref, *out_refs, cols, nchunk):
    x = x_ref[...]
    h = (x * _rms_scale(x) * g_ref[...]).astype(BF16)
    for c0, width in sorted(set(cols)):
        for c in range(0, width, nchunk):
            res = jnp.dot(h, w_ref[:, c0 + c:c0 + c + nchunk], preferred_element_type=F32)
            for o_ref, col in zip(out_refs, cols):
                if col == (c0, width):
                    o_ref[:, c:c + nchunk] = res.astype(o_ref.dtype)


def norm_matmul(x, g, w_bf16, outs, *, tm=512, nchunk=512):
    m, d = x.shape
    tm = min(tm, m)
    kern = functools.partial(_norm_matmul_kernel, cols=[(c0, wd) for c0, wd, _ in outs], nchunk=nchunk)
    return pl.pallas_call(
        kern,
        grid=(pl.cdiv(m, tm),),
        in_specs=[pl.BlockSpec((tm, d), lambda i: (i, 0)),
                  pl.BlockSpec((1, d), lambda i: (0, 0)),
                  pl.BlockSpec(w_bf16.shape, lambda i: (0, 0))],
        out_specs=[pl.BlockSpec((tm, wd), lambda i: (i, 0)) for _, wd, _ in outs],
        out_shape=[jax.ShapeDtypeStruct((m, wd), dt) for _, wd, dt in outs],
        compiler_params=_cparams("parallel"),
        name="norm_matmul",
    )(x, g.reshape(1, d), w_bf16)


def _qkv_shifted_kernel(x_ref, g_ref, w_ref, lead_ref, qb_ref, kb_ref, vb_ref, kf_ref, vf_ref, carry_scr,
                        *, n_lead, nchunk):
    t = pl.program_id(1)
    nt = pl.num_programs(1) - 1
    tm, width = qb_ref.shape

    @pl.when(t == 0)
    def _():
        carry_scr[...] = lead_ref[...]

    for n, f_ref in enumerate((kf_ref, vf_ref)):
        f_ref[:n_lead, :] = carry_scr[n]

    @pl.when(t < nt)
    def _():
        x = x_ref[...]
        h = (x * _rms_scale(x) * g_ref[...]).astype(BF16)
        for n, (b_ref, f_ref) in enumerate(((qb_ref, None), (kb_ref, kf_ref), (vb_ref, vf_ref))):
            for c in range(0, width, nchunk):
                res = jnp.dot(h, w_ref[:, n * width + c:n * width + c + nchunk], preferred_element_type=F32)
                b_ref[:, c:c + nchunk] = res.astype(b_ref.dtype)
                if f_ref is not None:
                    f_ref[n_lead:, c:c + nchunk] = res[:tm - n_lead]
                    carry_scr[n - 1, :, c:c + nchunk] = res[tm - n_lead:]


def qkv_shifted(x, g, w_bf16, lead_k, lead_v, bsz, t_len, *, tm=512, nchunk=512):
    m, d = x.shape
    width = w_bf16.shape[1] // 3
    n_lead = lead_k.shape[0]
    tm = math.gcd(tm, t_len)
    nt = t_len // tm
    tok = lambda b, t: (b * nt + jnp.minimum(t, nt - 1), 0)
    return pl.pallas_call(
        functools.partial(_qkv_shifted_kernel, n_lead=n_lead, nchunk=nchunk),
        grid=(bsz, nt + 1),
        in_specs=[pl.BlockSpec((tm, d), tok),
                  pl.BlockSpec((1, d), lambda b, t: (0, 0)),
                  pl.BlockSpec(w_bf16.shape, lambda b, t: (0, 0)),
                  pl.BlockSpec((2, n_lead, width), lambda b, t: (0, 0, 0))],
        out_specs=[pl.BlockSpec((tm, width), tok)] * 3
                  + [pl.BlockSpec((None, tm, width), lambda b, t: (b, t, 0))] * 2,
        out_shape=[jax.ShapeDtypeStruct((m, width), BF16)] * 3
                  + [jax.ShapeDtypeStruct((bsz, n_lead + t_len, width), F32)] * 2,
        scratch_shapes=[pltpu.VMEM((2, n_lead, width), F32)],
        compiler_params=_cparams("parallel", "arbitrary"),
        name="qkv_shifted",
    )(x, g.reshape(1, d), w_bf16, jnp.stack([lead_k, lead_v]))


def _layer_tail_kernel(*refs, n_mix, final_norm):
    x_ref = refs[0]
    mix = refs[1:1 + n_mix]
    g_ref, wup_ref, wdn_ref, gf_ref, o_ref, h_scr = refs[1 + n_mix:]
    j = pl.program_id(1)

    @pl.when(j == 0)
    def _():
        tm = x_ref.shape[0]
        sub = math.gcd(tm, LAYER_TAIL_SUB_ROWS)
        for r0 in range(0, tm, sub):
            rs = slice(r0, r0 + sub)
            if n_mix == 5:
                oa_ref, yg_ref, wglu_ref, wa_ref, wb_ref = mix
                yg = yg_ref[rs, :]
                gate = _sigmoid(jnp.dot(yg.astype(BF16), wglu_ref[...], preferred_element_type=F32))
                ob = (yg.astype(F32) * gate).astype(BF16)
                x1 = (x_ref[rs, :] + jnp.dot(oa_ref[rs, :], wa_ref[...], preferred_element_type=F32)
                      + jnp.dot(ob, wb_ref[...], preferred_element_type=F32))
            else:
                om_ref, wo_ref = mix
                x1 = x_ref[rs, :] + jnp.dot(om_ref[rs, :], wo_ref[...], preferred_element_type=F32)
            h_scr[rs, :] = (x1 * _rms_scale(x1) * g_ref[...]).astype(BF16)
            o_ref[rs, :] = x1

    a = jnp.dot(h_scr[...], wup_ref[...], preferred_element_type=F32)
    a = jnp.square(jnp.maximum(a, 0.0)).astype(BF16)
    o_ref[...] += jnp.dot(a, wdn_ref[...], preferred_element_type=F32)

    if final_norm:
        @pl.when(j == pl.num_programs(1) - 1)
        def _():
            y = o_ref[...]
            o_ref[...] = y * _rms_scale(y) * gf_ref[...]


def layer_tail(x, mix_rows, mix_weights, g, w_up, w_down, g_final, *, final_norm, tm=1024, tf=1024):
    m, d = x.shape
    tm = min(tm, m)
    ff = w_up.shape[1]
    row = lambda i, j: (i, 0)
    fixed = lambda i, j: (0, 0)
    once = dict(pipeline_mode=pl.Buffered(1))
    return pl.pallas_call(
        functools.partial(_layer_tail_kernel, n_mix=len(mix_rows) + len(mix_weights), final_norm=final_norm),
        grid=(pl.cdiv(m, tm), ff // tf),
        in_specs=([pl.BlockSpec((tm, d), row)]
                  + [pl.BlockSpec((tm, r.shape[1]), row) for r in mix_rows]
                  + [pl.BlockSpec(w.shape, fixed, **once) for w in mix_weights]
                  + [pl.BlockSpec((1, d), fixed),
                     pl.BlockSpec((d, tf), lambda i, j: (0, j)),
                     pl.BlockSpec((tf, d), lambda i, j: (j, 0)),
                     pl.BlockSpec((1, d), fixed)]),
        out_specs=pl.BlockSpec((tm, d), row),
        out_shape=jax.ShapeDtypeStruct((m, d), F32),
        scratch_shapes=[pltpu.VMEM((tm, d), BF16)],
        compiler_params=_cparams("parallel", "arbitrary"),
        name="layer_tail",
    )(x, *mix_rows, *mix_weights, g.reshape(1, d), w_up, w_down, g_final.reshape(1, d))


def _split3(x):
    hi = x.astype(BF16)
    r1 = x - hi.astype(F32)
    mid = r1.astype(BF16)
    lo = (r1 - mid.astype(F32)).astype(BF16)
    return hi, mid, lo


def _hgrn_offdiag_mask(chunk):
    n_blk = chunk // SUBLANES
    seg = np.concatenate([np.full(SUBLANES * i, i) for i in range(1, n_blk)])
    blk = np.repeat(np.arange(1, n_blk), SUBLANES)
    return (blk[:, None] == seg[None, :]).astype(np.float32)


def _hgrn_kernel(*refs, chunk, n_sub):
    n_blk = chunk // SUBLANES
    row_scr = [refs[len(refs) - 3 * (n_sub - u):len(refs) - 3 * (n_sub - u - 1)] for u in range(n_sub)]
    refs = refs[:-3 * n_sub]
    if n_blk > 1:
        q_ref, zf_ref, iv_ref, g_ref, lb_ref, gn_ref, s0_ref, mask_ref, o_ref, sout_ref, s_scr = refs
    else:
        q_ref, zf_ref, iv_ref, g_ref, lb_ref, gn_ref, s0_ref, o_ref, sout_ref, s_scr = refs
    step = pl.program_id(1)

    @pl.when(step == 0)
    def _():
        s_scr[...] = s0_ref[...]

    lb = lb_ref[...]
    gn = gn_ref[...]
    rows = lax.broadcasted_iota(jnp.int32, (chunk, chunk), 0)
    cols = lax.broadcasted_iota(jnp.int32, (chunk, chunk), 1)
    tri = jnp.where(rows >= cols, 1.0, 0.0).astype(BF16)
    hrow = lax.broadcasted_iota(jnp.int32, (A_QK, A_QK), 0) // A_DK
    hcol = lax.broadcasted_iota(jnp.int32, (A_QK, A_QK), 1) // A_DK
    head_ones = jnp.where(hrow == hcol, 1.0, 0.0).astype(BF16)
    sub = lax.broadcasted_iota(jnp.int32, (SUBLANES, A_QK), 0)
    head_lanes = [slice(hh * A_DK, (hh + 1) * A_DK) for hh in range(A_HEADS)]
    nt = (((1,), (1,)), ((), ()))
    tn = (((0,), (0,)), ((), ()))

    def pairwise_phase(cc, b_scr, k_scr, iv_scr):
        r0 = cc * chunk
        q = q_ref[pl.ds(r0, chunk), :].astype(F32)
        zf = zf_ref[pl.ds(r0, chunk), :]
        iv = iv_ref[pl.ds(r0, chunk), :].astype(F32)
        g = g_ref[pl.ds(r0, chunk), :]
        f = lb + (1.0 - lb) * _sigmoid(zf)
        logf = jnp.log(f)
        kk = 1.0 - f
        hi, mid, lo = _split3(logf)
        b = (jnp.dot(tri, hi, preferred_element_type=F32)
             + jnp.dot(tri, mid, preferred_element_type=F32)
             + jnp.dot(tri, lo, preferred_element_type=F32))
        b_scr[...] = b
        k_scr[...] = kk
        iv_scr[...] = iv

        ws = []
        for i in range(n_blk):
            b_i = b[SUBLANES * i:SUBLANES * (i + 1)]
            q_i = q[SUBLANES * i:SUBLANES * (i + 1)]
            for s in range(SUBLANES):
                row = SUBLANES * i + s
                dlt = jnp.where(sub >= s, b_i - b_scr[row:row + 1, :], NEG_BIG)
                ws.append(jnp.exp(dlt) * q_i * k_scr[row:row + 1, :])
        w = jnp.concatenate(ws, axis=0).astype(BF16)
        return r0, q, iv, g, kk, b, w, b_scr, iv_scr

    def matmul_phase(r0, q, iv, g, kk, b, w, b_scr, iv_scr):
        att = jnp.dot(w, head_ones, preferred_element_type=F32)
        o_blocks = []
        for i in range(n_blk):
            acc = jnp.zeros((SUBLANES, A_WIDTH), F32)
            for s in range(SUBLANES):
                row = SUBLANES * i + s
                acc = acc + att[SUBLANES * row:SUBLANES * (row + 1)] * iv_scr[row:row + 1, :]
            o_blocks.append(acc)
        o = jnp.concatenate(o_blocks, axis=0)

        if n_blk > 1:
            qt, kh, ivs = [], [], []
            for i in range(1, n_blk):
                n_s = SUBLANES * i
                r_i = b_scr[n_s - 1:n_s, :]
                qt.append(q[n_s:n_s + SUBLANES] * jnp.exp(b[n_s:n_s + SUBLANES] - r_i))
                kh.append(kk[:n_s] * jnp.exp(r_i - b[:n_s]))
                ivs.append(iv[:n_s])
            qt = jnp.concatenate(qt, axis=0).astype(BF16)
            kh = jnp.concatenate(kh, axis=0).astype(BF16)
            ivs = jnp.concatenate(ivs, axis=0).astype(BF16)
            mask = mask_ref[...]
            o_off = []
            for l in head_lanes:
                a = lax.dot_general(qt[:, l], kh[:, l], nt, preferred_element_type=F32)
                o_off.append(jnp.dot((a * mask).astype(BF16), ivs[:, l], preferred_element_type=F32))
            o_off = jnp.concatenate(o_off, axis=1)
            o = o + jnp.concatenate([jnp.zeros((SUBLANES, A_WIDTH), F32), o_off], axis=0)

        qh = (q * jnp.exp(b)).astype(BF16)
        b_last = b[chunk - 1:chunk, :]
        kd = (kk * jnp.exp(b_last - b)).astype(BF16)
        dec = jnp.exp(b_last)
        ivb = iv.astype(BF16)
        o_inter = []
        for hh, l in enumerate(head_lanes):
            s_t = s_scr[hh]
            o_inter.append(lax.dot_general(qh[:, l], s_t.astype(BF16), nt, preferred_element_type=F32))
            upd = lax.dot_general(ivb[:, l], kd[:, l], tn, preferred_element_type=F32)
            s_scr[hh] = s_t * dec[:, l] + upd
        o = o + jnp.concatenate(o_inter, axis=1)

        o = jnp.concatenate([o[:, l] * _rms_scale(o[:, l]) * gn for l in head_lanes], axis=1)
        o = o * (g * _sigmoid(g))
        o_ref[pl.ds(r0, chunk), :] = o.astype(o_ref.dtype)

    pending = None
    for u in range(n_sub):
        ctx = pairwise_phase(u, *row_scr[u])
        if pending is not None:
            matmul_phase(*pending)
        pending = ctx
    matmul_phase(*pending)

    @pl.when(step == pl.num_programs(1) - 1)
    def _():
        sout_ref[...] = s_scr[...]


def hgrn(q, zf, iv, g, lb, gnorm, s0_t, bsz, t_len):
    chunk = HGRN_CHUNK if t_len % HGRN_CHUNK == 0 else SUBLANES
    step_rows = math.gcd(HGRN_STEP_ROWS, t_len)
    assert step_rows % chunk == 0, (t_len, step_rows, chunk)
    n_steps = t_len // step_rows
    tok = pl.BlockSpec((step_rows, A_QK), lambda b, c: (b * n_steps + c, 0))
    s0_b = (lambda b: b) if s0_t.shape[0] == bsz else (lambda b: 0)
    in_specs = [tok, tok, tok, tok,
                pl.BlockSpec((1, A_QK), lambda b, c: (0, 0)),
                pl.BlockSpec((1, A_DV), lambda b, c: (0, 0)),
                pl.BlockSpec((None, A_HEADS, A_DV, A_DK), lambda b, c: (s0_b(b), 0, 0, 0))]
    args = [q, zf, iv, g, lb.reshape(1, A_QK), gnorm.reshape(1, A_DV), s0_t]
    if chunk > SUBLANES:
        mask = jnp.asarray(_hgrn_offdiag_mask(chunk))
        in_specs.append(pl.BlockSpec(mask.shape, lambda b, c: (0, 0)))
        args.append(mask)
    return pl.pallas_call(
        functools.partial(_hgrn_kernel, chunk=chunk, n_sub=step_rows // chunk),
        grid=(bsz, n_steps),
        in_specs=in_specs,
        out_specs=[tok, pl.BlockSpec((None, A_HEADS, A_DV, A_DK), lambda b, c: (b, 0, 0, 0))],
        out_shape=[jax.ShapeDtypeStruct((bsz * t_len, A_WIDTH), BF16),
                   jax.ShapeDtypeStruct((bsz, A_HEADS, A_DV, A_DK), F32)],
        scratch_shapes=([pltpu.VMEM((A_HEADS, A_DV, A_DK), F32)]
                        + [pltpu.VMEM((chunk, A_QK), F32)] * (3 * (step_rows // chunk))),
        compiler_params=_cparams("parallel", "arbitrary"),
        name="hgrn",
    )(*args)


def s5_weights(a_re, a_im, log_dt, b_re, b_im, c_re, c_im, d_skip):
    f32 = F32
    ar = a_re.astype(f32)
    ai = a_im.astype(f32)
    dt = jnp.exp(log_dt.astype(f32))[:, None]
    mag = jnp.exp(dt * ar)
    abar_re = mag * jnp.cos(dt * ai)
    abar_im = mag * jnp.sin(dt * ai)
    den = ar * ar + ai * ai
    zr = ((abar_re - 1.0) * ar + abar_im * ai) / den
    zi = (abar_im * ar - (abar_re - 1.0) * ai) / den
    br, bi = b_re.astype(f32), b_im.astype(f32)
    bb_re = zr[..., None] * br - zi[..., None] * bi
    bb_im = zr[..., None] * bi + zi[..., None] * br
    eye = jnp.eye(S5_GROUPS, dtype=f32)

    def in_proj(bb):
        return jnp.einsum('gnp,gh->gphn', bb, eye).reshape(B_WIDTH, S5_NSTATE)

    def out_proj(c):
        return jnp.einsum('gpn,gh->gnhp', c, eye).reshape(S5_NSTATE, B_WIDTH)

    n_sb = B_WIDTH // LANES
    sw = S5_NSTATE // n_sb

    def diag_blocks(full, rows, cols):
        return jnp.stack([full[c * rows:(c + 1) * rows, c * cols:(c + 1) * cols] for c in range(n_sb)])

    b_blk = jnp.concatenate([diag_blocks(in_proj(bb_re), LANES, sw), diag_blocks(in_proj(bb_im), LANES, sw)],
                            axis=2).astype(BF16)
    c_blk = jnp.concatenate([diag_blocks(out_proj(c_re.astype(f32)), sw, LANES),
                             -diag_blocks(out_proj(c_im.astype(f32)), sw, LANES)],
                            axis=1).astype(BF16)
    return dict(b_blk=b_blk, c_blk=c_blk, a_re=abar_re.reshape(1, S5_NSTATE),
                a_im=abar_im.reshape(1, S5_NSTATE), d=d_skip.astype(f32).reshape(1, B_WIDTH))


def _s5_kernel(u_ref, x0re_ref, x0im_ref, bblk_ref, cblk_ref, are_ref, aim_ref, d_ref,
               y_ref, fre_ref, fim_ref, u_scr, bu_scr, xre_scr, xim_scr, *, tt, nb):
    step = pl.program_id(0)

    @pl.when(step == 0)
    def _():
        xre_scr[...] = x0re_ref[...]
        xim_scr[...] = x0im_ref[...]

    n_lt = B_WIDTH // LANES
    u_bt = u_ref[...].reshape(nb * tt, B_WIDTH).astype(F32)
    for c in range(n_lt):
        u_scr[c] = u_bt[:, c * LANES:(c + 1) * LANES]
    sw = S5_NSTATE // n_lt
    skips = []
    for c in range(n_lt):
        u_tb = jnp.concatenate([u_scr[c, pl.ds(t, nb, stride=tt), :] for t in range(tt)], axis=0)
        bu_scr[c] = jnp.dot(u_tb.astype(BF16), bblk_ref[c], preferred_element_type=F32)
        skips.append(d_ref[:, c * LANES:(c + 1) * LANES] * u_tb)

    for c0 in range(0, n_lt, S5_SCAN_BLOCKS):
        blocks = range(c0, c0 + S5_SCAN_BLOCKS)
        st_ls = [slice(c * sw, (c + 1) * sw) for c in blocks]
        a_res = [jnp.broadcast_to(are_ref[:, l], (nb, sw)) for l in st_ls]
        a_ims = [jnp.broadcast_to(aim_ref[:, l], (nb, sw)) for l in st_ls]

        def body(t, st, blocks=blocks, a_res=a_res, a_ims=a_ims):
            r0 = pl.multiple_of(t * nb, nb)
            out = []
            for c, a_re, a_im, (x_re, x_im) in zip(blocks, a_res, a_ims, st):
                n_re = a_re * x_re - a_im * x_im + bu_scr[c, pl.ds(r0, nb), :sw]
                n_im = a_re * x_im + a_im * x_re + bu_scr[c, pl.ds(r0, nb), sw:]
                bu_scr[c, pl.ds(r0, nb), :sw] = n_re
                bu_scr[c, pl.ds(r0, nb), sw:] = n_im
                out.append((n_re, n_im))
            return tuple(out)

        fin = lax.fori_loop(0, tt, body, tuple((xre_scr[:, l], xim_scr[:, l]) for l in st_ls))
        for l, (x_re, x_im) in zip(st_ls, fin):
            xre_scr[:, l] = x_re
            xim_scr[:, l] = x_im

    for c in range(n_lt):
        y = jnp.dot(bu_scr[c].astype(BF16), cblk_ref[c], preferred_element_type=F32) + skips[c]
        y = 0.5 * y * (1.0 + lax.erf(y * (1.0 / math.sqrt(2.0))))
        for t in range(tt):
            u_scr[c, pl.ds(t, nb, stride=tt), :] = y[nb * t:nb * (t + 1)]
    y_bt = jnp.concatenate([u_scr[c] for c in range(n_lt)], axis=1).reshape(nb, tt, B_WIDTH)
    y_ref[...] = y_bt.astype(y_ref.dtype)

    @pl.when(step == pl.num_programs(0) - 1)
    def _():
        fre_ref[...] = xre_scr[...]
        fim_ref[...] = xim_scr[...]


def s5(u, wts, x0_re, x0_im, bsz, t_len):
    nb = SUBLANES
    tt = min(S5_STEP_T, t_len)
    u3 = u.reshape(bsz, t_len, B_WIDTH)
    x0 = [x.astype(F32).reshape(bsz, S5_NSTATE) for x in (x0_re, x0_im)]
    if bsz < nb:
        u3 = jnp.pad(u3, ((0, nb - bsz), (0, 0), (0, 0)))
        x0 = [jnp.pad(x, ((0, nb - bsz), (0, 0))) for x in x0]
    fixed = lambda i: (0, 0)
    fixed3 = lambda i: (0, 0, 0)
    n_sb = B_WIDTH // LANES
    sw2 = 2 * S5_NSTATE // n_sb
    tok = pl.BlockSpec((nb, tt, B_WIDTH), lambda i: (0, i, 0))
    state = pl.BlockSpec((nb, S5_NSTATE), fixed)
    yg, fre, fim = pl.pallas_call(
        functools.partial(_s5_kernel, tt=tt, nb=nb),
        grid=(t_len // tt,),
        in_specs=[tok, state, state,
                  pl.BlockSpec((n_sb, LANES, sw2), fixed3), pl.BlockSpec((n_sb, sw2, LANES), fixed3),
                  pl.BlockSpec((1, S5_NSTATE), fixed), pl.BlockSpec((1, S5_NSTATE), fixed),
                  pl.BlockSpec((1, B_WIDTH), fixed)],
        out_specs=[tok, state, state],
        out_shape=[jax.ShapeDtypeStruct((nb, t_len, B_WIDTH), u.dtype),
                   jax.ShapeDtypeStruct((nb, S5_NSTATE), F32), jax.ShapeDtypeStruct((nb, S5_NSTATE), F32)],
        scratch_shapes=[pltpu.VMEM((n_sb, nb * tt, LANES), F32),
                        pltpu.VMEM((n_sb, nb * tt, sw2), F32),
                        pltpu.VMEM((nb, S5_NSTATE), F32), pltpu.VMEM((nb, S5_NSTATE), F32)],
        compiler_params=_cparams("arbitrary"),
        name="s5",
    )(u3, x0[0], x0[1], wts['b_blk'], wts['c_blk'], wts['a_re'], wts['a_im'], wts['d'])
    yg = yg[:bsz].reshape(bsz * t_len, B_WIDTH)
    return (yg, fre[:bsz].reshape(bsz, S5_GROUPS, S5_STATE), fim[:bsz].reshape(bsz, S5_GROUPS, S5_STATE))


def _sb_attn_kernel(*refs, heads, n_past, p_valid, scale):
    if n_past:
        q_ref, k_ref, v_ref, pk_ref, pv_ref, o_ref, carry_scr, acc_scr = refs
    else:
        q_ref, k_ref, v_ref, o_ref, carry_scr, acc_scr = refs
    blk = ATT_BLOCK
    i = pl.program_id(1)

    def suffix_matrix(tk):
        srow = lax.broadcasted_iota(jnp.int32, (2 * tk, tk), 0)
        scol = lax.broadcasted_iota(jnp.int32, (2 * tk, tk), 1)
        return jnp.where(jnp.where(srow >= tk, srow - tk, srow) > scol, 1.0, 0.0).astype(BF16)

    suffix = {tk: suffix_matrix(tk) for tk in (blk, ATT_SLAB)}
    rows = lax.broadcasted_iota(jnp.int32, (blk, blk), 0)
    cols = lax.broadcasted_iota(jnp.int32, (blk, blk), 1)

    carry_scr[...] = jnp.zeros_like(carry_scr)
    acc_scr[...] = jnp.zeros_like(acc_scr)
    head_lanes = [slice(hh * C_HD, (hh + 1) * C_HD) for hh in range(heads)]
    n_rows = heads * blk

    def per_head(x, fn):
        return jnp.concatenate([fn(x[hh * blk:(hh + 1) * blk]) for hh in range(heads)], axis=0)

    def rows_of(ref, rj, tk, hh):
        if len(ref.shape) == 2:
            return ref[pl.ds(rj, tk), head_lanes[hh]]
        return ref[pl.ds(rj, tk), hh, :].astype(BF16)

    def visit(kr, vr, rj, specs):
        zs = []
        for s, (tk, _) in enumerate(specs):
            q_rows = slice(s * blk, (s + 1) * blk)
            zs.append(jnp.concatenate(
                [lax.dot_general(q_ref[q_rows, l], rows_of(kr, rj, tk, hh), (((1,), (1,)), ((), ())),
                                 preferred_element_type=F32) for hh, l in enumerate(head_lanes)], axis=0) * scale)
        mids = []
        for z, (tk, mask) in zip(zs, specs):
            log_b = jnp.minimum(z, 0.0) - jnp.log(1.0 + jnp.exp(-jnp.abs(z)))
            x = log_b - z
            if mask is not None:
                x = per_head(x, lambda xh, mask=mask: jnp.where(mask, xh, 0.0))
            hi = x.astype(BF16)
            lo = (x - hi.astype(F32)).astype(BF16)
            cs = jnp.dot(jnp.concatenate([hi, lo], axis=1), suffix[tk], preferred_element_type=F32)
            total = jnp.broadcast_to(cs[:, 0:1] + x[:, 0:1], (n_rows, blk))
            mids.append((log_b, cs, total))
        live = None
        for s, ((log_b, cs, total), (tk, mask)) in enumerate(zip(mids, specs)):
            carry = carry_scr[s]
            w = jnp.exp(log_b + cs + jnp.concatenate([carry] * (tk // blk), axis=1))
            if mask is not None:
                w = per_head(w, lambda wh, mask=mask: jnp.where(mask, wh, 0.0))
            w = w.astype(BF16)
            acc_scr[s] += jnp.concatenate(
                [jnp.dot(w[hh * blk:(hh + 1) * blk], rows_of(vr, rj, tk, hh),
                         preferred_element_type=F32) for hh in range(heads)], axis=0)
            carry = carry + total
            carry_scr[s] = carry
            m = jnp.max(carry)
            live = m if live is None else jnp.maximum(live, m)
        return live

    rows2 = lax.broadcasted_iota(jnp.int32, (blk, ATT_SLAB), 0)
    cols2 = lax.broadcasted_iota(jnp.int32, (blk, ATT_SLAB), 1)
    live = visit(k_ref, v_ref, pl.multiple_of(i * ATT_SLAB, ATT_SLAB),
                 [(blk, cols < rows), (ATT_SLAB, cols2 < rows2 + blk)])

    def slab_body(st):
        j, _ = st
        return j - 2, visit(k_ref, v_ref, pl.multiple_of((j - 1) * blk, blk), [(ATT_SLAB, None)] * 2)

    def alive_from(first):
        return lambda st: jnp.logical_and(st[0] >= first, st[1] > ATT_DEAD_LOG_WEIGHT)

    _, live = lax.while_loop(alive_from(1), slab_body, (2 * i - 1, live))

    if n_past:
        def past_body(st):
            j, _ = st
            return j - 1, visit(pk_ref, pv_ref, pl.multiple_of(j * blk, blk),
                                [(blk, (cols + j * blk) < p_valid)] * 2)

        lax.while_loop(alive_from(0), past_body, (jnp.int32(n_past - 1), live))

    for s in range(2):
        for hh in range(heads):
            o_ref[s * blk:(s + 1) * blk, head_lanes[hh]] = (
                acc_scr[s, hh * blk:(hh + 1) * blk, :].astype(o_ref.dtype))


def sb_attention(q, k, v, past_k, past_v, p_valid):
    bsz, t_len, width = q.shape
    n_past = 0 if past_k is None else past_k.shape[1] // ATT_BLOCK
    q_spec = pl.BlockSpec((None, ATT_SLAB, width), lambda b, i: (b, i, 0))
    kv_spec = pl.BlockSpec((None, t_len, width), lambda b, i: (b, 0, 0), pipeline_mode=pl.Buffered(1))
    in_specs = [q_spec, kv_spec, kv_spec]
    args = [q, k, v]
    if n_past:
        pb = (lambda b: b) if past_k.shape[0] == bsz else (lambda b: 0)
        p_spec = pl.BlockSpec((None,) + past_k.shape[1:], lambda b, i: (pb(b), 0, 0, 0))
        in_specs += [p_spec, p_spec]
        args += [past_k, past_v]
    return pl.pallas_call(
        functools.partial(_sb_attn_kernel, heads=C_HEADS, n_past=n_past, p_valid=p_valid,
                          scale=C_HD ** -0.5),
        grid=(bsz, t_len // ATT_SLAB),
        in_specs=in_specs,
        out_specs=q_spec,
        out_shape=jax.ShapeDtypeStruct((bsz, t_len, width), BF16),
        scratch_shapes=[pltpu.VMEM((2, C_HEADS * ATT_BLOCK, C_HD), F32)] * 2,
        compiler_params=_cparams("parallel", "arbitrary"),
        name="sb_attention",
    )(*args)


def _pad_rows(x, mult):
    t = x.shape[1]
    tp = -(-t // mult) * mult
    return x if tp == t else jnp.pad(x, ((0, 0), (0, tp - t), (0, 0)))


def _run_stream(x, bsz, t_len, wts, hgrn_s0_t, ssm0_re, ssm0_im, past_k, past_v, p_valid, lead_kv=None):
    act = BF16 if t_len % HGRN_CHUNK == 0 else F32
    q, zf, iv, g, u = norm_matmul(
        x, wts['ln_mix'][0], wts['w_in_even'],
        [(0, A_QK, act), (A_QK, A_QK, F32), (2 * A_QK, A_WIDTH, act),
         (2 * A_QK + A_WIDTH, A_WIDTH, F32), (2 * A_QK + 2 * A_WIDTH, B_WIDTH, act)], tm=1024)
    o_a, s_t = hgrn(q, zf, iv, g, wts['lb'], wts['hgrn_norm'], hgrn_s0_t, bsz, t_len)
    yg, x_re, x_im = s5(u, wts['s5'], ssm0_re, ssm0_im, bsz, t_len)
    x = layer_tail(x, [o_a, yg], [wts['w_glu'], wts['w_out_even_a'], wts['w_out_even_b']],
                   wts['ln_mlp'][0], wts['w_up'][0], wts['w_down'][0], wts['ln_final'], final_norm=False)
    if lead_kv is None:
        qb, kb, vb, k32, v32 = norm_matmul(
            x, wts['ln_mix'][1], wts['w_in_odd'],
            [(0, D_MODEL, BF16), (D_MODEL, D_MODEL, BF16), (2 * D_MODEL, D_MODEL, BF16),
             (D_MODEL, D_MODEL, F32), (2 * D_MODEL, D_MODEL, F32)])
    else:
        qb, kb, vb, k32, v32 = qkv_shifted(x, wts['ln_mix'][1], wts['w_in_odd'], *lead_kv, bsz, t_len)
    shp = (bsz, t_len, D_MODEL)
    o = sb_attention(_pad_rows(qb.reshape(shp), ATT_SLAB), _pad_rows(kb.reshape(shp), ATT_SLAB),
                     _pad_rows(vb.reshape(shp), ATT_SLAB), past_k, past_v, p_valid)
    o = o[:, :t_len].reshape(bsz * t_len, D_MODEL)
    y = layer_tail(x, [o], [wts['w_out_odd']],
                   wts['ln_mlp'][1], wts['w_up'][1], wts['w_down'][1], wts['ln_final'], final_norm=True)
    return y, s_t, x_re, x_im, k32, v32


def kernel(x_prompt, x_sample, state_hgrn, state_ssm_re, state_ssm_im, cache_k, cache_v, meta_tokens,
           ln_mix, ln_mlp, ln_final, w_in_even, hgrn_lb, hgrn_norm, ssm_a_re, ssm_a_im, ssm_log_dt,
           ssm_b_re, ssm_b_im, ssm_c_re, ssm_c_im, ssm_d, w_glu, w_out_even, w_in_odd, w_out_odd,
           w_up, w_down):
    bsz, seq, _ = x_prompt.shape
    dbsz, dseq, _ = x_sample.shape
    past = cache_k.shape[2]
    lb_all = jnp.cumsum(jax.nn.softmax(hgrn_lb.astype(F32), axis=0), axis=0)
    w_out_e = w_out_even[0].astype(BF16)
    wts = dict(
        ln_mix=ln_mix.astype(F32), ln_mlp=ln_mlp.astype(F32), ln_final=ln_final.astype(F32),
        w_in_even=w_in_even[0].astype(BF16), lb=lb_all[0], hgrn_norm=hgrn_norm[0].astype(F32),
        s5=s5_weights(ssm_a_re[0], ssm_a_im[0], ssm_log_dt[0], ssm_b_re[0], ssm_b_im[0],
                      ssm_c_re[0], ssm_c_im[0], ssm_d[0]),
        w_glu=w_glu[0].astype(BF16), w_out_even_a=w_out_e[:A_WIDTH], w_out_even_b=w_out_e[A_WIDTH:],
        w_in_odd=w_in_odd[0].astype(BF16), w_out_odd=w_out_odd[0].astype(BF16),
        w_up=w_up.astype(BF16), w_down=w_down.astype(BF16))

    zeros_s = jnp.zeros((1, A_HEADS, A_DV, A_DK), F32)
    zeros_x = jnp.zeros((1, S5_GROUPS, S5_STATE), F32)
    _, m_s, m_re, m_im, m_k, m_v = _run_stream(
        meta_tokens.astype(F32), 1, N_META, wts, zeros_s, zeros_x, zeros_x, None, None, 0)

    m_kp = _pad_rows(m_k[None], ATT_BLOCK).reshape(1, -1, C_HEADS, C_HD)
    m_vp = _pad_rows(m_v[None], ATT_BLOCK).reshape(1, -1, C_HEADS, C_HD)
    y_p, s_p, re_p, im_p, k_p, v_p = _run_stream(
        x_prompt.reshape(bsz * seq, D_MODEL), bsz, seq, wts, m_s,
        jnp.broadcast_to(m_re, (bsz, S5_GROUPS, S5_STATE)), jnp.broadcast_to(m_im, (bsz, S5_GROUPS, S5_STATE)),
        m_kp, m_vp, N_META, lead_kv=(m_k, m_v))

    pad_past = ((0, 0), (0, -past % ATT_BLOCK), (0, 0), (0, 0))
    y_s, s_s, re_s, im_s, k_s, v_s = _run_stream(
        x_sample.reshape(dbsz * dseq, D_MODEL), dbsz, dseq, wts,
        jnp.swapaxes(state_hgrn[0].astype(F32), -1, -2), state_ssm_re[0], state_ssm_im[0],
        jnp.pad(cache_k[0].astype(F32), pad_past), jnp.pad(cache_v[0].astype(F32), pad_past), past)

    kv_shape = (1, bsz, N_META + seq, C_HEADS, C_HD)
    return (y_p.reshape(bsz, seq, D_MODEL), y_s.reshape(dbsz, dseq, D_MODEL),
            jnp.swapaxes(s_p, -1, -2)[None], jnp.swapaxes(s_s, -1, -2)[None],
            re_p[None], im_p[None], re_s[None], im_s[None],
            k_p.reshape(kv_shape), v_p.reshape(kv_shape),
            k_s.reshape(1, dbsz, dseq, C_HEADS, C_HD), v_s.reshape(1, dbsz, dseq, C_HEADS, C_HD))
```

```python
import functools
import math

import numpy as np
import jax
import jax.numpy as jnp
from jax import lax
from jax.experimental import pallas as pl
from jax.experimental.pallas import tpu as pltpu

F32 = jnp.float32
BF16 = jnp.bfloat16

D_MODEL = 1024
N_META = 16
A_HEADS = 4
A_DK = 128
A_DV = 128
A_WIDTH = 512
A_QK = A_HEADS * A_DK
B_WIDTH = 512
S5_GROUP = 16
S5_GROUPS = 32
S5_STATE = 64
S5_NSTATE = S5_GROUPS * S5_STATE
C_HEADS = 8
C_HD = 128
D_FF = 4 * D_MODEL
EPS = 1e-6

SUBLANES = 8
LANES = 128
VMEM_LIMIT_BYTES = 56 * 1024 * 1024

HGRN_CHUNK = 64
HGRN_STEP_ROWS = 512
HGRN_LOOKAHEAD = 2
LAYER_TAIL_SUB_ROWS = 256
S5_STEP_T = 64
S5_SCAN_BLOCKS = 2
ATT_BLOCK = 128
ATT_SLAB = 256
ATT_DEAD_LOG_WEIGHT = -104.0
NEG_BIG = -1e30


def _cparams(*sem):
    return pltpu.CompilerParams(dimension_semantics=sem, vmem_limit_bytes=VMEM_LIMIT_BYTES)


def _sigmoid(x):
    return 1.0 / (1.0 + jnp.exp(-x))


def _rms_scale(x):
    return lax.rsqrt(jnp.mean(x * x, axis=-1, keepdims=True) + EPS)


def _norm_matmul_kernel(x_ref, g_ref, w_ref, *out_refs, cols, nchunk):
    x = x_ref[...]
    h = (x * _rms_scale(x) * g_ref[...]).astype(BF16)
    for c0, width in sorted(set(cols)):
        for c in range(0, width, nchunk):
            res = jnp.dot(h, w_ref[:, c0 + c:c0 + c + nchunk], preferred_element_type=F32)
            for o_ref, col in zip(out_refs, cols):
                if col == (c0, width):
                    o_ref[:, c:c + nchunk] = res.astype(o_ref.dtype)


def norm_matmul(x, g, w_bf16, outs, *, tm=512, nchunk=512):
    m, d = x.shape
    tm = min(tm, m)
    kern = functools.partial(_norm_matmul_kernel, cols=[(c0, wd) for c0, wd, _ in outs], nchunk=nchunk)
    return pl.pallas_call(
        kern,
        grid=(pl.cdiv(m, tm),),
        in_specs=[pl.BlockSpec((tm, d), lambda i: (i, 0)),
                  pl.BlockSpec((1, d), lambda i: (0, 0)),
                  pl.BlockSpec(w_bf16.shape, lambda i: (0, 0))],
        out_specs=[pl.BlockSpec((tm, wd), lambda i: (i, 0)) for _, wd, _ in outs],
        out_shape=[jax.ShapeDtypeStruct((m, wd), dt) for _, wd, dt in outs],
        compiler_params=_cparams("parallel"),
        name="norm_matmul",
    )(x, g.reshape(1, d), w_bf16)


def _qkv_shifted_kernel(x_ref, g_ref, w_ref, lead_ref, qb_ref, kb_ref, vb_ref, kf_ref, vf_ref, carry_scr,
                        *, n_lead, nchunk):
    t = pl.program_id(1)
    nt = pl.num_programs(1) - 1
    tm, width = qb_ref.shape

    @pl.when(t == 0)
    def _():
        carry_scr[...] = lead_ref[...]

    for n, f_ref in enumerate((kf_ref, vf_ref)):
        f_ref[:n_lead, :] = carry_scr[n]

    @pl.when(t < nt)
    def _():
        x = x_ref[...]
        h = (x * _rms_scale(x) * g_ref[...]).astype(BF16)
        for n, (b_ref, f_ref) in enumerate(((qb_ref, None), (kb_ref, kf_ref), (vb_ref, vf_ref))):
            for c in range(0, width, nchunk):
                res = jnp.dot(h, w_ref[:, n * width + c:n * width + c + nchunk], preferred_element_type=F32)
                b_ref[:, c:c + nchunk] = res.astype(b_ref.dtype)
                if f_ref is not None:
                    f_ref[n_lead:, c:c + nchunk] = res[:tm - n_lead]
                    carry_scr[n - 1, :, c:c + nchunk] = res[tm - n_lead:]


def qkv_shifted(x, g, w_bf16, lead_k, lead_v, bsz, t_len, *, tm=512, nchunk=512):
    m, d = x.shape
    width = w_bf16.shape[1] // 3
    n_lead = lead_k.shape[0]
    tm = math.gcd(tm, t_len)
    nt = t_len // tm
    tok = lambda b, t: (b * nt + jnp.minimum(t, nt - 1), 0)
    return pl.pallas_call(
        functools.partial(_qkv_shifted_kernel, n_lead=n_lead, nchunk=nchunk),
        grid=(bsz, nt + 1),
        in_specs=[pl.BlockSpec((tm, d), tok),
                  pl.BlockSpec((1, d), lambda b, t: (0, 0)),
                  pl.BlockSpec(w_bf16.shape, lambda b, t: (0, 0)),
                  pl.BlockSpec((2, n_lead, width), lambda b, t: (0, 0, 0))],
        out_specs=[pl.BlockSpec((tm, width), tok)] * 3
                  + [pl.BlockSpec((None, tm, width), lambda b, t: (b, t, 0))] * 2,
        out_shape=[jax.ShapeDtypeStruct((m, width), BF16)] * 3
                  + [jax.ShapeDtypeStruct((bsz, n_lead + t_len, width), F32)] * 2,
        scratch_shapes=[pltpu.VMEM((2, n_lead, width), F32)],
        compiler_params=_cparams("parallel", "arbitrary"),
        name="qkv_shifted",
    )(x, g.reshape(1, d), w_bf16, jnp.stack([lead_k, lead_v]))


def _layer_tail_kernel(*refs, n_mix, final_norm):
    x_ref = refs[0]
    mix = refs[1:1 + n_mix]
    g_ref, wup_ref, wdn_ref, gf_ref, o_ref, h_scr = refs[1 + n_mix:]
    j = pl.program_id(1)

    @pl.when(j == 0)
    def _():
        tm = x_ref.shape[0]
        sub = math.gcd(tm, LAYER_TAIL_SUB_ROWS)
        def finish(rs, x1):
            h_scr[rs, :] = (x1 * _rms_scale(x1) * g_ref[...]).astype(BF16)
            o_ref[rs, :] = x1

        pending = None
        for r0 in range(0, tm, sub):
            rs = slice(r0, r0 + sub)
            if n_mix == 5:
                oa_ref, yg_ref, wglu_ref, wa_ref, wb_ref = mix
                yg = yg_ref[rs, :]
                gate = _sigmoid(jnp.dot(yg.astype(BF16), wglu_ref[...], preferred_element_type=F32))
                ob = (yg.astype(F32) * gate).astype(BF16)
                x1 = (x_ref[rs, :] + jnp.dot(oa_ref[rs, :], wa_ref[...], preferred_element_type=F32)
                      + jnp.dot(ob, wb_ref[...], preferred_element_type=F32))
            else:
                om_ref, wo_ref = mix
                x1 = x_ref[rs, :] + jnp.dot(om_ref[rs, :], wo_ref[...], preferred_element_type=F32)
            if pending is not None:
                finish(*pending)
            pending = (rs, x1)
        finish(*pending)

    a = jnp.dot(h_scr[...], wup_ref[...], preferred_element_type=F32)
    a = jnp.square(jnp.maximum(a, 0.0)).astype(BF16)
    o_ref[...] += jnp.dot(a, wdn_ref[...], preferred_element_type=F32)

    if final_norm:
        @pl.when(j == pl.num_programs(1) - 1)
        def _():
            y = o_ref[...]
            o_ref[...] = y * _rms_scale(y) * gf_ref[...]


def layer_tail(x, mix_rows, mix_weights, g, w_up, w_down, g_final, *, final_norm, tm=1024, tf=2048):
    m, d = x.shape
    tm = min(tm, m)
    ff = w_up.shape[1]
    row = lambda i, j: (i, 0)
    fixed = lambda i, j: (0, 0)
    once = dict(pipeline_mode=pl.Buffered(1))
    return pl.pallas_call(
        functools.partial(_layer_tail_kernel, n_mix=len(mix_rows) + len(mix_weights), final_norm=final_norm),
        grid=(pl.cdiv(m, tm), ff // tf),
        in_specs=([pl.BlockSpec((tm, d), row)]
                  + [pl.BlockSpec((tm, r.shape[1]), row) for r in mix_rows]
                  + [pl.BlockSpec(w.shape, fixed, **once) for w in mix_weights]
                  + [pl.BlockSpec((1, d), fixed),
                     pl.BlockSpec((d, tf), lambda i, j: (0, j)),
                     pl.BlockSpec((tf, d), lambda i, j: (j, 0)),
                     pl.BlockSpec((1, d), fixed)]),
        out_specs=pl.BlockSpec((tm, d), row),
        out_shape=jax.ShapeDtypeStruct((m, d), F32),
        scratch_shapes=[pltpu.VMEM((tm, d), BF16)],
        compiler_params=_cparams("parallel", "arbitrary"),
        name="layer_tail",
    )(x, *mix_rows, *mix_weights, g.reshape(1, d), w_up, w_down, g_final.reshape(1, d))


def _split3(x):
    hi = x.astype(BF16)
    r1 = x - hi.astype(F32)
    mid = r1.astype(BF16)
    lo = (r1 - mid.astype(F32)).astype(BF16)
    return hi, mid, lo


def _hgrn_offdiag_mask(chunk):
    n_blk = chunk // SUBLANES
    seg = np.concatenate([np.full(SUBLANES * i, i) for i in range(1, n_blk)])
    blk = np.repeat(np.arange(1, n_blk), SUBLANES)
    return (blk[:, None] == seg[None, :]).astype(np.float32)


def _hgrn_kernel(*refs, chunk, n_sub):
    n_blk = chunk // SUBLANES
    row_scr = [refs[len(refs) - 3 * (n_sub - u):len(refs) - 3 * (n_sub - u - 1)] for u in range(n_sub)]
    refs = refs[:-3 * n_sub]
    if n_blk > 1:
        q_ref, zf_ref, iv_ref, g_ref, lb_ref, gn_ref, s0_ref, mask_ref, o_ref, sout_ref, s_scr = refs
    else:
        q_ref, zf_ref, iv_ref, g_ref, lb_ref, gn_ref, s0_ref, o_ref, sout_ref, s_scr = refs
    step = pl.program_id(1)

    @pl.when(step == 0)
    def _():
        s_scr[...] = s0_ref[...]

    lb = lb_ref[...]
    gn = gn_ref[...]
    rows = lax.broadcasted_iota(jnp.int32, (chunk, chunk), 0)
    cols = lax.broadcasted_iota(jnp.int32, (chunk, chunk), 1)
    tri = jnp.where(rows >= cols, 1.0, 0.0).astype(BF16)
    hrow = lax.broadcasted_iota(jnp.int32, (A_QK, A_QK), 0) // A_DK
    hcol = lax.broadcasted_iota(jnp.int32, (A_QK, A_QK), 1) // A_DK
    head_ones = jnp.where(hrow == hcol, 1.0, 0.0).astype(BF16)
    sub = lax.broadcasted_iota(jnp.int32, (SUBLANES, A_QK), 0)
    head_lanes = [slice(hh * A_DK, (hh + 1) * A_DK) for hh in range(A_HEADS)]
    nt = (((1,), (1,)), ((), ()))
    tn = (((0,), (0,)), ((), ()))

    def pairwise_phase(cc, b_scr, k_scr, iv_scr):
        r0 = cc * chunk
        q = q_ref[pl.ds(r0, chunk), :].astype(F32)
        zf = zf_ref[pl.ds(r0, chunk), :]
        iv = iv_ref[pl.ds(r0, chunk), :].astype(F32)
        g = g_ref[pl.ds(r0, chunk), :]
        f = lb + (1.0 - lb) * _sigmoid(zf)
        logf = jnp.log(f)
        kk = 1.0 - f
        hi, mid, lo = _split3(logf)
        b = (jnp.dot(tri, hi, preferred_element_type=F32)
             + jnp.dot(tri, mid, preferred_element_type=F32)
             + jnp.dot(tri, lo, preferred_element_type=F32))
        b_scr[...] = b
        k_scr[...] = kk
        iv_scr[...] = iv

        ws = []
        for i in range(n_blk):
            b_i = b[SUBLANES * i:SUBLANES * (i + 1)]
            q_i = q[SUBLANES * i:SUBLANES * (i + 1)]
            for s in range(SUBLANES):
                row = SUBLANES * i + s
                dlt = jnp.where(sub >= s, b_i - b_scr[row:row + 1, :], NEG_BIG)
                ws.append(jnp.exp(dlt) * q_i * k_scr[row:row + 1, :])
        w = jnp.concatenate(ws, axis=0).astype(BF16)
        return r0, q, iv, g, kk, b, w, b_scr, iv_scr

    def matmul_phase(r0, q, iv, g, kk, b, w, b_scr, iv_scr):
        att = jnp.dot(w, head_ones, preferred_element_type=F32)
        o_blocks = []
        for i in range(n_blk):
            acc = jnp.zeros((SUBLANES, A_WIDTH), F32)
            for s in range(SUBLANES):
                row = SUBLANES * i + s
                acc = acc + att[SUBLANES * row:SUBLANES * (row + 1)] * iv_scr[row:row + 1, :]
            o_blocks.append(acc)
        o = jnp.concatenate(o_blocks, axis=0)

        if n_blk > 1:
            qt, kh, ivs = [], [], []
            for i in range(1, n_blk):
                n_s = SUBLANES * i
                r_i = b_scr[n_s - 1:n_s, :]
                qt.append(q[n_s:n_s + SUBLANES] * jnp.exp(b[n_s:n_s + SUBLANES] - r_i))
                kh.append(kk[:n_s] * jnp.exp(r_i - b[:n_s]))
                ivs.append(iv[:n_s])
            qt = jnp.concatenate(qt, axis=0).astype(BF16)
            kh = jnp.concatenate(kh, axis=0).astype(BF16)
            ivs = jnp.concatenate(ivs, axis=0).astype(BF16)
            mask = mask_ref[...]
            o_off = []
            for l in head_lanes:
                a = lax.dot_general(qt[:, l], kh[:, l], nt, preferred_element_type=F32)
                o_off.append(jnp.dot((a * mask).astype(BF16), ivs[:, l], preferred_element_type=F32))
            o_off = jnp.concatenate(o_off, axis=1)
            o = o + jnp.concatenate([jnp.zeros((SUBLANES, A_WIDTH), F32), o_off], axis=0)

        qh = (q * jnp.exp(b)).astype(BF16)
        b_last = b[chunk - 1:chunk, :]
        kd = (kk * jnp.exp(b_last - b)).astype(BF16)
        dec = jnp.exp(b_last)
        ivb = iv.astype(BF16)
        o_inter = []
        for hh, l in enumerate(head_lanes):
            s_t = s_scr[hh]
            o_inter.append(lax.dot_general(qh[:, l], s_t.astype(BF16), nt, preferred_element_type=F32))
            upd = lax.dot_general(ivb[:, l], kd[:, l], tn, preferred_element_type=F32)
            s_scr[hh] = s_t * dec[:, l] + upd
        o = o + jnp.concatenate(o_inter, axis=1)

        o = jnp.concatenate([o[:, l] * _rms_scale(o[:, l]) * gn for l in head_lanes], axis=1)
        o = o * (g * _sigmoid(g))
        o_ref[pl.ds(r0, chunk), :] = o.astype(o_ref.dtype)

    pending = []
    for u in range(n_sub):
        pending.append(pairwise_phase(u, *row_scr[u]))
        if len(pending) > HGRN_LOOKAHEAD:
            matmul_phase(*pending.pop(0))
    for ctx in pending:
        matmul_phase(*ctx)

    @pl.when(step == pl.num_programs(1) - 1)
    def _():
        sout_ref[...] = s_scr[...]


def hgrn(q, zf, iv, g, lb, gnorm, s0_t, bsz, t_len):
    chunk = HGRN_CHUNK if t_len % HGRN_CHUNK == 0 else SUBLANES
    step_rows = math.gcd(HGRN_STEP_ROWS, t_len)
    assert step_rows % chunk == 0, (t_len, step_rows, chunk)
    n_steps = t_len // step_rows
    tok = pl.BlockSpec((step_rows, A_QK), lambda b, c: (b * n_steps + c, 0))
    s0_b = (lambda b: b) if s0_t.shape[0] == bsz else (lambda b: 0)
    in_specs = [tok, tok, tok, tok,
                pl.BlockSpec((1, A_QK), lambda b, c: (0, 0)),
                pl.BlockSpec((1, A_DV), lambda b, c: (0, 0)),
                pl.BlockSpec((None, A_HEADS, A_DV, A_DK), lambda b, c: (s0_b(b), 0, 0, 0))]
    args = [q, zf, iv, g, lb.reshape(1, A_QK), gnorm.reshape(1, A_DV), s0_t]
    if chunk > SUBLANES:
        mask = jnp.asarray(_hgrn_offdiag_mask(chunk))
        in_specs.append(pl.BlockSpec(mask.shape, lambda b, c: (0, 0)))
        args.append(mask)
    return pl.pallas_call(
        functools.partial(_hgrn_kernel, chunk=chunk, n_sub=step_rows // chunk),
        grid=(bsz, n_steps),
        in_specs=in_specs,
        out_specs=[tok, pl.BlockSpec((None, A_HEADS, A_DV, A_DK), lambda b, c: (b, 0, 0, 0))],
        out_shape=[jax.ShapeDtypeStruct((bsz * t_len, A_WIDTH), BF16),
                   jax.ShapeDtypeStruct((bsz, A_HEADS, A_DV, A_DK), F32)],
        scratch_shapes=([pltpu.VMEM((A_HEADS, A_DV, A_DK), F32)]
                        + [pltpu.VMEM((chunk, A_QK), F32)] * (3 * (step_rows // chunk))),
        compiler_params=_cparams("parallel", "arbitrary"),
        name="hgrn",
    )(*args)


def s5_weights(a_re, a_im, log_dt, b_re, b_im, c_re, c_im, d_skip):
    f32 = F32
    ar = a_re.astype(f32)
    ai = a_im.astype(f32)
    dt = jnp.exp(log_dt.astype(f32))[:, None]
    mag = jnp.exp(dt * ar)
    abar_re = mag * jnp.cos(dt * ai)
    abar_im = mag * jnp.sin(dt * ai)
    den = ar * ar + ai * ai
    zr = ((abar_re - 1.0) * ar + abar_im * ai) / den
    zi = (abar_im * ar - (abar_re - 1.0) * ai) / den
    br, bi = b_re.astype(f32), b_im.astype(f32)
    bb_re = zr[..., None] * br - zi[..., None] * bi
    bb_im = zr[..., None] * bi + zi[..., None] * br
    eye = jnp.eye(S5_GROUPS, dtype=f32)

    def in_proj(bb):
        return jnp.einsum('gnp,gh->gphn', bb, eye).reshape(B_WIDTH, S5_NSTATE)

    def out_proj(c):
        return jnp.einsum('gpn,gh->gnhp', c, eye).reshape(S5_NSTATE, B_WIDTH)

    n_sb = B_WIDTH // LANES
    sw = S5_NSTATE // n_sb

    def diag_blocks(full, rows, cols):
        return jnp.stack([full[c * rows:(c + 1) * rows, c * cols:(c + 1) * cols] for c in range(n_sb)])

    b_blk = jnp.concatenate([diag_blocks(in_proj(bb_re), LANES, sw), diag_blocks(in_proj(bb_im), LANES, sw)],
                            axis=2).astype(BF16)
    c_blk = jnp.concatenate([diag_blocks(out_proj(c_re.astype(f32)), sw, LANES),
                             -diag_blocks(out_proj(c_im.astype(f32)), sw, LANES)],
                            axis=1).astype(BF16)
    return dict(b_blk=b_blk, c_blk=c_blk, a_re=abar_re.reshape(1, S5_NSTATE),
                a_im=abar_im.reshape(1, S5_NSTATE), d=d_skip.astype(f32).reshape(1, B_WIDTH))


def _s5_kernel(u_ref, x0re_ref, x0im_ref, bblk_ref, cblk_ref, are_ref, aim_ref, d_ref,
               y_ref, fre_ref, fim_ref, u_scr, bu_scr, xre_scr, xim_scr, *, tt, nb):
    step = pl.program_id(0)

    @pl.when(step == 0)
    def _():
        xre_scr[...] = x0re_ref[...]
        xim_scr[...] = x0im_ref[...]

    n_lt = B_WIDTH // LANES
    u_bt = u_ref[...].reshape(nb * tt, B_WIDTH).astype(F32)
    for c in range(n_lt):
        u_scr[c] = u_bt[:, c * LANES:(c + 1) * LANES]
    sw = S5_NSTATE // n_lt
    skips = []
    for c in range(n_lt):
        u_tb = jnp.concatenate([u_scr[c, pl.ds(t, nb, stride=tt), :] for t in range(tt)], axis=0)
        bu_scr[c] = jnp.dot(u_tb.astype(BF16), bblk_ref[c], preferred_element_type=F32)
        skips.append(d_ref[:, c * LANES:(c + 1) * LANES] * u_tb)

    for c0 in range(0, n_lt, S5_SCAN_BLOCKS):
        blocks = range(c0, c0 + S5_SCAN_BLOCKS)
        st_ls = [slice(c * sw, (c + 1) * sw) for c in blocks]
        a_res = [jnp.broadcast_to(are_ref[:, l], (nb, sw)) for l in st_ls]
        a_ims = [jnp.broadcast_to(aim_ref[:, l], (nb, sw)) for l in st_ls]

        def body(t, st, blocks=blocks, a_res=a_res, a_ims=a_ims):
            r0 = pl.multiple_of(t * nb, nb)
            out = []
            for c, a_re, a_im, (x_re, x_im) in zip(blocks, a_res, a_ims, st):
                n_re = a_re * x_re - a_im * x_im + bu_scr[c, pl.ds(r0, nb), :sw]
                n_im = a_re * x_im + a_im * x_re + bu_scr[c, pl.ds(r0, nb), sw:]
                bu_scr[c, pl.ds(r0, nb), :sw] = n_re
                bu_scr[c, pl.ds(r0, nb), sw:] = n_im
                out.append((n_re, n_im))
            return tuple(out)

        fin = lax.fori_loop(0, tt, body, tuple((xre_scr[:, l], xim_scr[:, l]) for l in st_ls))
        for l, (x_re, x_im) in zip(st_ls, fin):
            xre_scr[:, l] = x_re
            xim_scr[:, l] = x_im

    for c in range(n_lt):
        y = jnp.dot(bu_scr[c].astype(BF16), cblk_ref[c], preferred_element_type=F32) + skips[c]
        y = 0.5 * y * (1.0 + lax.erf(y * (1.0 / math.sqrt(2.0))))
        for t in range(tt):
            u_scr[c, pl.ds(t, nb, stride=tt), :] = y[nb * t:nb * (t + 1)]
    y_bt = jnp.concatenate([u_scr[c] for c in range(n_lt)], axis=1).reshape(nb, tt, B_WIDTH)
    y_ref[...] = y_bt.astype(y_ref.dtype)

    @pl.when(step == pl.num_programs(0) - 1)
    def _():
        fre_ref[...] = xre_scr[...]
        fim_ref[...] = xim_scr[...]


def s5(u, wts, x0_re, x0_im, bsz, t_len):
    nb = SUBLANES
    tt = min(S5_STEP_T, t_len)
    u3 = u.reshape(bsz, t_len, B_WIDTH)
    x0 = [x.astype(F32).reshape(bsz, S5_NSTATE) for x in (x0_re, x0_im)]
    if bsz < nb:
        u3 = jnp.pad(u3, ((0, nb - bsz), (0, 0), (0, 0)))
        x0 = [jnp.pad(x, ((0, nb - bsz), (0, 0))) for x in x0]
    fixed = lambda i: (0, 0)
    fixed3 = lambda i: (0, 0, 0)
    n_sb = B_WIDTH // LANES
    sw2 = 2 * S5_NSTATE // n_sb
    tok = pl.BlockSpec((nb, tt, B_WIDTH), lambda i: (0, i, 0))
    state = pl.BlockSpec((nb, S5_NSTATE), fixed)
    yg, fre, fim = pl.pallas_call(
        functools.partial(_s5_kernel, tt=tt, nb=nb),
        grid=(t_len // tt,),
        in_specs=[tok, state, state,
                  pl.BlockSpec((n_sb, LANES, sw2), fixed3), pl.BlockSpec((n_sb, sw2, LANES), fixed3),
                  pl.BlockSpec((1, S5_NSTATE), fixed), pl.BlockSpec((1, S5_NSTATE), fixed),
                  pl.BlockSpec((1, B_WIDTH), fixed)],
        out_specs=[tok, state, state],
        out_shape=[jax.ShapeDtypeStruct((nb, t_len, B_WIDTH), u.dtype),
                   jax.ShapeDtypeStruct((nb, S5_NSTATE), F32), jax.ShapeDtypeStruct((nb, S5_NSTATE), F32)],
        scratch_shapes=[pltpu.VMEM((n_sb, nb * tt, LANES), F32),
                        pltpu.VMEM((n_sb, nb * tt, sw2), F32),
                        pltpu.VMEM((nb, S5_NSTATE), F32), pltpu.VMEM((nb, S5_NSTATE), F32)],
        compiler_params=_cparams("arbitrary"),
        name="s5",
    )(u3, x0[0], x0[1], wts['b_blk'], wts['c_blk'], wts['a_re'], wts['a_im'], wts['d'])
    yg = yg[:bsz].reshape(bsz * t_len, B_WIDTH)
    return (yg, fre[:bsz].reshape(bsz, S5_GROUPS, S5_STATE), fim[:bsz].reshape(bsz, S5_GROUPS, S5_STATE))


def _sb_attn_kernel(*refs, heads, n_past, p_valid, scale):
    if n_past:
        q_ref, k_ref, v_ref, pk_ref, pv_ref, o_ref, carry_scr, acc_scr = refs
    else:
        q_ref, k_ref, v_ref, o_ref, carry_scr, acc_scr = refs
    blk = ATT_BLOCK
    i = pl.program_id(1)

    def suffix_matrix(tk):
        srow = lax.broadcasted_iota(jnp.int32, (2 * tk, tk), 0)
        scol = lax.broadcasted_iota(jnp.int32, (2 * tk, tk), 1)
        return jnp.where(jnp.where(srow >= tk, srow - tk, srow) > scol, 1.0, 0.0).astype(BF16)

    suffix = {tk: suffix_matrix(tk) for tk in (blk, ATT_SLAB)}
    rows = lax.broadcasted_iota(jnp.int32, (blk, blk), 0)
    cols = lax.broadcasted_iota(jnp.int32, (blk, blk), 1)

    carry_scr[...] = jnp.zeros_like(carry_scr)
    acc_scr[...] = jnp.zeros_like(acc_scr)
    head_lanes = [slice(hh * C_HD, (hh + 1) * C_HD) for hh in range(heads)]
    n_rows = heads * blk

    def per_head(x, fn):
        return jnp.concatenate([fn(x[hh * blk:(hh + 1) * blk]) for hh in range(heads)], axis=0)

    def rows_of(ref, rj, tk, hh):
        if len(ref.shape) == 2:
            return ref[pl.ds(rj, tk), head_lanes[hh]]
        return ref[pl.ds(rj, tk), hh, :].astype(BF16)

    def visit(kr, vr, rj, specs):
        zs = []
        for s, (tk, _) in enumerate(specs):
            q_rows = slice(s * blk, (s + 1) * blk)
            zs.append(jnp.concatenate(
                [lax.dot_general(q_ref[q_rows, l], rows_of(kr, rj, tk, hh), (((1,), (1,)), ((), ())),
                                 preferred_element_type=F32) for hh, l in enumerate(head_lanes)], axis=0) * scale)
        mids = []
        for z, (tk, mask) in zip(zs, specs):
            log_b = jnp.minimum(z, 0.0) - jnp.log(1.0 + jnp.exp(-jnp.abs(z)))
            x = log_b - z
            if mask is not None:
                x = per_head(x, lambda xh, mask=mask: jnp.where(mask, xh, 0.0))
            hi = x.astype(BF16)
            lo = (x - hi.astype(F32)).astype(BF16)
            cs = jnp.dot(jnp.concatenate([hi, lo], axis=1), suffix[tk], preferred_element_type=F32)
            total = jnp.broadcast_to(cs[:, 0:1] + x[:, 0:1], (n_rows, blk))
            mids.append((log_b, cs, total))
        live = None
        for s, ((log_b, cs, total), (tk, mask)) in enumerate(zip(mids, specs)):
            carry = carry_scr[s]
            w = jnp.exp(log_b + cs + jnp.concatenate([carry] * (tk // blk), axis=1))
            if mask is not None:
                w = per_head(w, lambda wh, mask=mask: jnp.where(mask, wh, 0.0))
            w = w.astype(BF16)
            acc_scr[s] += jnp.concatenate(
                [jnp.dot(w[hh * blk:(hh + 1) * blk], rows_of(vr, rj, tk, hh),
                         preferred_element_type=F32) for hh in range(heads)], axis=0)
            carry = carry + total
            carry_scr[s] = carry
            m = jnp.max(carry)
            live = m if live is None else jnp.maximum(live, m)
        return live

    rows2 = lax.broadcasted_iota(jnp.int32, (blk, ATT_SLAB), 0)
    cols2 = lax.broadcasted_iota(jnp.int32, (blk, ATT_SLAB), 1)
    live = visit(k_ref, v_ref, pl.multiple_of(i * ATT_SLAB, ATT_SLAB),
                 [(blk, cols < rows), (ATT_SLAB, cols2 < rows2 + blk)])

    def slab_body(st):
        j, _ = st
        return j - 2, visit(k_ref, v_ref, pl.multiple_of((j - 1) * blk, blk), [(ATT_SLAB, None)] * 2)

    def alive_from(first):
        return lambda st: jnp.logical_and(st[0] >= first, st[1] > ATT_DEAD_LOG_WEIGHT)

    _, live = lax.while_loop(alive_from(1), slab_body, (2 * i - 1, live))

    if n_past:
        def past_body(st):
            j, _ = st
            return j - 1, visit(pk_ref, pv_ref, pl.multiple_of(j * blk, blk),
                                [(blk, (cols + j * blk) < p_valid)] * 2)

        lax.while_loop(alive_from(0), past_body, (jnp.int32(n_past - 1), live))

    for s in range(2):
        for hh in range(heads):
            o_ref[s * blk:(s + 1) * blk, head_lanes[hh]] = (
                acc_scr[s, hh * blk:(hh + 1) * blk, :].astype(o_ref.dtype))


def sb_attention(q, k, v, past_k, past_v, p_valid):
    bsz, t_len, width = q.shape
    n_past = 0 if past_k is None else past_k.shape[1] // ATT_BLOCK
    q_spec = pl.BlockSpec((None, ATT_SLAB, width), lambda b, i: (b, i, 0))
    kv_spec = pl.BlockSpec((None, t_len, width), lambda b, i: (b, 0, 0))
    in_specs = [q_spec, kv_spec, kv_spec]
    args = [q, k, v]
    if n_past:
        pb = (lambda b: b) if past_k.shape[0] == bsz else (lambda b: 0)
        p_spec = pl.BlockSpec((None,) + past_k.shape[1:], lambda b, i: (pb(b), 0, 0, 0))
        in_specs += [p_spec, p_spec]
        args += [past_k, past_v]
    return pl.pallas_call(
        functools.partial(_sb_attn_kernel, heads=C_HEADS, n_past=n_past, p_valid=p_valid,
                          scale=C_HD ** -0.5),
        grid=(bsz, t_len // ATT_SLAB),
        in_specs=in_specs,
        out_specs=q_spec,
        out_shape=jax.ShapeDtypeStruct((bsz, t_len, width), BF16),
        scratch_shapes=[pltpu.VMEM((2, C_HEADS * ATT_BLOCK, C_HD), F32)] * 2,
        compiler_params=_cparams("parallel", "arbitrary"),
        name="sb_attention",
    )(*args)


def _pad_rows(x, mult):
    t = x.shape[1]
    tp = -(-t // mult) * mult
    return x if tp == t else jnp.pad(x, ((0, 0), (0, tp - t), (0, 0)))


def _run_stream(x, bsz, t_len, wts, hgrn_s0_t, ssm0_re, ssm0_im, past_k, past_v, p_valid, lead_kv=None):
    act = BF16 if t_len % HGRN_CHUNK == 0 else F32
    q, zf, iv, g, u = norm_matmul(
        x, wts['ln_mix'][0], wts['w_in_even'],
        [(0, A_QK, act), (A_QK, A_QK, F32), (2 * A_QK, A_WIDTH, act),
         (2 * A_QK + A_WIDTH, A_WIDTH, F32), (2 * A_QK + 2 * A_WIDTH, B_WIDTH, act)], tm=1024)
    o_a, s_t = hgrn(q, zf, iv, g, wts['lb'], wts['hgrn_norm'], hgrn_s0_t, bsz, t_len)
    yg, x_re, x_im = s5(u, wts['s5'], ssm0_re, ssm0_im, bsz, t_len)
    x = layer_tail(x, [o_a, yg], [wts['w_glu'], wts['w_out_even_a'], wts['w_out_even_b']],
                   wts['ln_mlp'][0], wts['w_up'][0], wts['w_down'][0], wts['ln_final'], final_norm=False)
    if lead_kv is None:
        qb, kb, vb, k32, v32 = norm_matmul(
            x, wts['ln_mix'][1], wts['w_in_odd'],
            [(0, D_MODEL, BF16), (D_MODEL, D_MODEL, BF16), (2 * D_MODEL, D_MODEL, BF16),
             (D_MODEL, D_MODEL, F32), (2 * D_MODEL, D_MODEL, F32)])
    else:
        qb, kb, vb, k32, v32 = qkv_shifted(x, wts['ln_mix'][1], wts['w_in_odd'], *lead_kv, bsz, t_len)
    shp = (bsz, t_len, D_MODEL)
    o = sb_attention(_pad_rows(qb.reshape(shp), ATT_SLAB), _pad_rows(kb.reshape(shp), ATT_SLAB),
                     _pad_rows(vb.reshape(shp), ATT_SLAB), past_k, past_v, p_valid)
    o = o[:, :t_len].reshape(bsz * t_len, D_MODEL)
    y = layer_tail(x, [o], [wts['w_out_odd']],
                   wts['ln_mlp'][1], wts['w_up'][1], wts['w_down'][1], wts['ln_final'], final_norm=True)
    return y, s_t, x_re, x_im, k32, v32


def kernel(x_prompt, x_sample, state_hgrn, state_ssm_re, state_ssm_im, cache_k, cache_v, meta_tokens,
           ln_mix, ln_mlp, ln_final, w_in_even, hgrn_lb, hgrn_norm, ssm_a_re, ssm_a_im, ssm_log_dt,
           ssm_b_re, ssm_b_im, ssm_c_re, ssm_c_im, ssm_d, w_glu, w_out_even, w_in_odd, w_out_odd,
           w_up, w_down):
    bsz, seq, _ = x_prompt.shape
    dbsz, dseq, _ = x_sample.shape
    past = cache_k.shape[2]
    lb_all = jnp.cumsum(jax.nn.softmax(hgrn_lb.astype(F32), axis=0), axis=0)
    w_out_e = w_out_even[0].astype(BF16)
    wts = dict(
        ln_mix=ln_mix.astype(F32), ln_mlp=ln_mlp.astype(F32), ln_final=ln_final.astype(F32),
        w_in_even=w_in_even[0].astype(BF16), lb=lb_all[0], hgrn_norm=hgrn_norm[0].astype(F32),
        s5=s5_weights(ssm_a_re[0], ssm_a_im[0], ssm_log_dt[0], ssm_b_re[0], ssm_b_im[0],
                      ssm_c_re[0], ssm_c_im[0], ssm_d[0]),
        w_glu=w_glu[0].astype(BF16), w_out_even_a=w_out_e[:A_WIDTH], w_out_even_b=w_out_e[A_WIDTH:],
        w_in_odd=w_in_odd[0].astype(BF16), w_out_odd=w_out_odd[0].astype(BF16),
        w_up=w_up.astype(BF16), w_down=w_down.astype(BF16))

    zeros_s = jnp.zeros((1, A_HEADS, A_DV, A_DK), F32)
    zeros_x = jnp.zeros((1, S5_GROUPS, S5_STATE), F32)
    _, m_s, m_re, m_im, m_k, m_v = _run_stream(
        meta_tokens.astype(F32), 1, N_META, wts, zeros_s, zeros_x, zeros_x, None, None, 0)

    m_kp = _pad_rows(m_k[None], ATT_BLOCK).reshape(1, -1, C_HEADS, C_HD)
    m_vp = _pad_rows(m_v[None], ATT_BLOCK).reshape(1, -1, C_HEADS, C_HD)
    y_p, s_p, re_p, im_p, k_p, v_p = _run_stream(
        x_prompt.reshape(bsz * seq, D_MODEL), bsz, seq, wts, m_s,
        jnp.broadcast_to(m_re, (bsz, S5_GROUPS, S5_STATE)), jnp.broadcast_to(m_im, (bsz, S5_GROUPS, S5_STATE)),
        m_kp, m_vp, N_META, lead_kv=(m_k, m_v))

    pad_past = ((0, 0), (0, -past % ATT_BLOCK), (0, 0), (0, 0))
    y_s, s_s, re_s, im_s, k_s, v_s = _run_stream(
        x_sample.reshape(dbsz * dseq, D_MODEL), dbsz, dseq, wts,
        jnp.swapaxes(state_hgrn[0].astype(F32), -1, -2), state_ssm_re[0], state_ssm_im[0],
        jnp.pad(cache_k[0].astype(F32), pad_past), jnp.pad(cache_v[0].astype(F32), pad_past), past)

    kv_shape = (1, bsz, N_META + seq, C_HEADS, C_HD)
    return (y_p.reshape(bsz, seq, D_MODEL), y_s.reshape(dbsz, dseq, D_MODEL),
            jnp.swapaxes(s_p, -1, -2)[None], jnp.swapaxes(s_s, -1, -2)[None],
            re_p[None], im_p[None], re_s[None], im_s[None],
            k_p.reshape(kv_shape), v_p.reshape(kv_shape),
            k_s.reshape(1, dbsz, dseq, C_HEADS, C_HD), v_s.reshape(1, dbsz, dseq, C_HEADS, C_HD))
```

```python
import functools
import math

import numpy as np
import jax
import jax.numpy as jnp
from jax import lax
from jax.experimental import pallas as pl
from jax.experimental.pallas import tpu as pltpu

F32 = jnp.float32
BF16 = jnp.bfloat16

D_MODEL = 1024
N_META = 16
A_HEADS = 4
A_DK = 128
A_DV = 128
A_WIDTH = 512
A_QK = A_HEADS * A_DK
B_WIDTH = 512
S5_GROUP = 16
S5_GROUPS = 32
S5_STATE = 64
S5_NSTATE = S5_GROUPS * S5_STATE
C_HEADS = 8
C_HD = 128
D_FF = 4 * D_MODEL
EPS = 1e-6

SUBLANES = 8
LANES = 128
VMEM_LIMIT_BYTES = 56 * 1024 * 1024

HGRN_CHUNK = 64
HGRN_STEP_ROWS = 512
HGRN_LOOKAHEAD = 2
LAYER_TAIL_SUB_ROWS = 256
S5_STEP_T = 64
S5_SCAN_BLOCKS = 2
ATT_BLOCK = 128
ATT_SLAB = 256
ATT_DEAD_LOG_WEIGHT = -104.0
NEG_BIG = -1e30


def _cparams(*sem):
    return pltpu.CompilerParams(dimension_semantics=sem, vmem_limit_bytes=VMEM_LIMIT_BYTES)


def _sigmoid(x):
    return 1.0 / (1.0 + jnp.exp(-x))


def _rms_scale(x):
    return lax.rsqrt(jnp.mean(x * x, axis=-1, keepdims=True) + EPS)


def _norm_matmul_kernel(x_ref, g_ref, w_ref, *out_refs, cols, nchunk):
    x = x_ref[...]
    h = (x * _rms_scale(x) * g_ref[...]).astype(BF16)
    for c0, width in sorted(set(cols)):
        for c in range(0, width, nchunk):
            res = jnp.dot(h, w_ref[:, c0 + c:c0 + c + nchunk], preferred_element_type=F32)
            for o_ref, col in zip(out_refs, cols):
                if col == (c0, width):
                    o_ref[:, c:c + nchunk] = res.astype(o_ref.dtype)


def norm_matmul(x, g, w_bf16, outs, *, tm=512, nchunk=512):
    m, d = x.shape
    tm = min(tm, m)
    kern = functools.partial(_norm_matmul_kernel, cols=[(c0, wd) for c0, wd, _ in outs], nchunk=nchunk)
    return pl.pallas_call(
        kern,
        grid=(pl.cdiv(m, tm),),
        in_specs=[pl.BlockSpec((tm, d), lambda i: (i, 0)),
                  pl.BlockSpec((1, d), lambda i: (0, 0)),
                  pl.BlockSpec(w_bf16.shape, lambda i: (0, 0))],
        out_specs=[pl.BlockSpec((tm, wd), lambda i: (i, 0)) for _, wd, _ in outs],
        out_shape=[jax.ShapeDtypeStruct((m, wd), dt) for _, wd, dt in outs],
        compiler_params=_cparams("parallel"),
        name="norm_matmul",
    )(x, g.reshape(1, d), w_bf16)


def _qkv_shifted_kernel(x_ref, g_ref, w_ref, lead_ref, qb_ref, kb_ref, vb_ref, kf_ref, vf_ref, carry_scr,
                        *, n_lead, nchunk):
    t = pl.program_id(1)
    nt = pl.num_programs(1) - 1
    tm, width = qb_ref.shape

    @pl.when(t == 0)
    def _():
        carry_scr[...] = lead_ref[...]

    for n, f_ref in enumerate((kf_ref, vf_ref)):
        f_ref[:n_lead, :] = carry_scr[n]

    @pl.when(t < nt)
    def _():
        x = x_ref[...]
        h = (x * _rms_scale(x) * g_ref[...]).astype(BF16)
        for n, (b_ref, f_ref) in enumerate(((qb_ref, None), (kb_ref, kf_ref), (vb_ref, vf_ref))):
            for c in range(0, width, nchunk):
                res = jnp.dot(h, w_ref[:, n * width + c:n * width + c + nchunk], preferred_element_type=F32)
                b_ref[:, c:c + nchunk] = res.astype(b_ref.dtype)
                if f_ref is not None:
                    f_ref[n_lead:, c:c + nchunk] = res[:tm - n_lead]
                    carry_scr[n - 1, :, c:c + nchunk] = res[tm - n_lead:]


def qkv_shifted(x, g, w_bf16, lead_k, lead_v, bsz, t_len, *, tm=512, nchunk=512):
    m, d = x.shape
    width = w_bf16.shape[1] // 3
    n_lead = lead_k.shape[0]
    tm = math.gcd(tm, t_len)
    nt = t_len // tm
    tok = lambda b, t: (b * nt + jnp.minimum(t, nt - 1), 0)
    return pl.pallas_call(
        functools.partial(_qkv_shifted_kernel, n_lead=n_lead, nchunk=nchunk),
        grid=(bsz, nt + 1),
        in_specs=[pl.BlockSpec((tm, d), tok),
                  pl.BlockSpec((1, d), lambda b, t: (0, 0)),
                  pl.BlockSpec(w_bf16.shape, lambda b, t: (0, 0)),
                  pl.BlockSpec((2, n_lead, width), lambda b, t: (0, 0, 0))],
        out_specs=[pl.BlockSpec((tm, width), tok)] * 3
                  + [pl.BlockSpec((None, tm, width), lambda b, t: (b, t, 0))] * 2,
        out_shape=[jax.ShapeDtypeStruct((m, width), BF16)] * 3
                  + [jax.ShapeDtypeStruct((bsz, n_lead + t_len, width), F32)] * 2,
        scratch_shapes=[pltpu.VMEM((2, n_lead, width), F32)],
        compiler_params=_cparams("parallel", "arbitrary"),
        name="qkv_shifted",
    )(x, g.reshape(1, d), w_bf16, jnp.stack([lead_k, lead_v]))


def _layer_tail_kernel(*refs, n_mix, final_norm):
    x_ref = refs[0]
    mix = refs[1:1 + n_mix]
    g_ref, wup_ref, wdn_ref, gf_ref, o_ref, h_scr = refs[1 + n_mix:]
    j = pl.program_id(1)

    @pl.when(j == 0)
    def _():
        tm = x_ref.shape[0]
        sub = math.gcd(tm, LAYER_TAIL_SUB_ROWS)
        def finish(rs, x1):
            h_scr[rs, :] = (x1 * _rms_scale(x1) * g_ref[...]).astype(BF16)
            o_ref[rs, :] = x1

        pending = None
        for r0 in range(0, tm, sub):
            rs = slice(r0, r0 + sub)
            if n_mix == 5:
                oa_ref, yg_ref, wglu_ref, wa_ref, wb_ref = mix
                yg = yg_ref[rs, :]
                gate = _sigmoid(jnp.dot(yg.astype(BF16), wglu_ref[...], preferred_element_type=F32))
                ob = (yg.astype(F32) * gate).astype(BF16)
                x1 = (x_ref[rs, :] + jnp.dot(oa_ref[rs, :], wa_ref[...], preferred_element_type=F32)
                      + jnp.dot(ob, wb_ref[...], preferred_element_type=F32))
            else:
                om_ref, wo_ref = mix
                x1 = x_ref[rs, :] + jnp.dot(om_ref[rs, :], wo_ref[...], preferred_element_type=F32)
            if pending is not None:
                finish(*pending)
            pending = (rs, x1)
        finish(*pending)

    a = jnp.dot(h_scr[...], wup_ref[...], preferred_element_type=F32)
    a = jnp.square(jnp.maximum(a, 0.0)).astype(BF16)
    o_ref[...] += jnp.dot(a, wdn_ref[...], preferred_element_type=F32)

    if final_norm:
        @pl.when(j == pl.num_programs(1) - 1)
        def _():
            y = o_ref[...]
            o_ref[...] = y * _rms_scale(y) * gf_ref[...]


def layer_tail(x, mix_rows, mix_weights, g, w_up, w_down, g_final, *, final_norm, tm=1024, tf=2048):
    m, d = x.shape
    tm = min(tm, m)
    ff = w_up.shape[1]
    row = lambda i, j: (i, 0)
    fixed = lambda i, j: (0, 0)
    once = dict(pipeline_mode=pl.Buffered(1))
    return pl.pallas_call(
        functools.partial(_layer_tail_kernel, n_mix=len(mix_rows) + len(mix_weights), final_norm=final_norm),
        grid=(pl.cdiv(m, tm), ff // tf),
        in_specs=([pl.BlockSpec((tm, d), row)]
                  + [pl.BlockSpec((tm, r.shape[1]), row) for r in mix_rows]
                  + [pl.BlockSpec(w.shape, fixed, **once) for w in mix_weights]
                  + [pl.BlockSpec((1, d), fixed),
                     pl.BlockSpec((d, tf), lambda i, j: (0, j)),
                     pl.BlockSpec((tf, d), lambda i, j: (j, 0)),
                     pl.BlockSpec((1, d), fixed)]),
        out_specs=pl.BlockSpec((tm, d), row),
        out_shape=jax.ShapeDtypeStruct((m, d), F32),
        scratch_shapes=[pltpu.VMEM((tm, d), BF16)],
        compiler_params=_cparams("parallel", "arbitrary"),
        name="layer_tail",
    )(x, *mix_rows, *mix_weights, g.reshape(1, d), w_up, w_down, g_final.reshape(1, d))


def _split3(x):
    hi = x.astype(BF16)
    r1 = x - hi.astype(F32)
    mid = r1.astype(BF16)
    lo = (r1 - mid.astype(F32)).astype(BF16)
    return hi, mid, lo


def _hgrn_offdiag_mask(chunk):
    n_blk = chunk // SUBLANES
    seg = np.concatenate([np.full(SUBLANES * i, i) for i in range(1, n_blk)])
    blk = np.repeat(np.arange(1, n_blk), SUBLANES)
    return (blk[:, None] == seg[None, :]).astype(np.float32)


def _hgrn_kernel(*refs, chunk, n_sub, zero_start):
    n_blk = chunk // SUBLANES
    row_scr = [refs[len(refs) - 3 * (n_sub - u):len(refs) - 3 * (n_sub - u - 1)] for u in range(n_sub)]
    refs = refs[:-3 * n_sub]
    if n_blk > 1:
        q_ref, zf_ref, iv_ref, g_ref, lb_ref, gn_ref, s0_ref, mask_ref, o_ref, sout_ref, s_scr = refs
    else:
        q_ref, zf_ref, iv_ref, g_ref, lb_ref, gn_ref, s0_ref, o_ref, sout_ref, s_scr = refs
    step = pl.program_id(1)

    @pl.when(step == 0)
    def _():
        s_scr[...] = s0_ref[...]

    lb = lb_ref[...]
    gn = gn_ref[...]
    rows = lax.broadcasted_iota(jnp.int32, (chunk, chunk), 0)
    cols = lax.broadcasted_iota(jnp.int32, (chunk, chunk), 1)
    tri = jnp.where(rows >= cols, 1.0, 0.0).astype(BF16)
    hrow = lax.broadcasted_iota(jnp.int32, (A_QK, A_QK), 0) // A_DK
    hcol = lax.broadcasted_iota(jnp.int32, (A_QK, A_QK), 1) // A_DK
    head_ones = jnp.where(hrow == hcol, 1.0, 0.0).astype(BF16)
    sub = lax.broadcasted_iota(jnp.int32, (SUBLANES, A_QK), 0)
    head_lanes = [slice(hh * A_DK, (hh + 1) * A_DK) for hh in range(A_HEADS)]
    nt = (((1,), (1,)), ((), ()))
    tn = (((0,), (0,)), ((), ()))

    def pairwise_phase(cc, b_scr, k_scr, iv_scr):
        r0 = cc * chunk
        q = q_ref[pl.ds(r0, chunk), :].astype(F32)
        zf = zf_ref[pl.ds(r0, chunk), :]
        iv = iv_ref[pl.ds(r0, chunk), :].astype(F32)
        g = g_ref[pl.ds(r0, chunk), :]
        f = lb + (1.0 - lb) * _sigmoid(zf)
        logf = jnp.log(f)
        kk = 1.0 - f
        hi, mid, lo = _split3(logf)
        b = (jnp.dot(tri, hi, preferred_element_type=F32)
             + jnp.dot(tri, mid, preferred_element_type=F32)
             + jnp.dot(tri, lo, preferred_element_type=F32))
        b_scr[...] = b
        k_scr[...] = kk
        iv_scr[...] = iv

        ws = []
        for i in range(n_blk):
            b_i = b[SUBLANES * i:SUBLANES * (i + 1)]
            q_i = q[SUBLANES * i:SUBLANES * (i + 1)]
            for s in range(SUBLANES):
                row = SUBLANES * i + s
                dlt = jnp.where(sub >= s, b_i - b_scr[row:row + 1, :], NEG_BIG)
                ws.append(jnp.exp(dlt) * q_i * k_scr[row:row + 1, :])
        w = jnp.concatenate(ws, axis=0)
        w = _split3(w) if zero_start else (w.astype(BF16),)
        return r0, q, iv, g, kk, b, w, b_scr, iv_scr

    def matmul_phase(r0, q, iv, g, kk, b, w, b_scr, iv_scr):
        att = sum(jnp.dot(part, head_ones, preferred_element_type=F32) for part in w)
        o_blocks = []
        for i in range(n_blk):
            acc = jnp.zeros((SUBLANES, A_WIDTH), F32)
            for s in range(SUBLANES):
                row = SUBLANES * i + s
                acc = acc + att[SUBLANES * row:SUBLANES * (row + 1)] * iv_scr[row:row + 1, :]
            o_blocks.append(acc)
        o = jnp.concatenate(o_blocks, axis=0)

        if n_blk > 1:
            qt, kh, ivs = [], [], []
            for i in range(1, n_blk):
                n_s = SUBLANES * i
                r_i = b_scr[n_s - 1:n_s, :]
                qt.append(q[n_s:n_s + SUBLANES] * jnp.exp(b[n_s:n_s + SUBLANES] - r_i))
                kh.append(kk[:n_s] * jnp.exp(r_i - b[:n_s]))
                ivs.append(iv[:n_s])
            qt = jnp.concatenate(qt, axis=0).astype(BF16)
            kh = jnp.concatenate(kh, axis=0).astype(BF16)
            ivs = jnp.concatenate(ivs, axis=0).astype(BF16)
            mask = mask_ref[...]
            o_off = []
            for l in head_lanes:
                a = lax.dot_general(qt[:, l], kh[:, l], nt, preferred_element_type=F32)
                o_off.append(jnp.dot((a * mask).astype(BF16), ivs[:, l], preferred_element_type=F32))
            o_off = jnp.concatenate(o_off, axis=1)
            o = o + jnp.concatenate([jnp.zeros((SUBLANES, A_WIDTH), F32), o_off], axis=0)

        qh = (q * jnp.exp(b)).astype(BF16)
        b_last = b[chunk - 1:chunk, :]
        kd = (kk * jnp.exp(b_last - b)).astype(BF16)
        dec = jnp.exp(b_last)
        ivb = iv.astype(BF16)
        o_inter = []
        for hh, l in enumerate(head_lanes):
            s_t = s_scr[hh]
            o_inter.append(lax.dot_general(qh[:, l], s_t.astype(BF16), nt, preferred_element_type=F32))
            upd = lax.dot_general(ivb[:, l], kd[:, l], tn, preferred_element_type=F32)
            s_scr[hh] = s_t * dec[:, l] + upd
        o = o + jnp.concatenate(o_inter, axis=1)

        o = jnp.concatenate([o[:, l] * _rms_scale(o[:, l]) * gn for l in head_lanes], axis=1)
        o = o * (g * _sigmoid(g))
        o_ref[pl.ds(r0, chunk), :] = o.astype(o_ref.dtype)

    pending = []
    for u in range(n_sub):
        pending.append(pairwise_phase(u, *row_scr[u]))
        if len(pending) > HGRN_LOOKAHEAD:
            matmul_phase(*pending.pop(0))
    for ctx in pending:
        matmul_phase(*ctx)

    @pl.when(step == pl.num_programs(1) - 1)
    def _():
        sout_ref[...] = s_scr[...]


def hgrn(q, zf, iv, g, lb, gnorm, s0_t, bsz, t_len, *, zero_start):
    chunk = HGRN_CHUNK if t_len % HGRN_CHUNK == 0 else SUBLANES
    step_rows = math.gcd(HGRN_STEP_ROWS, t_len)
    assert step_rows % chunk == 0, (t_len, step_rows, chunk)
    n_steps = t_len // step_rows
    tok = pl.BlockSpec((step_rows, A_QK), lambda b, c: (b * n_steps + c, 0))
    s0_b = (lambda b: b) if s0_t.shape[0] == bsz else (lambda b: 0)
    in_specs = [tok, tok, tok, tok,
                pl.BlockSpec((1, A_QK), lambda b, c: (0, 0)),
                pl.BlockSpec((1, A_DV), lambda b, c: (0, 0)),
                pl.BlockSpec((None, A_HEADS, A_DV, A_DK), lambda b, c: (s0_b(b), 0, 0, 0))]
    args = [q, zf, iv, g, lb.reshape(1, A_QK), gnorm.reshape(1, A_DV), s0_t]
    if chunk > SUBLANES:
        mask = jnp.asarray(_hgrn_offdiag_mask(chunk))
        in_specs.append(pl.BlockSpec(mask.shape, lambda b, c: (0, 0)))
        args.append(mask)
    return pl.pallas_call(
        functools.partial(_hgrn_kernel, chunk=chunk, n_sub=step_rows // chunk, zero_start=zero_start),
        grid=(bsz, n_steps),
        in_specs=in_specs,
        out_specs=[tok, pl.BlockSpec((None, A_HEADS, A_DV, A_DK), lambda b, c: (b, 0, 0, 0))],
        out_shape=[jax.ShapeDtypeStruct((bsz * t_len, A_WIDTH), BF16),
                   jax.ShapeDtypeStruct((bsz, A_HEADS, A_DV, A_DK), F32)],
        scratch_shapes=([pltpu.VMEM((A_HEADS, A_DV, A_DK), F32)]
                        + [pltpu.VMEM((chunk, A_QK), F32)] * (3 * (step_rows // chunk))),
        compiler_params=_cparams("parallel", "arbitrary"),
        name="hgrn",
    )(*args)


def s5_weights(a_re, a_im, log_dt, b_re, b_im, c_re, c_im, d_skip):
    f32 = F32
    ar = a_re.astype(f32)
    ai = a_im.astype(f32)
    dt = jnp.exp(log_dt.astype(f32))[:, None]
    mag = jnp.exp(dt * ar)
    abar_re = mag * jnp.cos(dt * ai)
    abar_im = mag * jnp.sin(dt * ai)
    den = ar * ar + ai * ai
    zr = ((abar_re - 1.0) * ar + abar_im * ai) / den
    zi = (abar_im * ar - (abar_re - 1.0) * ai) / den
    br, bi = b_re.astype(f32), b_im.astype(f32)
    bb_re = zr[..., None] * br - zi[..., None] * bi
    bb_im = zr[..., None] * bi + zi[..., None] * br
    eye = jnp.eye(S5_GROUPS, dtype=f32)

    def in_proj(bb):
        return jnp.einsum('gnp,gh->gphn', bb, eye).reshape(B_WIDTH, S5_NSTATE)

    def out_proj(c):
        return jnp.einsum('gpn,gh->gnhp', c, eye).reshape(S5_NSTATE, B_WIDTH)

    n_sb = B_WIDTH // LANES
    sw = S5_NSTATE // n_sb

    def diag_blocks(full, rows, cols):
        return jnp.stack([full[c * rows:(c + 1) * rows, c * cols:(c + 1) * cols] for c in range(n_sb)])

    b_blk = jnp.concatenate([diag_blocks(in_proj(bb_re), LANES, sw), diag_blocks(in_proj(bb_im), LANES, sw)],
                            axis=2).astype(BF16)
    c_blk = jnp.concatenate([diag_blocks(out_proj(c_re.astype(f32)), sw, LANES),
                             -diag_blocks(out_proj(c_im.astype(f32)), sw, LANES)],
                            axis=1).astype(BF16)
    return dict(b_blk=b_blk, c_blk=c_blk, a_re=abar_re.reshape(1, S5_NSTATE),
                a_im=abar_im.reshape(1, S5_NSTATE), d=d_skip.astype(f32).reshape(1, B_WIDTH))


def _s5_kernel(u_ref, x0re_ref, x0im_ref, bblk_ref, cblk_ref, are_ref, aim_ref, d_ref,
               y_ref, fre_ref, fim_ref, u_scr, bu_scr, xre_scr, xim_scr, *, tt, nb):
    step = pl.program_id(0)

    @pl.when(step == 0)
    def _():
        xre_scr[...] = x0re_ref[...]
        xim_scr[...] = x0im_ref[...]

    n_lt = B_WIDTH // LANES
    u_bt = u_ref[...].reshape(nb * tt, B_WIDTH).astype(F32)
    for c in range(n_lt):
        u_scr[c] = u_bt[:, c * LANES:(c + 1) * LANES]
    sw = S5_NSTATE // n_lt
    skips = []
    for c in range(n_lt):
        u_tb = jnp.concatenate([u_scr[c, pl.ds(t, nb, stride=tt), :] for t in range(tt)], axis=0)
        bu_scr[c] = jnp.dot(u_tb.astype(BF16), bblk_ref[c], preferred_element_type=F32)
        skips.append(d_ref[:, c * LANES:(c + 1) * LANES] * u_tb)

    for c0 in range(0, n_lt, S5_SCAN_BLOCKS):
        blocks = range(c0, c0 + S5_SCAN_BLOCKS)
        st_ls = [slice(c * sw, (c + 1) * sw) for c in blocks]
        a_res = [jnp.broadcast_to(are_ref[:, l], (nb, sw)) for l in st_ls]
        a_ims = [jnp.broadcast_to(aim_ref[:, l], (nb, sw)) for l in st_ls]

        def body(t, st, blocks=blocks, a_res=a_res, a_ims=a_ims):
            r0 = pl.multiple_of(t * nb, nb)
            out = []
            for c, a_re, a_im, (x_re, x_im) in zip(blocks, a_res, a_ims, st):
                n_re = a_re * x_re - a_im * x_im + bu_scr[c, pl.ds(r0, nb), :sw]
                n_im = a_re * x_im + a_im * x_re + bu_scr[c, pl.ds(r0, nb), sw:]
                bu_scr[c, pl.ds(r0, nb), :sw] = n_re
                bu_scr[c, pl.ds(r0, nb), sw:] = n_im
                out.append((n_re, n_im))
            return tuple(out)

        fin = lax.fori_loop(0, tt, body, tuple((xre_scr[:, l], xim_scr[:, l]) for l in st_ls))
        for l, (x_re, x_im) in zip(st_ls, fin):
            xre_scr[:, l] = x_re
            xim_scr[:, l] = x_im

    for c in range(n_lt):
        y = jnp.dot(bu_scr[c].astype(BF16), cblk_ref[c], preferred_element_type=F32) + skips[c]
        y = 0.5 * y * (1.0 + lax.erf(y * (1.0 / math.sqrt(2.0))))
        for t in range(tt):
            u_scr[c, pl.ds(t, nb, stride=tt), :] = y[nb * t:nb * (t + 1)]
    y_bt = jnp.concatenate([u_scr[c] for c in range(n_lt)], axis=1).reshape(nb, tt, B_WIDTH)
    y_ref[...] = y_bt.astype(y_ref.dtype)

    @pl.when(step == pl.num_programs(0) - 1)
    def _():
        fre_ref[...] = xre_scr[...]
        fim_ref[...] = xim_scr[...]


def s5(u, wts, x0_re, x0_im, bsz, t_len):
    nb = SUBLANES
    tt = min(S5_STEP_T, t_len)
    u3 = u.reshape(bsz, t_len, B_WIDTH)
    x0 = [x.astype(F32).reshape(bsz, S5_NSTATE) for x in (x0_re, x0_im)]
    if bsz < nb:
        u3 = jnp.pad(u3, ((0, nb - bsz), (0, 0), (0, 0)))
        x0 = [jnp.pad(x, ((0, nb - bsz), (0, 0))) for x in x0]
    fixed = lambda i: (0, 0)
    fixed3 = lambda i: (0, 0, 0)
    n_sb = B_WIDTH // LANES
    sw2 = 2 * S5_NSTATE // n_sb
    tok = pl.BlockSpec((nb, tt, B_WIDTH), lambda i: (0, i, 0))
    state = pl.BlockSpec((nb, S5_NSTATE), fixed)
    yg, fre, fim = pl.pallas_call(
        functools.partial(_s5_kernel, tt=tt, nb=nb),
        grid=(t_len // tt,),
        in_specs=[tok, state, state,
                  pl.BlockSpec((n_sb, LANES, sw2), fixed3), pl.BlockSpec((n_sb, sw2, LANES), fixed3),
                  pl.BlockSpec((1, S5_NSTATE), fixed), pl.BlockSpec((1, S5_NSTATE), fixed),
                  pl.BlockSpec((1, B_WIDTH), fixed)],
        out_specs=[tok, state, state],
        out_shape=[jax.ShapeDtypeStruct((nb, t_len, B_WIDTH), u.dtype),
                   jax.ShapeDtypeStruct((nb, S5_NSTATE), F32), jax.ShapeDtypeStruct((nb, S5_NSTATE), F32)],
        scratch_shapes=[pltpu.VMEM((n_sb, nb * tt, LANES), F32),
                        pltpu.VMEM((n_sb, nb * tt, sw2), F32),
                        pltpu.VMEM((nb, S5_NSTATE), F32), pltpu.VMEM((nb, S5_NSTATE), F32)],
        compiler_params=_cparams("arbitrary"),
        name="s5",
    )(u3, x0[0], x0[1], wts['b_blk'], wts['c_blk'], wts['a_re'], wts['a_im'], wts['d'])
    yg = yg[:bsz].reshape(bsz * t_len, B_WIDTH)
    return (yg, fre[:bsz].reshape(bsz, S5_GROUPS, S5_STATE), fim[:bsz].reshape(bsz, S5_GROUPS, S5_STATE))


def _sb_attn_kernel(*refs, heads, n_past, p_valid, scale):
    if n_past:
        q_ref, k_ref, v_ref, pk_ref, pv_ref, o_ref, carry_scr, acc_scr = refs
    else:
        q_ref, k_ref, v_ref, o_ref, carry_scr, acc_scr = refs
    blk = ATT_BLOCK
    i = pl.program_id(1)

    def suffix_matrix(tk):
        srow = lax.broadcasted_iota(jnp.int32, (2 * tk, tk), 0)
        scol = lax.broadcasted_iota(jnp.int32, (2 * tk, tk), 1)
        return jnp.where(jnp.where(srow >= tk, srow - tk, srow) > scol, 1.0, 0.0).astype(BF16)

    suffix = {tk: suffix_matrix(tk) for tk in (blk, ATT_SLAB)}
    rows = lax.broadcasted_iota(jnp.int32, (blk, blk), 0)
    cols = lax.broadcasted_iota(jnp.int32, (blk, blk), 1)

    carry_scr[...] = jnp.zeros_like(carry_scr)
    acc_scr[...] = jnp.zeros_like(acc_scr)
    head_lanes = [slice(hh * C_HD, (hh + 1) * C_HD) for hh in range(heads)]
    n_rows = heads * blk

    def per_head(x, fn):
        return jnp.concatenate([fn(x[hh * blk:(hh + 1) * blk]) for hh in range(heads)], axis=0)

    def rows_of(ref, rj, tk, hh):
        if len(ref.shape) == 2:
            return ref[pl.ds(rj, tk), head_lanes[hh]]
        return ref[pl.ds(rj, tk), hh, :].astype(BF16)

    def visit(kr, vr, rj, specs):
        zs = []
        for s, (tk, _) in enumerate(specs):
            q_rows = slice(s * blk, (s + 1) * blk)
            zs.append(jnp.concatenate(
                [lax.dot_general(q_ref[q_rows, l], rows_of(kr, rj, tk, hh), (((1,), (1,)), ((), ())),
                                 preferred_element_type=F32) for hh, l in enumerate(head_lanes)], axis=0) * scale)
        mids = []
        for z, (tk, mask) in zip(zs, specs):
            log_b = jnp.minimum(z, 0.0) - jnp.log(1.0 + jnp.exp(-jnp.abs(z)))
            x = log_b - z
            if mask is not None:
                x = per_head(x, lambda xh, mask=mask: jnp.where(mask, xh, 0.0))
            hi = x.astype(BF16)
            lo = (x - hi.astype(F32)).astype(BF16)
            cs = jnp.dot(jnp.concatenate([hi, lo], axis=1), suffix[tk], preferred_element_type=F32)
            total = jnp.broadcast_to(cs[:, 0:1] + x[:, 0:1], (n_rows, blk))
            mids.append((log_b, cs, total))
        live = None
        for s, ((log_b, cs, total), (tk, mask)) in enumerate(zip(mids, specs)):
            carry = carry_scr[s]
            w = jnp.exp(log_b + cs + jnp.concatenate([carry] * (tk // blk), axis=1))
            if mask is not None:
                w = per_head(w, lambda wh, mask=mask: jnp.where(mask, wh, 0.0))
            w = w.astype(BF16)
            acc_scr[s] += jnp.concatenate(
                [jnp.dot(w[hh * blk:(hh + 1) * blk], rows_of(vr, rj, tk, hh),
                         preferred_element_type=F32) for hh in range(heads)], axis=0)
            carry = carry + total
            carry_scr[s] = carry
            m = jnp.max(carry)
            live = m if live is None else jnp.maximum(live, m)
        return live

    rows2 = lax.broadcasted_iota(jnp.int32, (blk, ATT_SLAB), 0)
    cols2 = lax.broadcasted_iota(jnp.int32, (blk, ATT_SLAB), 1)
    live = visit(k_ref, v_ref, pl.multiple_of(i * ATT_SLAB, ATT_SLAB),
                 [(blk, cols < rows), (ATT_SLAB, cols2 < rows2 + blk)])

    def slab_body(st):
        j, _ = st
        return j - 2, visit(k_ref, v_ref, pl.multiple_of((j - 1) * blk, blk), [(ATT_SLAB, None)] * 2)

    def alive_from(first):
        return lambda st: jnp.logical_and(st[0] >= first, st[1] > ATT_DEAD_LOG_WEIGHT)

    _, live = lax.while_loop(alive_from(1), slab_body, (2 * i - 1, live))

    if n_past:
        def past_body(st):
            j, _ = st
            return j - 1, visit(pk_ref, pv_ref, pl.multiple_of(j * blk, blk),
                                [(blk, (cols + j * blk) < p_valid)] * 2)

        lax.while_loop(alive_from(0), past_body, (jnp.int32(n_past - 1), live))

    for s in range(2):
        for hh in range(heads):
            o_ref[s * blk:(s + 1) * blk, head_lanes[hh]] = (
                acc_scr[s, hh * blk:(hh + 1) * blk, :].astype(o_ref.dtype))


def sb_attention(q, k, v, past_k, past_v, p_valid):
    bsz, t_len, width = q.shape
    n_past = 0 if past_k is None else past_k.shape[1] // ATT_BLOCK
    q_spec = pl.BlockSpec((None, ATT_SLAB, width), lambda b, i: (b, i, 0))
    kv_spec = pl.BlockSpec((None, t_len, width), lambda b, i: (b, 0, 0))
    in_specs = [q_spec, kv_spec, kv_spec]
    args = [q, k, v]
    if n_past:
        pb = (lambda b: b) if past_k.shape[0] == bsz else (lambda b: 0)
        p_spec = pl.BlockSpec((None,) + past_k.shape[1:], lambda b, i: (pb(b), 0, 0, 0))
        in_specs += [p_spec, p_spec]
        args += [past_k, past_v]
    return pl.pallas_call(
        functools.partial(_sb_attn_kernel, heads=C_HEADS, n_past=n_past, p_valid=p_valid,
                          scale=C_HD ** -0.5),
        grid=(bsz, t_len // ATT_SLAB),
        in_specs=in_specs,
        out_specs=q_spec,
        out_shape=jax.ShapeDtypeStruct((bsz, t_len, width), BF16),
        scratch_shapes=[pltpu.VMEM((2, C_HEADS * ATT_BLOCK, C_HD), F32)] * 2,
        compiler_params=_cparams("parallel", "arbitrary"),
        name="sb_attention",
    )(*args)


def _pad_rows(x, mult):
    t = x.shape[1]
    tp = -(-t // mult) * mult
    return x if tp == t else jnp.pad(x, ((0, 0), (0, tp - t), (0, 0)))


def _run_stream(x, bsz, t_len, wts, hgrn_s0_t, ssm0_re, ssm0_im, past_k, past_v, p_valid, lead_kv=None,
                starts_sequence=False):
    act = BF16 if t_len % HGRN_CHUNK == 0 else F32
    q, zf, iv, g, u = norm_matmul(
        x, wts['ln_mix'][0], wts['w_in_even'],
        [(0, A_QK, act), (A_QK, A_QK, F32), (2 * A_QK, A_WIDTH, act),
         (2 * A_QK + A_WIDTH, A_WIDTH, F32), (2 * A_QK + 2 * A_WIDTH, B_WIDTH, act)], tm=1024)
    o_a, s_t = hgrn(q, zf, iv, g, wts['lb'], wts['hgrn_norm'], hgrn_s0_t, bsz, t_len,
                    zero_start=starts_sequence)
    yg, x_re, x_im = s5(u, wts['s5'], ssm0_re, ssm0_im, bsz, t_len)
    x = layer_tail(x, [o_a, yg], [wts['w_glu'], wts['w_out_even_a'], wts['w_out_even_b']],
                   wts['ln_mlp'][0], wts['w_up'][0], wts['w_down'][0], wts['ln_final'], final_norm=False)
    if lead_kv is None:
        qb, kb, vb, k32, v32 = norm_matmul(
            x, wts['ln_mix'][1], wts['w_in_odd'],
            [(0, D_MODEL, BF16), (D_MODEL, D_MODEL, BF16), (2 * D_MODEL, D_MODEL, BF16),
             (D_MODEL, D_MODEL, F32), (2 * D_MODEL, D_MODEL, F32)])
    else:
        qb, kb, vb, k32, v32 = qkv_shifted(x, wts['ln_mix'][1], wts['w_in_odd'], *lead_kv, bsz, t_len)
    shp = (bsz, t_len, D_MODEL)
    o = sb_attention(_pad_rows(qb.reshape(shp), ATT_SLAB), _pad_rows(kb.reshape(shp), ATT_SLAB),
                     _pad_rows(vb.reshape(shp), ATT_SLAB), past_k, past_v, p_valid)
    o = o[:, :t_len].reshape(bsz * t_len, D_MODEL)
    y = layer_tail(x, [o], [wts['w_out_odd']],
                   wts['ln_mlp'][1], wts['w_up'][1], wts['w_down'][1], wts['ln_final'], final_norm=True)
    return y, s_t, x_re, x_im, k32, v32


def kernel(x_prompt, x_sample, state_hgrn, state_ssm_re, state_ssm_im, cache_k, cache_v, meta_tokens,
           ln_mix, ln_mlp, ln_final, w_in_even, hgrn_lb, hgrn_norm, ssm_a_re, ssm_a_im, ssm_log_dt,
           ssm_b_re, ssm_b_im, ssm_c_re, ssm_c_im, ssm_d, w_glu, w_out_even, w_in_odd, w_out_odd,
           w_up, w_down):
    bsz, seq, _ = x_prompt.shape
    dbsz, dseq, _ = x_sample.shape
    past = cache_k.shape[2]
    lb_all = jnp.cumsum(jax.nn.softmax(hgrn_lb.astype(F32), axis=0), axis=0)
    w_out_e = w_out_even[0].astype(BF16)
    wts = dict(
        ln_mix=ln_mix.astype(F32), ln_mlp=ln_mlp.astype(F32), ln_final=ln_final.astype(F32),
        w_in_even=w_in_even[0].astype(BF16), lb=lb_all[0], hgrn_norm=hgrn_norm[0].astype(F32),
        s5=s5_weights(ssm_a_re[0], ssm_a_im[0], ssm_log_dt[0], ssm_b_re[0], ssm_b_im[0],
                      ssm_c_re[0], ssm_c_im[0], ssm_d[0]),
        w_glu=w_glu[0].astype(BF16), w_out_even_a=w_out_e[:A_WIDTH], w_out_even_b=w_out_e[A_WIDTH:],
        w_in_odd=w_in_odd[0].astype(BF16), w_out_odd=w_out_odd[0].astype(BF16),
        w_up=w_up.astype(BF16), w_down=w_down.astype(BF16))

    zeros_s = jnp.zeros((1, A_HEADS, A_DV, A_DK), F32)
    zeros_x = jnp.zeros((1, S5_GROUPS, S5_STATE), F32)
    _, m_s, m_re, m_im, m_k, m_v = _run_stream(
        meta_tokens.astype(F32), 1, N_META, wts, zeros_s, zeros_x, zeros_x, None, None, 0, starts_sequence=True)

    m_kp = _pad_rows(m_k[None], ATT_BLOCK).reshape(1, -1, C_HEADS, C_HD)
    m_vp = _pad_rows(m_v[None], ATT_BLOCK).reshape(1, -1, C_HEADS, C_HD)
    y_p, s_p, re_p, im_p, k_p, v_p = _run_stream(
        x_prompt.reshape(bsz * seq, D_MODEL), bsz, seq, wts, m_s,
        jnp.broadcast_to(m_re, (bsz, S5_GROUPS, S5_STATE)), jnp.broadcast_to(m_im, (bsz, S5_GROUPS, S5_STATE)),
        m_kp, m_vp, N_META, lead_kv=(m_k, m_v))

    pad_past = ((0, 0), (0, -past % ATT_BLOCK), (0, 0), (0, 0))
    y_s, s_s, re_s, im_s, k_s, v_s = _run_stream(
        x_sample.reshape(dbsz * dseq, D_MODEL), dbsz, dseq, wts,
        jnp.swapaxes(state_hgrn[0].astype(F32), -1, -2), state_ssm_re[0], state_ssm_im[0],
        jnp.pad(cache_k[0].astype(F32), pad_past), jnp.pad(cache_v[0].astype(F32), pad_past), past)

    kv_shape = (1, bsz, N_META + seq, C_HEADS, C_HD)
    return (y_p.reshape(bsz, seq, D_MODEL), y_s.reshape(dbsz, dseq, D_MODEL),
            jnp.swapaxes(s_p, -1, -2)[None], jnp.swapaxes(s_s, -1, -2)[None],
            re_p[None], im_p[None], re_s[None], im_s[None],
            k_p.reshape(kv_shape), v_p.reshape(kv_shape),
            k_s.reshape(1, dbsz, dseq, C_HEADS, C_HD), v_s.reshape(1, dbsz, dseq, C_HEADS, C_HD))
```

```python
import functools
import math

import numpy as np
import jax
import jax.numpy as jnp
from jax import lax
from jax.experimental import pallas as pl
from jax.experimental.pallas import tpu as pltpu

F32 = jnp.float32
BF16 = jnp.bfloat16

D_MODEL = 1024
N_META = 16
A_HEADS = 4
A_DK = 128
A_DV = 128
A_WIDTH = 512
A_QK = A_HEADS * A_DK
B_WIDTH = 512
S5_GROUP = 16
S5_GROUPS = 32
S5_STATE = 64
S5_NSTATE = S5_GROUPS * S5_STATE
C_HEADS = 8
C_HD = 128
D_FF = 4 * D_MODEL
EPS = 1e-6

SUBLANES = 8
LANES = 128
VMEM_LIMIT_BYTES = 56 * 1024 * 1024

HGRN_CHUNK = 64
HGRN_STEP_ROWS = 512
HGRN_LOOKAHEAD = 2
LAYER_TAIL_SUB_ROWS = 256
S5_STEP_T = 64
S5_SCAN_BLOCKS = 2
ATT_BLOCK = 128
ATT_SLAB = 256
ATT_DEAD_LOG_WEIGHT = -104.0
NEG_BIG = -1e30


def _cparams(*sem):
    return pltpu.CompilerParams(dimension_semantics=sem, vmem_limit_bytes=VMEM_LIMIT_BYTES)


def _sigmoid(x):
    return 1.0 / (1.0 + jnp.exp(-x))


def _rms_scale(x):
    return lax.rsqrt(jnp.mean(x * x, axis=-1, keepdims=True) + EPS)


def _norm_matmul_kernel(x_ref, g_ref, w_ref, *out_refs, cols, nchunk):
    x = x_ref[...]
    h = (x * _rms_scale(x) * g_ref[...]).astype(BF16)
    for c0, width in sorted(set(cols)):
        for c in range(0, width, nchunk):
            res = jnp.dot(h, w_ref[:, c0 + c:c0 + c + nchunk], preferred_element_type=F32)
            for o_ref, col in zip(out_refs, cols):
                if col == (c0, width):
                    o_ref[:, c:c + nchunk] = res.astype(o_ref.dtype)


def norm_matmul(x, g, w_bf16, outs, *, tm=512, nchunk=512):
    m, d = x.shape
    tm = min(tm, m)
    kern = functools.partial(_norm_matmul_kernel, cols=[(c0, wd) for c0, wd, _ in outs], nchunk=nchunk)
    return pl.pallas_call(
        kern,
        grid=(pl.cdiv(m, tm),),
        in_specs=[pl.BlockSpec((tm, d), lambda i: (i, 0)),
                  pl.BlockSpec((1, d), lambda i: (0, 0)),
                  pl.BlockSpec(w_bf16.shape, lambda i: (0, 0))],
        out_specs=[pl.BlockSpec((tm, wd), lambda i: (i, 0)) for _, wd, _ in outs],
        out_shape=[jax.ShapeDtypeStruct((m, wd), dt) for _, wd, dt in outs],
        compiler_params=_cparams("parallel"),
        name="norm_matmul",
    )(x, g.reshape(1, d), w_bf16)


def _qkv_shifted_kernel(x_ref, g_ref, w_ref, lead_ref, qb_ref, kb_ref, vb_ref, kf_ref, vf_ref, carry_scr,
                        *, n_lead, nchunk):
    t = pl.program_id(1)
    nt = pl.num_programs(1) - 1
    tm, width = qb_ref.shape

    @pl.when(t == 0)
    def _():
        carry_scr[...] = lead_ref[...]

    for n, f_ref in enumerate((kf_ref, vf_ref)):
        f_ref[:n_lead, :] = carry_scr[n]

    @pl.when(t < nt)
    def _():
        x = x_ref[...]
        h = (x * _rms_scale(x) * g_ref[...]).astype(BF16)
        for n, (b_ref, f_ref) in enumerate(((qb_ref, None), (kb_ref, kf_ref), (vb_ref, vf_ref))):
            for c in range(0, width, nchunk):
                res = jnp.dot(h, w_ref[:, n * width + c:n * width + c + nchunk], preferred_element_type=F32)
                b_ref[:, c:c + nchunk] = res.astype(b_ref.dtype)
                if f_ref is not None:
                    f_ref[n_lead:, c:c + nchunk] = res[:tm - n_lead]
                    carry_scr[n - 1, :, c:c + nchunk] = res[tm - n_lead:]


def qkv_shifted(x, g, w_bf16, lead_k, lead_v, bsz, t_len, *, tm=1024, nchunk=512):
    m, d = x.shape
    width = w_bf16.shape[1] // 3
    n_lead = lead_k.shape[0]
    tm = math.gcd(tm, t_len)
    nt = t_len // tm
    tok = lambda b, t: (b * nt + jnp.minimum(t, nt - 1), 0)
    return pl.pallas_call(
        functools.partial(_qkv_shifted_kernel, n_lead=n_lead, nchunk=nchunk),
        grid=(bsz, nt + 1),
        in_specs=[pl.BlockSpec((tm, d), tok),
                  pl.BlockSpec((1, d), lambda b, t: (0, 0)),
                  pl.BlockSpec(w_bf16.shape, lambda b, t: (0, 0)),
                  pl.BlockSpec((2, n_lead, width), lambda b, t: (0, 0, 0))],
        out_specs=[pl.BlockSpec((tm, width), tok)] * 3
                  + [pl.BlockSpec((None, tm, width), lambda b, t: (b, t, 0))] * 2,
        out_shape=[jax.ShapeDtypeStruct((m, width), BF16)] * 3
                  + [jax.ShapeDtypeStruct((bsz, n_lead + t_len, width), F32)] * 2,
        scratch_shapes=[pltpu.VMEM((2, n_lead, width), F32)],
        compiler_params=_cparams("parallel", "arbitrary"),
        name="qkv_shifted",
    )(x, g.reshape(1, d), w_bf16, jnp.stack([lead_k, lead_v]))


def _layer_tail_kernel(*refs, n_mix, final_norm):
    x_ref = refs[0]
    mix = refs[1:1 + n_mix]
    g_ref, wup_ref, wdn_ref, gf_ref, o_ref, h_scr = refs[1 + n_mix:]
    j = pl.program_id(1)

    @pl.when(j == 0)
    def _():
        tm = x_ref.shape[0]
        sub = math.gcd(tm, LAYER_TAIL_SUB_ROWS)
        def finish(rs, x1):
            h_scr[rs, :] = (x1 * _rms_scale(x1) * g_ref[...]).astype(BF16)
            o_ref[rs, :] = x1

        pending = None
        for r0 in range(0, tm, sub):
            rs = slice(r0, r0 + sub)
            if n_mix == 5:
                oa_ref, yg_ref, wglu_ref, wa_ref, wb_ref = mix
                yg = yg_ref[rs, :]
                gate = _sigmoid(jnp.dot(yg.astype(BF16), wglu_ref[...], preferred_element_type=F32))
                ob = (yg.astype(F32) * gate).astype(BF16)
                x1 = (x_ref[rs, :] + jnp.dot(oa_ref[rs, :], wa_ref[...], preferred_element_type=F32)
                      + jnp.dot(ob, wb_ref[...], preferred_element_type=F32))
            else:
                om_ref, wo_ref = mix
                x1 = x_ref[rs, :] + jnp.dot(om_ref[rs, :], wo_ref[...], preferred_element_type=F32)
            if pending is not None:
                finish(*pending)
            pending = (rs, x1)
        finish(*pending)

    a = jnp.dot(h_scr[...], wup_ref[...], preferred_element_type=F32)
    a = jnp.square(jnp.maximum(a, 0.0)).astype(BF16)
    o_ref[...] += jnp.dot(a, wdn_ref[...], preferred_element_type=F32)

    if final_norm:
        @pl.when(j == pl.num_programs(1) - 1)
        def _():
            y = o_ref[...]
            o_ref[...] = y * _rms_scale(y) * gf_ref[...]


def layer_tail(x, mix_rows, mix_weights, g, w_up, w_down, g_final, *, final_norm, tm=1024, tf=2048):
    m, d = x.shape
    tm = min(tm, m)
    ff = w_up.shape[1]
    row = lambda i, j: (i, 0)
    fixed = lambda i, j: (0, 0)
    once = dict(pipeline_mode=pl.Buffered(1))
    return pl.pallas_call(
        functools.partial(_layer_tail_kernel, n_mix=len(mix_rows) + len(mix_weights), final_norm=final_norm),
        grid=(pl.cdiv(m, tm), ff // tf),
        in_specs=([pl.BlockSpec((tm, d), row)]
                  + [pl.BlockSpec((tm, r.shape[1]), row) for r in mix_rows]
                  + [pl.BlockSpec(w.shape, fixed, **once) for w in mix_weights]
                  + [pl.BlockSpec((1, d), fixed),
                     pl.BlockSpec((d, tf), lambda i, j: (0, j)),
                     pl.BlockSpec((tf, d), lambda i, j: (j, 0)),
                     pl.BlockSpec((1, d), fixed)]),
        out_specs=pl.BlockSpec((tm, d), row),
        out_shape=jax.ShapeDtypeStruct((m, d), F32),
        scratch_shapes=[pltpu.VMEM((tm, d), BF16)],
        compiler_params=_cparams("parallel", "arbitrary"),
        name="layer_tail",
    )(x, *mix_rows, *mix_weights, g.reshape(1, d), w_up, w_down, g_final.reshape(1, d))


def _split3(x):
    hi = x.astype(BF16)
    r1 = x - hi.astype(F32)
    mid = r1.astype(BF16)
    lo = (r1 - mid.astype(F32)).astype(BF16)
    return hi, mid, lo


def _hgrn_offdiag_mask(chunk):
    n_blk = chunk // SUBLANES
    seg = np.concatenate([np.full(SUBLANES * i, i) for i in range(1, n_blk)])
    blk = np.repeat(np.arange(1, n_blk), SUBLANES)
    return (blk[:, None] == seg[None, :]).astype(np.float32)


def _hgrn_kernel(*refs, chunk, n_sub, zero_start):
    n_blk = chunk // SUBLANES
    row_scr = [refs[len(refs) - 3 * (n_sub - u):len(refs) - 3 * (n_sub - u - 1)] for u in range(n_sub)]
    refs = refs[:-3 * n_sub]
    if n_blk > 1:
        q_ref, zf_ref, iv_ref, g_ref, lb_ref, gn_ref, s0_ref, mask_ref, o_ref, sout_ref, s_scr = refs
    else:
        q_ref, zf_ref, iv_ref, g_ref, lb_ref, gn_ref, s0_ref, o_ref, sout_ref, s_scr = refs
    step = pl.program_id(1)

    @pl.when(step == 0)
    def _():
        s_scr[...] = s0_ref[...]

    lb = lb_ref[...]
    gn = gn_ref[...]
    rows = lax.broadcasted_iota(jnp.int32, (chunk, chunk), 0)
    cols = lax.broadcasted_iota(jnp.int32, (chunk, chunk), 1)
    tri = jnp.where(rows >= cols, 1.0, 0.0).astype(BF16)
    hrow = lax.broadcasted_iota(jnp.int32, (A_QK, A_QK), 0) // A_DK
    hcol = lax.broadcasted_iota(jnp.int32, (A_QK, A_QK), 1) // A_DK
    head_ones = jnp.where(hrow == hcol, 1.0, 0.0).astype(BF16)
    sub = lax.broadcasted_iota(jnp.int32, (SUBLANES, A_QK), 0)
    head_lanes = [slice(hh * A_DK, (hh + 1) * A_DK) for hh in range(A_HEADS)]
    nt = (((1,), (1,)), ((), ()))
    tn = (((0,), (0,)), ((), ()))

    def pairwise_phase(cc, b_scr, k_scr, iv_scr):
        r0 = cc * chunk
        q = q_ref[pl.ds(r0, chunk), :].astype(F32)
        zf = zf_ref[pl.ds(r0, chunk), :]
        iv = iv_ref[pl.ds(r0, chunk), :].astype(F32)
        g = g_ref[pl.ds(r0, chunk), :]
        f = lb + (1.0 - lb) * _sigmoid(zf)
        logf = jnp.log(f)
        kk = 1.0 - f
        hi, mid, lo = _split3(logf)
        b = (jnp.dot(tri, hi, preferred_element_type=F32)
             + jnp.dot(tri, mid, preferred_element_type=F32)
             + jnp.dot(tri, lo, preferred_element_type=F32))
        b_scr[...] = b
        k_scr[...] = kk
        iv_scr[...] = iv

        ws = []
        for i in range(n_blk):
            b_i = b[SUBLANES * i:SUBLANES * (i + 1)]
            q_i = q[SUBLANES * i:SUBLANES * (i + 1)]
            for s in range(SUBLANES):
                row = SUBLANES * i + s
                dlt = jnp.where(sub >= s, b_i - b_scr[row:row + 1, :], NEG_BIG)
                ws.append(jnp.exp(dlt) * q_i * k_scr[row:row + 1, :])
        w = jnp.concatenate(ws, axis=0)
        w = _split3(w) if zero_start else (w.astype(BF16),)
        return r0, q, iv, g, kk, b, w, b_scr, iv_scr

    def matmul_phase(r0, q, iv, g, kk, b, w, b_scr, iv_scr):
        att = sum(jnp.dot(part, head_ones, preferred_element_type=F32) for part in w)
        o_blocks = []
        for i in range(n_blk):
            acc = jnp.zeros((SUBLANES, A_WIDTH), F32)
            for s in range(SUBLANES):
                row = SUBLANES * i + s
                acc = acc + att[SUBLANES * row:SUBLANES * (row + 1)] * iv_scr[row:row + 1, :]
            o_blocks.append(acc)
        o = jnp.concatenate(o_blocks, axis=0)

        if n_blk > 1:
            qt, kh, ivs = [], [], []
            for i in range(1, n_blk):
                n_s = SUBLANES * i
                r_i = b_scr[n_s - 1:n_s, :]
                qt.append(q[n_s:n_s + SUBLANES] * jnp.exp(b[n_s:n_s + SUBLANES] - r_i))
                kh.append(kk[:n_s] * jnp.exp(r_i - b[:n_s]))
                ivs.append(iv[:n_s])
            qt = jnp.concatenate(qt, axis=0).astype(BF16)
            kh = jnp.concatenate(kh, axis=0).astype(BF16)
            ivs = jnp.concatenate(ivs, axis=0).astype(BF16)
            mask = mask_ref[...]
            o_off = []
            for l in head_lanes:
                a = lax.dot_general(qt[:, l], kh[:, l], nt, preferred_element_type=F32)
                o_off.append(jnp.dot((a * mask).astype(BF16), ivs[:, l], preferred_element_type=F32))
            o_off = jnp.concatenate(o_off, axis=1)
            o = o + jnp.concatenate([jnp.zeros((SUBLANES, A_WIDTH), F32), o_off], axis=0)

        qh = (q * jnp.exp(b)).astype(BF16)
        b_last = b[chunk - 1:chunk, :]
        kd = (kk * jnp.exp(b_last - b)).astype(BF16)
        dec = jnp.exp(b_last)
        ivb = iv.astype(BF16)
        o_inter = []
        for hh, l in enumerate(head_lanes):
            s_t = s_scr[hh]
            o_inter.append(lax.dot_general(qh[:, l], s_t.astype(BF16), nt, preferred_element_type=F32))
            upd = lax.dot_general(ivb[:, l], kd[:, l], tn, preferred_element_type=F32)
            s_scr[hh] = s_t * dec[:, l] + upd
        o = o + jnp.concatenate(o_inter, axis=1)

        o = jnp.concatenate([o[:, l] * _rms_scale(o[:, l]) * gn for l in head_lanes], axis=1)
        o = o * (g * _sigmoid(g))
        o_ref[pl.ds(r0, chunk), :] = o.astype(o_ref.dtype)

    pending = []
    for u in range(n_sub):
        pending.append(pairwise_phase(u, *row_scr[u]))
        if len(pending) > HGRN_LOOKAHEAD:
            matmul_phase(*pending.pop(0))
    for ctx in pending:
        matmul_phase(*ctx)

    @pl.when(step == pl.num_programs(1) - 1)
    def _():
        sout_ref[...] = s_scr[...]


def hgrn(q, zf, iv, g, lb, gnorm, s0_t, bsz, t_len, *, zero_start):
    chunk = HGRN_CHUNK if t_len % HGRN_CHUNK == 0 else SUBLANES
    step_rows = math.gcd(HGRN_STEP_ROWS, t_len)
    assert step_rows % chunk == 0, (t_len, step_rows, chunk)
    n_steps = t_len // step_rows
    tok = pl.BlockSpec((step_rows, A_QK), lambda b, c: (b * n_steps + c, 0))
    s0_b = (lambda b: b) if s0_t.shape[0] == bsz else (lambda b: 0)
    in_specs = [tok, tok, tok, tok,
                pl.BlockSpec((1, A_QK), lambda b, c: (0, 0)),
                pl.BlockSpec((1, A_DV), lambda b, c: (0, 0)),
                pl.BlockSpec((None, A_HEADS, A_DV, A_DK), lambda b, c: (s0_b(b), 0, 0, 0))]
    args = [q, zf, iv, g, lb.reshape(1, A_QK), gnorm.reshape(1, A_DV), s0_t]
    if chunk > SUBLANES:
        mask = jnp.asarray(_hgrn_offdiag_mask(chunk))
        in_specs.append(pl.BlockSpec(mask.shape, lambda b, c: (0, 0)))
        args.append(mask)
    return pl.pallas_call(
        functools.partial(_hgrn_kernel, chunk=chunk, n_sub=step_rows // chunk, zero_start=zero_start),
        grid=(bsz, n_steps),
        in_specs=in_specs,
        out_specs=[tok, pl.BlockSpec((None, A_HEADS, A_DV, A_DK), lambda b, c: (b, 0, 0, 0))],
        out_shape=[jax.ShapeDtypeStruct((bsz * t_len, A_WIDTH), BF16),
                   jax.ShapeDtypeStruct((bsz, A_HEADS, A_DV, A_DK), F32)],
        scratch_shapes=([pltpu.VMEM((A_HEADS, A_DV, A_DK), F32)]
                        + [pltpu.VMEM((chunk, A_QK), F32)] * (3 * (step_rows // chunk))),
        compiler_params=_cparams("parallel", "arbitrary"),
        name="hgrn",
    )(*args)


def s5_weights(a_re, a_im, log_dt, b_re, b_im, c_re, c_im, d_skip):
    f32 = F32
    ar = a_re.astype(f32)
    ai = a_im.astype(f32)
    dt = jnp.exp(log_dt.astype(f32))[:, None]
    mag = jnp.exp(dt * ar)
    abar_re = mag * jnp.cos(dt * ai)
    abar_im = mag * jnp.sin(dt * ai)
    den = ar * ar + ai * ai
    zr = ((abar_re - 1.0) * ar + abar_im * ai) / den
    zi = (abar_im * ar - (abar_re - 1.0) * ai) / den
    br, bi = b_re.astype(f32), b_im.astype(f32)
    bb_re = zr[..., None] * br - zi[..., None] * bi
    bb_im = zr[..., None] * bi + zi[..., None] * br
    eye = jnp.eye(S5_GROUPS, dtype=f32)

    def in_proj(bb):
        return jnp.einsum('gnp,gh->gphn', bb, eye).reshape(B_WIDTH, S5_NSTATE)

    def out_proj(c):
        return jnp.einsum('gpn,gh->gnhp', c, eye).reshape(S5_NSTATE, B_WIDTH)

    n_sb = B_WIDTH // LANES
    sw = S5_NSTATE // n_sb

    def diag_blocks(full, rows, cols):
        return jnp.stack([full[c * rows:(c + 1) * rows, c * cols:(c + 1) * cols] for c in range(n_sb)])

    b_blk = jnp.concatenate([diag_blocks(in_proj(bb_re), LANES, sw), diag_blocks(in_proj(bb_im), LANES, sw)],
                            axis=2).astype(BF16)
    c_blk = jnp.concatenate([diag_blocks(out_proj(c_re.astype(f32)), sw, LANES),
                             -diag_blocks(out_proj(c_im.astype(f32)), sw, LANES)],
                            axis=1).astype(BF16)
    return dict(b_blk=b_blk, c_blk=c_blk, a_re=abar_re.reshape(1, S5_NSTATE),
                a_im=abar_im.reshape(1, S5_NSTATE), d=d_skip.astype(f32).reshape(1, B_WIDTH))


def _s5_kernel(u_ref, x0re_ref, x0im_ref, bblk_ref, cblk_ref, are_ref, aim_ref, d_ref,
               y_ref, fre_ref, fim_ref, u_scr, bu_scr, xre_scr, xim_scr, *, tt, nb):
    step = pl.program_id(0)

    @pl.when(step == 0)
    def _():
        xre_scr[...] = x0re_ref[...]
        xim_scr[...] = x0im_ref[...]

    n_lt = B_WIDTH // LANES
    u_bt = u_ref[...].reshape(nb * tt, B_WIDTH).astype(F32)
    for c in range(n_lt):
        u_scr[c] = u_bt[:, c * LANES:(c + 1) * LANES]
    sw = S5_NSTATE // n_lt
    skips = []
    for c in range(n_lt):
        u_tb = jnp.concatenate([u_scr[c, pl.ds(t, nb, stride=tt), :] for t in range(tt)], axis=0)
        bu_scr[c] = jnp.dot(u_tb.astype(BF16), bblk_ref[c], preferred_element_type=F32)
        skips.append(d_ref[:, c * LANES:(c + 1) * LANES] * u_tb)

    for c0 in range(0, n_lt, S5_SCAN_BLOCKS):
        blocks = range(c0, c0 + S5_SCAN_BLOCKS)
        st_ls = [slice(c * sw, (c + 1) * sw) for c in blocks]
        a_res = [jnp.broadcast_to(are_ref[:, l], (nb, sw)) for l in st_ls]
        a_ims = [jnp.broadcast_to(aim_ref[:, l], (nb, sw)) for l in st_ls]

        def body(t, st, blocks=blocks, a_res=a_res, a_ims=a_ims):
            r0 = pl.multiple_of(t * nb, nb)
            out = []
            for c, a_re, a_im, (x_re, x_im) in zip(blocks, a_res, a_ims, st):
                n_re = a_re * x_re - a_im * x_im + bu_scr[c, pl.ds(r0, nb), :sw]
                n_im = a_re * x_im + a_im * x_re + bu_scr[c, pl.ds(r0, nb), sw:]
                bu_scr[c, pl.ds(r0, nb), :sw] = n_re
                bu_scr[c, pl.ds(r0, nb), sw:] = n_im
                out.append((n_re, n_im))
            return tuple(out)

        fin = lax.fori_loop(0, tt, body, tuple((xre_scr[:, l], xim_scr[:, l]) for l in st_ls))
        for l, (x_re, x_im) in zip(st_ls, fin):
            xre_scr[:, l] = x_re
            xim_scr[:, l] = x_im

    for c in range(n_lt):
        y = jnp.dot(bu_scr[c].astype(BF16), cblk_ref[c], preferred_element_type=F32) + skips[c]
        y = 0.5 * y * (1.0 + lax.erf(y * (1.0 / math.sqrt(2.0))))
        for t in range(tt):
            u_scr[c, pl.ds(t, nb, stride=tt), :] = y[nb * t:nb * (t + 1)]
    y_bt = jnp.concatenate([u_scr[c] for c in range(n_lt)], axis=1).reshape(nb, tt, B_WIDTH)
    y_ref[...] = y_bt.astype(y_ref.dtype)

    @pl.when(step == pl.num_programs(0) - 1)
    def _():
        fre_ref[...] = xre_scr[...]
        fim_ref[...] = xim_scr[...]


def s5(u, wts, x0_re, x0_im, bsz, t_len):
    nb = SUBLANES
    tt = min(S5_STEP_T, t_len)
    u3 = u.reshape(bsz, t_len, B_WIDTH)
    x0 = [x.astype(F32).reshape(bsz, S5_NSTATE) for x in (x0_re, x0_im)]
    if bsz < nb:
        u3 = jnp.pad(u3, ((0, nb - bsz), (0, 0), (0, 0)))
        x0 = [jnp.pad(x, ((0, nb - bsz), (0, 0))) for x in x0]
    fixed = lambda i: (0, 0)
    fixed3 = lambda i: (0, 0, 0)
    n_sb = B_WIDTH // LANES
    sw2 = 2 * S5_NSTATE // n_sb
    tok = pl.BlockSpec((nb, tt, B_WIDTH), lambda i: (0, i, 0))
    state = pl.BlockSpec((nb, S5_NSTATE), fixed)
    yg, fre, fim = pl.pallas_call(
        functools.partial(_s5_kernel, tt=tt, nb=nb),
        grid=(t_len // tt,),
        in_specs=[tok, state, state,
                  pl.BlockSpec((n_sb, LANES, sw2), fixed3), pl.BlockSpec((n_sb, sw2, LANES), fixed3),
                  pl.BlockSpec((1, S5_NSTATE), fixed), pl.BlockSpec((1, S5_NSTATE), fixed),
                  pl.BlockSpec((1, B_WIDTH), fixed)],
        out_specs=[tok, state, state],
        out_shape=[jax.ShapeDtypeStruct((nb, t_len, B_WIDTH), u.dtype),
                   jax.ShapeDtypeStruct((nb, S5_NSTATE), F32), jax.ShapeDtypeStruct((nb, S5_NSTATE), F32)],
        scratch_shapes=[pltpu.VMEM((n_sb, nb * tt, LANES), F32),
                        pltpu.VMEM((n_sb, nb * tt, sw2), F32),
                        pltpu.VMEM((nb, S5_NSTATE), F32), pltpu.VMEM((nb, S5_NSTATE), F32)],
        compiler_params=_cparams("arbitrary"),
        name="s5",
    )(u3, x0[0], x0[1], wts['b_blk'], wts['c_blk'], wts['a_re'], wts['a_im'], wts['d'])
    yg = yg[:bsz].reshape(bsz * t_len, B_WIDTH)
    return (yg, fre[:bsz].reshape(bsz, S5_GROUPS, S5_STATE), fim[:bsz].reshape(bsz, S5_GROUPS, S5_STATE))


def _sb_attn_kernel(*refs, heads, n_past, p_valid, scale):
    if n_past:
        q_ref, k_ref, v_ref, pk_ref, pv_ref, o_ref, carry_scr, acc_scr = refs
    else:
        q_ref, k_ref, v_ref, o_ref, carry_scr, acc_scr = refs
    blk = ATT_BLOCK
    i = pl.program_id(1)

    def suffix_matrix(tk):
        srow = lax.broadcasted_iota(jnp.int32, (2 * tk, tk), 0)
        scol = lax.broadcasted_iota(jnp.int32, (2 * tk, tk), 1)
        return jnp.where(jnp.where(srow >= tk, srow - tk, srow) > scol, 1.0, 0.0).astype(BF16)

    suffix = {tk: suffix_matrix(tk) for tk in (blk, ATT_SLAB)}
    rows = lax.broadcasted_iota(jnp.int32, (blk, blk), 0)
    cols = lax.broadcasted_iota(jnp.int32, (blk, blk), 1)

    carry_scr[...] = jnp.zeros_like(carry_scr)
    acc_scr[...] = jnp.zeros_like(acc_scr)
    head_lanes = [slice(hh * C_HD, (hh + 1) * C_HD) for hh in range(heads)]
    n_rows = heads * blk

    def per_head(x, fn):
        return jnp.concatenate([fn(x[hh * blk:(hh + 1) * blk]) for hh in range(heads)], axis=0)

    def rows_of(ref, rj, tk, hh):
        if len(ref.shape) == 2:
            return ref[pl.ds(rj, tk), head_lanes[hh]]
        return ref[pl.ds(rj, tk), hh, :].astype(BF16)

    def visit(kr, vr, rj, specs):
        zs = []
        for s, (tk, _) in enumerate(specs):
            q_rows = slice(s * blk, (s + 1) * blk)
            zs.append(jnp.concatenate(
                [lax.dot_general(q_ref[q_rows, l], rows_of(kr, rj, tk, hh), (((1,), (1,)), ((), ())),
                                 preferred_element_type=F32) for hh, l in enumerate(head_lanes)], axis=0) * scale)
        mids = []
        for z, (tk, mask) in zip(zs, specs):
            log_b = jnp.minimum(z, 0.0) - jnp.log(1.0 + jnp.exp(-jnp.abs(z)))
            x = log_b - z
            if mask is not None:
                x = per_head(x, lambda xh, mask=mask: jnp.where(mask, xh, 0.0))
            hi = x.astype(BF16)
            lo = (x - hi.astype(F32)).astype(BF16)
            cs = jnp.dot(jnp.concatenate([hi, lo], axis=1), suffix[tk], preferred_element_type=F32)
            total = jnp.broadcast_to(cs[:, 0:1] + x[:, 0:1], (n_rows, blk))
            mids.append((log_b, cs, total))
        live = None
        for s, ((log_b, cs, total), (tk, mask)) in enumerate(zip(mids, specs)):
            carry = carry_scr[s]
            w = jnp.exp(log_b + cs + jnp.concatenate([carry] * (tk // blk), axis=1))
            if mask is not None:
                w = per_head(w, lambda wh, mask=mask: jnp.where(mask, wh, 0.0))
            w = w.astype(BF16)
            acc_scr[s] += jnp.concatenate(
                [jnp.dot(w[hh * blk:(hh + 1) * blk], rows_of(vr, rj, tk, hh),
                         preferred_element_type=F32) for hh in range(heads)], axis=0)
            carry = carry + total
            carry_scr[s] = carry
            m = jnp.max(carry)
            live = m if live is None else jnp.maximum(live, m)
        return live

    rows2 = lax.broadcasted_iota(jnp.int32, (blk, ATT_SLAB), 0)
    cols2 = lax.broadcasted_iota(jnp.int32, (blk, ATT_SLAB), 1)
    live = visit(k_ref, v_ref, pl.multiple_of(i * ATT_SLAB, ATT_SLAB),
                 [(blk, cols < rows), (ATT_SLAB, cols2 < rows2 + blk)])

    def slab_body(st):
        j, _ = st
        return j - 2, visit(k_ref, v_ref, pl.multiple_of((j - 1) * blk, blk), [(ATT_SLAB, None)] * 2)

    def alive_from(first):
        return lambda st: jnp.logical_and(st[0] >= first, st[1] > ATT_DEAD_LOG_WEIGHT)

    _, live = lax.while_loop(alive_from(1), slab_body, (2 * i - 1, live))

    if n_past:
        def past_body(st):
            j, _ = st
            return j - 1, visit(pk_ref, pv_ref, pl.multiple_of(j * blk, blk),
                                [(blk, (cols + j * blk) < p_valid)] * 2)

        lax.while_loop(alive_from(0), past_body, (jnp.int32(n_past - 1), live))

    for s in range(2):
        for hh in range(heads):
            o_ref[s * blk:(s + 1) * blk, head_lanes[hh]] = (
                acc_scr[s, hh * blk:(hh + 1) * blk, :].astype(o_ref.dtype))


def sb_attention(q, k, v, past_k, past_v, p_valid):
    bsz, t_len, width = q.shape
    n_past = 0 if past_k is None else past_k.shape[1] // ATT_BLOCK
    q_spec = pl.BlockSpec((None, ATT_SLAB, width), lambda b, i: (b, i, 0))
    kv_spec = pl.BlockSpec((None, t_len, width), lambda b, i: (b, 0, 0))
    in_specs = [q_spec, kv_spec, kv_spec]
    args = [q, k, v]
    if n_past:
        pb = (lambda b: b) if past_k.shape[0] == bsz else (lambda b: 0)
        p_spec = pl.BlockSpec((None,) + past_k.shape[1:], lambda b, i: (pb(b), 0, 0, 0))
        in_specs += [p_spec, p_spec]
        args += [past_k, past_v]
    return pl.pallas_call(
        functools.partial(_sb_attn_kernel, heads=C_HEADS, n_past=n_past, p_valid=p_valid,
                          scale=C_HD ** -0.5),
        grid=(bsz, t_len // ATT_SLAB),
        in_specs=in_specs,
        out_specs=q_spec,
        out_shape=jax.ShapeDtypeStruct((bsz, t_len, width), BF16),
        scratch_shapes=[pltpu.VMEM((2, C_HEADS * ATT_BLOCK, C_HD), F32)] * 2,
        compiler_params=_cparams("parallel", "arbitrary"),
        name="sb_attention",
    )(*args)


def _pad_rows(x, mult):
    t = x.shape[1]
    tp = -(-t // mult) * mult
    return x if tp == t else jnp.pad(x, ((0, 0), (0, tp - t), (0, 0)))


def _run_stream(x, bsz, t_len, wts, hgrn_s0_t, ssm0_re, ssm0_im, past_k, past_v, p_valid, lead_kv=None,
                starts_sequence=False):
    act = BF16 if t_len % HGRN_CHUNK == 0 else F32
    q, zf, iv, g, u = norm_matmul(
        x, wts['ln_mix'][0], wts['w_in_even'],
        [(0, A_QK, act), (A_QK, A_QK, F32), (2 * A_QK, A_WIDTH, act),
         (2 * A_QK + A_WIDTH, A_WIDTH, F32), (2 * A_QK + 2 * A_WIDTH, B_WIDTH, act)], tm=1024)
    o_a, s_t = hgrn(q, zf, iv, g, wts['lb'], wts['hgrn_norm'], hgrn_s0_t, bsz, t_len,
                    zero_start=starts_sequence)
    yg, x_re, x_im = s5(u, wts['s5'], ssm0_re, ssm0_im, bsz, t_len)
    x = layer_tail(x, [o_a, yg], [wts['w_glu'], wts['w_out_even_a'], wts['w_out_even_b']],
                   wts['ln_mlp'][0], wts['w_up'][0], wts['w_down'][0], wts['ln_final'], final_norm=False)
    if lead_kv is None:
        qb, kb, vb, k32, v32 = norm_matmul(
            x, wts['ln_mix'][1], wts['w_in_odd'],
            [(0, D_MODEL, BF16), (D_MODEL, D_MODEL, BF16), (2 * D_MODEL, D_MODEL, BF16),
             (D_MODEL, D_MODEL, F32), (2 * D_MODEL, D_MODEL, F32)])
    else:
        qb, kb, vb, k32, v32 = qkv_shifted(x, wts['ln_mix'][1], wts['w_in_odd'], *lead_kv, bsz, t_len)
    shp = (bsz, t_len, D_MODEL)
    o = sb_attention(_pad_rows(qb.reshape(shp), ATT_SLAB), _pad_rows(kb.reshape(shp), ATT_SLAB),
                     _pad_rows(vb.reshape(shp), ATT_SLAB), past_k, past_v, p_valid)
    o = o[:, :t_len].reshape(bsz * t_len, D_MODEL)
    y = layer_tail(x, [o], [wts['w_out_odd']],
                   wts['ln_mlp'][1], wts['w_up'][1], wts['w_down'][1], wts['ln_final'], final_norm=True)
    return y, s_t, x_re, x_im, k32, v32


def kernel(x_prompt, x_sample, state_hgrn, state_ssm_re, state_ssm_im, cache_k, cache_v, meta_tokens,
           ln_mix, ln_mlp, ln_final, w_in_even, hgrn_lb, hgrn_norm, ssm_a_re, ssm_a_im, ssm_log_dt,
           ssm_b_re, ssm_b_im, ssm_c_re, ssm_c_im, ssm_d, w_glu, w_out_even, w_in_odd, w_out_odd,
           w_up, w_down):
    bsz, seq, _ = x_prompt.shape
    dbsz, dseq, _ = x_sample.shape
    past = cache_k.shape[2]
    lb_all = jnp.cumsum(jax.nn.softmax(hgrn_lb.astype(F32), axis=0), axis=0)
    w_out_e = w_out_even[0].astype(BF16)
    wts = dict(
        ln_mix=ln_mix.astype(F32), ln_mlp=ln_mlp.astype(F32), ln_final=ln_final.astype(F32),
        w_in_even=w_in_even[0].astype(BF16), lb=lb_all[0], hgrn_norm=hgrn_norm[0].astype(F32),
        s5=s5_weights(ssm_a_re[0], ssm_a_im[0], ssm_log_dt[0], ssm_b_re[0], ssm_b_im[0],
                      ssm_c_re[0], ssm_c_im[0], ssm_d[0]),
        w_glu=w_glu[0].astype(BF16), w_out_even_a=w_out_e[:A_WIDTH], w_out_even_b=w_out_e[A_WIDTH:],
        w_in_odd=w_in_odd[0].astype(BF16), w_out_odd=w_out_odd[0].astype(BF16),
        w_up=w_up.astype(BF16), w_down=w_down.astype(BF16))

    zeros_s = jnp.zeros((1, A_HEADS, A_DV, A_DK), F32)
    zeros_x = jnp.zeros((1, S5_GROUPS, S5_STATE), F32)
    _, m_s, m_re, m_im, m_k, m_v = _run_stream(
        meta_tokens.astype(F32), 1, N_META, wts, zeros_s, zeros_x, zeros_x, None, None, 0, starts_sequence=True)

    m_kp = _pad_rows(m_k[None], ATT_BLOCK).reshape(1, -1, C_HEADS, C_HD)
    m_vp = _pad_rows(m_v[None], ATT_BLOCK).reshape(1, -1, C_HEADS, C_HD)
    y_p, s_p, re_p, im_p, k_p, v_p = _run_stream(
        x_prompt.reshape(bsz * seq, D_MODEL), bsz, seq, wts, m_s,
        jnp.broadcast_to(m_re, (bsz, S5_GROUPS, S5_STATE)), jnp.broadcast_to(m_im, (bsz, S5_GROUPS, S5_STATE)),
        m_kp, m_vp, N_META, lead_kv=(m_k, m_v))

    pad_past = ((0, 0), (0, -past % ATT_BLOCK), (0, 0), (0, 0))
    y_s, s_s, re_s, im_s, k_s, v_s = _run_stream(
        x_sample.reshape(dbsz * dseq, D_MODEL), dbsz, dseq, wts,
        jnp.swapaxes(state_hgrn[0].astype(F32), -1, -2), state_ssm_re[0], state_ssm_im[0],
        jnp.pad(cache_k[0].astype(F32), pad_past), jnp.pad(cache_v[0].astype(F32), pad_past), past)

    kv_shape = (1, bsz, N_META + seq, C_HEADS, C_HD)
    return (y_p.reshape(bsz, seq, D_MODEL), y_s.reshape(dbsz, dseq, D_MODEL),
            jnp.swapaxes(s_p, -1, -2)[None], jnp.swapaxes(s_s, -1, -2)[None],
            re_p[None], im_p[None], re_s[None], im_s[None],
            k_p.reshape(kv_shape), v_p.reshape(kv_shape),
            k_s.reshape(1, dbsz, dseq, C_HEADS, C_HD), v_s.reshape(1, dbsz, dseq, C_HEADS, C_HD))
```
